```python
import math
import jax, jax.numpy as jnp
from jax import lax
import numpy as np

D_MODEL = 1024
BATCH = 8
SEQ = 2048
DEPTH = 2

CHUNK = 64
Q_BLOCK = 128
N_MIXERS = 2
ATTN_HEADS = 8
ATTN_HEAD_DIM = 64
SSM_GROUP = 16
SSM_GROUPS = D_MODEL // SSM_GROUP
SSM_STATE = 64
D_FF = 2816
ALPHA = (2 * DEPTH) ** 0.25
BETA = (8 * DEPTH) ** -0.25
N_ATTN_LAYERS = (DEPTH + 1) // 2
N_SSM_LAYERS = DEPTH // 2
LN_EPS = 1e-5
DT_MIN = 1e-3
DT_MAX = 1e-1

kernel_name = "hybrid_diffattn_s5_macaron_deepnorm_adaln"


def layer_norm(x, g, b):
    xf = x.astype(jnp.float32)
    mu = jnp.mean(xf, axis=-1, keepdims=True)
    var = jnp.mean(jnp.square(xf - mu), axis=-1, keepdims=True)
    y = (xf - mu) * lax.rsqrt(var + LN_EPS) * g.astype(jnp.float32) + b.astype(jnp.float32)
    return y.astype(x.dtype)


def modulate(x, shift, scale):
    return x * (1.0 + scale[:, None, :]) + shift[:, None, :]


def post_norm_residual(x, y, gate, g, b):
    return layer_norm(ALPHA * x + gate[:, None, :] * y, g, b)


def swiglu(h, w1, w3, w2):
    return (jax.nn.silu(h @ w1) * (h @ w3)) @ w2


def diff_attention(h, w_in, lam, subln_g, w_out, lam_init):
    bsz, seq, _ = h.shape
    hd = ATTN_HEAD_DIM
    q, k, v = jnp.split(h @ w_in, 3, axis=-1)
    q = q.reshape(bsz, seq, ATTN_HEADS, 2, hd)
    k = k.reshape(bsz, seq, ATTN_HEADS, 2, hd)
    v = v.reshape(bsz, seq, ATTN_HEADS, 2 * hd)
    lam_f = lam.astype(jnp.float32)
    lam_full = (jnp.exp(jnp.sum(lam_f[0] * lam_f[1]))
                - jnp.exp(jnp.sum(lam_f[2] * lam_f[3])) + lam_init)
    scale = hd ** -0.5
    outs = []
    for s0 in range(0, seq, Q_BLOCK):
        kend = s0 + Q_BLOCK
        scores = jnp.einsum('bqhmd,bkhmd->bhmqk', q[:, s0:kend], k[:, :kend])
        scores = scores.astype(jnp.float32) * scale
        q_chunk = (s0 + jnp.arange(Q_BLOCK)) // CHUNK
        k_chunk = jnp.arange(kend) // CHUNK
        allowed = k_chunk[None, :] <= q_chunk[:, None]
        scores = jnp.where(allowed, scores, -jnp.inf)
        p = jax.nn.softmax(scores, axis=-1)
        attn = p[:, :, 0] - lam_full * p[:, :, 1]
        outs.append(jnp.einsum('bhqk,bkhe->bqhe', attn.astype(v.dtype), v[:, :kend]))
    o = jnp.concatenate(outs, axis=1).astype(jnp.float32)
    o = o * lax.rsqrt(jnp.mean(jnp.square(o), axis=-1, keepdims=True) + LN_EPS)
    o = o * subln_g.astype(jnp.float32) * (1.0 - lam_init)
    return o.astype(h.dtype).reshape(bsz, seq, ATTN_HEADS * 2 * hd) @ w_out


def _complex_affine_combine(e1, e2):
    a1r, a1i, b1r, b1i = e1
    a2r, a2i, b2r, b2i = e2
    ar = a1r * a2r - a1i * a2i
    ai = a1r * a2i + a1i * a2r
    br = a2r * b1r - a2i * b1i + b2r
    bi = a2r * b1i + a2i * b1r + b2i
    return (ar, ai, br, bi)


def s5_scan(u, a_re, a_im, log_dt, b_re, b_im, c_re, c_im):
    bsz, seq, g, p = u.shape
    n_chunks = seq // CHUNK
    f32 = jnp.float32
    a_re, a_im = a_re.astype(f32), a_im.astype(f32)
    b_re, b_im = b_re.astype(f32), b_im.astype(f32)
    c_re, c_im = c_re.astype(f32), c_im.astype(f32)
    dt = jnp.exp(log_dt.astype(f32))[:, None]
    mag = jnp.exp(a_re * dt)
    abar_re, abar_im = mag * jnp.cos(a_im * dt), mag * jnp.sin(a_im * dt)
    den = a_re * a_re + a_im * a_im
    pr, pi_ = abar_re - 1.0, abar_im
    coef_re = (pr * a_re + pi_ * a_im) / den
    coef_im = (pi_ * a_re - pr * a_im) / den
    bbar_re = coef_re[..., None] * b_re - coef_im[..., None] * b_im
    bbar_im = coef_re[..., None] * b_im + coef_im[..., None] * b_re
    steps = jnp.arange(1, CHUNK + 1, dtype=f32)[:, None, None]
    pmag = jnp.exp(a_re[None] * dt[None] * steps)
    pow_re = pmag * jnp.cos(a_im[None] * dt[None] * steps)
    pow_im = pmag * jnp.sin(a_im[None] * dt[None] * steps)
    a_b_re = jnp.broadcast_to(abar_re, (CHUNK, bsz, g, SSM_STATE))
    a_b_im = jnp.broadcast_to(abar_im, (CHUNK, bsz, g, SSM_STATE))
    uc = u.reshape(bsz, n_chunks, CHUNK, g, p).transpose(1, 2, 0, 3, 4)

    def step(carry, u_chunk):
        h_re, h_im = carry
        bu_re = jnp.einsum('lbgp,gnp->lbgn', u_chunk, bbar_re)
        bu_im = jnp.einsum('lbgp,gnp->lbgn', u_chunk, bbar_im)
        _, _, s_re, s_im = lax.associative_scan(
            _complex_affine_combine, (a_b_re, a_b_im, bu_re, bu_im), axis=0)
        hr = s_re + pow_re[:, None] * h_re[None] - pow_im[:, None] * h_im[None]
        hi = s_im + pow_re[:, None] * h_im[None] + pow_im[:, None] * h_re[None]
        y = (jnp.einsum('lbgn,gpn->lbgp', hr, c_re)
             - jnp.einsum('lbgn,gpn->lbgp', hi, c_im))
        return (hr[-1], hi[-1]), y

    init = (jnp.zeros((bsz, g, SSM_STATE), f32), jnp.zeros((bsz, g, SSM_STATE), f32))
    _, ys = lax.scan(step, init, uc)
    return ys.transpose(2, 0, 1, 3, 4).reshape(bsz, seq, g, p)


def s5_mixer(h, w_in, a_re, a_im, log_dt, b_re, b_im, c_re, c_im, d, w_gate, w_out):
    bsz, seq, _ = h.shape
    u = (h @ w_in).reshape(bsz, seq, SSM_GROUPS, SSM_GROUP).astype(jnp.float32)
    y = s5_scan(u, a_re, a_im, log_dt, b_re, b_im, c_re, c_im) + d.astype(jnp.float32) * u
    z = jax.nn.gelu(y.reshape(bsz, seq, D_MODEL)).astype(h.dtype)
    z = z * jax.nn.sigmoid(z @ w_gate)
    return z @ w_out


def setup_inputs(seed: int = 0) -> dict:
    key = jax.random.key(seed)
    ks = iter(jax.random.split(key, 32))
    f32 = jnp.float32
    D, F = D_MODEL, D_FF
    NA, NS, G, N, P = N_ATTN_LAYERS, N_SSM_LAYERS, SSM_GROUPS, SSM_STATE, SSM_GROUP
    nrm = lambda shape, std: std * jax.random.normal(next(ks), shape, f32)
    x = nrm((BATCH, SEQ, D), 1.0)
    c = nrm((BATCH, D), 1.0)
    ada_w = nrm((DEPTH, D, 9 * D), 0.5 * D ** -0.5)
    ada_b = nrm((DEPTH, 9 * D), 0.01)
    ln_g = 1.0 + nrm((DEPTH, 3, D), 0.02)
    ln_b = nrm((DEPTH, 3, D), 0.02)
    ffn_w1 = nrm((DEPTH, 2, D, F), D ** -0.5)
    ffn_w3 = nrm((DEPTH, 2, D, F), D ** -0.5)
    ffn_w2 = nrm((DEPTH, 2, F, D), BETA * F ** -0.5)
    qk = nrm((NA, D, 2 * D), D ** -0.5)
    vv = nrm((NA, D, D), BETA * D ** -0.5)
    attn_w_in = jnp.concatenate([qk, vv], axis=-1)
    attn_lam = nrm((NA, 4, ATTN_HEAD_DIM), 0.1)
    attn_subln_g = 1.0 + nrm((NA, 2 * ATTN_HEAD_DIM), 0.02)
    attn_w_out = nrm((NA, D, D), BETA * D ** -0.5)
    ssm_w_in = nrm((NS, D, D), D ** -0.5)
    ssm_a_re = -0.5 + nrm((NS, G, N), 0.01)
    ssm_a_im = math.pi * jnp.arange(N, dtype=f32)[None, None, :] + nrm((NS, G, N), 0.01)
    ssm_log_dt = jax.random.uniform(next(ks), (NS, G), f32,
                                    minval=math.log(DT_MIN), maxval=math.log(DT_MAX))
    ssm_b_re = nrm((NS, G, N, P), (2 * P) ** -0.5)
    ssm_b_im = nrm((NS, G, N, P), (2 * P) ** -0.5)
    ssm_c_re = nrm((NS, G, P, N), (2 * N) ** -0.5)
    ssm_c_im = nrm((NS, G, P, N), (2 * N) ** -0.5)
    ssm_d = nrm((NS, G, P), 1.0)
    ssm_w_gate = nrm((NS, D, D), D ** -0.5)
    ssm_w_out = nrm((NS, D, D), BETA * D ** -0.5)
    return {"x": x, "c": c, "ada_w": ada_w, "ada_b": ada_b, "ln_g": ln_g, "ln_b": ln_b,
            "ffn_w1": ffn_w1, "ffn_w3": ffn_w3, "ffn_w2": ffn_w2,
            "attn_w_in": attn_w_in, "attn_lam": attn_lam, "attn_subln_g": attn_subln_g,
            "attn_w_out": attn_w_out, "ssm_w_in": ssm_w_in, "ssm_a_re": ssm_a_re,
            "ssm_a_im": ssm_a_im, "ssm_log_dt": ssm_log_dt, "ssm_b_re": ssm_b_re,
            "ssm_b_im": ssm_b_im, "ssm_c_re": ssm_c_re, "ssm_c_im": ssm_c_im,
            "ssm_d": ssm_d, "ssm_w_gate": ssm_w_gate, "ssm_w_out": ssm_w_out}


def reference(x, c, ada_w, ada_b, ln_g, ln_b, ffn_w1, ffn_w3, ffn_w2,
              attn_w_in, attn_lam, attn_subln_g, attn_w_out,
              ssm_w_in, ssm_a_re, ssm_a_im, ssm_log_dt, ssm_b_re, ssm_b_im,
              ssm_c_re, ssm_c_im, ssm_d, ssm_w_gate, ssm_w_out):
    bsz = x.shape[0]
    cond = jax.nn.silu(c)
    for layer in range(DEPTH):
        mods = (cond @ ada_w[layer] + ada_b[layer]).reshape(bsz, 3, 3, D_MODEL)
        shift, scale, gate = mods[:, :, 0], mods[:, :, 1], 1.0 + mods[:, :, 2]
        h = modulate(x, shift[:, 0], scale[:, 0])
        y = 0.5 * swiglu(h, ffn_w1[layer, 0], ffn_w3[layer, 0], ffn_w2[layer, 0])
        x = post_norm_residual(x, y, gate[:, 0], ln_g[layer, 0], ln_b[layer, 0])
        h = modulate(x, shift[:, 1], scale[:, 1])
        i = layer // N_MIXERS
        if layer % N_MIXERS == 0:
            lam_init = 0.8 - 0.6 * math.exp(-0.3 * layer)
            y = diff_attention(h, attn_w_in[i], attn_lam[i], attn_subln_g[i],
                               attn_w_out[i], lam_init)
        else:
            y = s5_mixer(h, ssm_w_in[i], ssm_a_re[i], ssm_a_im[i], ssm_log_dt[i],
                         ssm_b_re[i], ssm_b_im[i], ssm_c_re[i], ssm_c_im[i], ssm_d[i],
                         ssm_w_gate[i], ssm_w_out[i])
        x = post_norm_residual(x, y, gate[:, 1], ln_g[layer, 1], ln_b[layer, 1])
        h = modulate(x, shift[:, 2], scale[:, 2])
        y = 0.5 * swiglu(h, ffn_w1[layer, 1], ffn_w3[layer, 1], ffn_w2[layer, 1])
        x = post_norm_residual(x, y, gate[:, 2], ln_g[layer, 2], ln_b[layer, 2])
    return x
```

```python
import functools
import math

import jax
import jax.numpy as jnp
from jax import lax
from jax.experimental import pallas as pl
from jax.experimental.pallas import tpu as pltpu

D_MODEL = 1024
DEPTH = 2
CHUNK = 64
ATTN_HEADS = 8
ATTN_HEAD_DIM = 64
SSM_GROUP = 16
SSM_GROUPS = D_MODEL // SSM_GROUP
SSM_STATE = 64
D_FF = 2816
ALPHA = (2 * DEPTH) ** 0.25
LN_EPS = 1e-5

LANES = 128
F_CHUNK = 256
MIB = 1024 * 1024

F32 = jnp.float32
BF16 = jnp.bfloat16


def _params(sem, vmem_mib):
    return pltpu.CompilerParams(dimension_semantics=sem, vmem_limit_bytes=vmem_mib * MIB)


def _resident(block_shape, index_map):
    return pl.BlockSpec(block_shape, index_map, pipeline_mode=pl.Buffered(1))


def _layer_norm(r, g, b):
    mu = jnp.mean(r, axis=-1, keepdims=True)
    d = r - mu
    var = jnp.mean(d * d, axis=-1, keepdims=True)
    return d * lax.rsqrt(var + LN_EPS) * g + b


def _mods(mod_ref, sub, bidx):
    shift = mod_ref[3 * sub + 0, pl.ds(bidx, 1), :]
    scale = mod_ref[3 * sub + 1, pl.ds(bidx, 1), :]
    gate = 1.0 + mod_ref[3 * sub + 2, pl.ds(bidx, 1), :]
    return shift, scale, gate


def _ada_kernel(c_ref, w_ref, b_ref, o_ref):
    c = c_ref[...]
    cond = (c * jax.nn.sigmoid(c)).astype(BF16)
    o_ref[...] = jnp.dot(cond, w_ref[...].astype(BF16), preferred_element_type=F32) + b_ref[...]


def _ada(c, ada_w, ada_b):
    bsz = c.shape[0]
    n_blk = ada_w.shape[2] // D_MODEL
    return pl.pallas_call(
        _ada_kernel,
        grid=(DEPTH, n_blk),
        in_specs=[
            pl.BlockSpec((bsz, D_MODEL), lambda l, n: (0, 0)),
            pl.BlockSpec((None, D_MODEL, D_MODEL), lambda l, n: (l, 0, n)),
            pl.BlockSpec((None, None, 1, D_MODEL), lambda l, n: (l, n, 0, 0)),
        ],
        out_specs=pl.BlockSpec((None, None, bsz, D_MODEL), lambda l, n: (l, n, 0, 0)),
        out_shape=jax.ShapeDtypeStruct((DEPTH, n_blk, bsz, D_MODEL), F32),
        compiler_params=_params(("arbitrary", "arbitrary"), 24),
        name="ada_mods",
    )(c, ada_w, ada_b.reshape(DEPTH, n_blk, 1, D_MODEL))


def _ffn_kernel(x_ref, mod_ref, w1_ref, w3_ref, w2_ref, g_ref, b_ref, o_ref, *, sub):
    shift, scale, gate = _mods(mod_ref, sub, pl.program_id(0))
    x = x_ref[...]
    h = (x * (1.0 + scale) + shift).astype(BF16)
    acc = jnp.zeros(x.shape, F32)
    for f in range(D_FF // F_CHUNK):
        sl = slice(f * F_CHUNK, (f + 1) * F_CHUNK)
        a = jnp.dot(h, w1_ref[:, sl], preferred_element_type=F32)
        b = jnp.dot(h, w3_ref[:, sl], preferred_element_type=F32)
        u = (a * jax.nn.sigmoid(a) * b).astype(BF16)
        acc = acc + jnp.dot(u, w2_ref[sl, :], preferred_element_type=F32)
    r = ALPHA * x + gate * (0.5 * acc)
    o_ref[...] = _layer_norm(r, g_ref[...], b_ref[...])


def _ffn(x, mods, w1, w3, w2, ln_g, ln_b, *, layer, half, sub, tm=512):
    bsz, seq, _ = x.shape
    wmap = lambda b, i: (layer, half, 0, 0)
    return pl.pallas_call(
        functools.partial(_ffn_kernel, sub=sub),
        grid=(bsz, seq // tm),
        in_specs=[
            pl.BlockSpec((None, tm, D_MODEL), lambda b, i: (b, i, 0)),
            _resident((None, 9, bsz, D_MODEL), lambda b, i: (layer, 0, 0, 0)),
            _resident((None, None, D_MODEL, D_FF), wmap),
            _resident((None, None, D_MODEL, D_FF), wmap),
            _resident((None, None, D_FF, D_MODEL), wmap),
            _resident((None, None, 1, D_MODEL), lambda b, i: (layer, sub, 0, 0)),
            _resident((None, None, 1, D_MODEL), lambda b, i: (layer, sub, 0, 0)),
        ],
        out_specs=pl.BlockSpec((None, tm, D_MODEL), lambda b, i: (b, i, 0)),
        out_shape=jax.ShapeDtypeStruct(x.shape, F32),
        compiler_params=_params(("arbitrary", "arbitrary"), 48),
        name=f"ffn_l{layer}_h{half}",
    )(x, mods, w1, w3, w2, ln_g, ln_b)


def _modproj_kernel(x_ref, mod_ref, w_ref, o_ref, *, sub):
    shift, scale, _ = _mods(mod_ref, sub, pl.program_id(0))
    h = (x_ref[...] * (1.0 + scale) + shift).astype(BF16)
    o_ref[...] = jnp.dot(h, w_ref[...], preferred_element_type=F32).astype(o_ref.dtype)


def _modproj(x, mods, w, *, layer, sub, tm=512):
    bsz, seq, _ = x.shape
    n_out = w.shape[1]
    return pl.pallas_call(
        functools.partial(_modproj_kernel, sub=sub),
        grid=(bsz, seq // tm),
        in_specs=[
            pl.BlockSpec((None, tm, D_MODEL), lambda b, i: (b, i, 0)),
            _resident((None, 9, bsz, D_MODEL), lambda b, i: (layer, 0, 0, 0)),
            _resident((D_MODEL, n_out), lambda b, i: (0, 0)),
        ],
        out_specs=pl.BlockSpec((None, tm, n_out), lambda b, i: (b, i, 0)),
        out_shape=jax.ShapeDtypeStruct((bsz, seq, n_out), BF16),
        compiler_params=_params(("arbitrary", "arbitrary"), 32),
        name=f"modproj_l{layer}",
    )(x, mods, w)


def _projres_kernel(a_ref, x_ref, mod_ref, w_ref, g_ref, b_ref, o_ref, *, sub):
    _, _, gate = _mods(mod_ref, sub, pl.program_id(0))
    y = jnp.dot(a_ref[...], w_ref[...], preferred_element_type=F32)
    o_ref[...] = _layer_norm(ALPHA * x_ref[...] + gate * y, g_ref[...], b_ref[...])


def _projres(a, x, mods, w, ln_g, ln_b, *, layer, sub, tm=512):
    bsz, seq, _ = x.shape
    return pl.pallas_call(
        functools.partial(_projres_kernel, sub=sub),
        grid=(bsz, seq // tm),
        in_specs=[
            pl.BlockSpec((None, tm, D_MODEL), lambda b, i: (b, i, 0)),
            pl.BlockSpec((None, tm, D_MODEL), lambda b, i: (b, i, 0)),
            _resident((None, 9, bsz, D_MODEL), lambda b, i: (layer, 0, 0, 0)),
            _resident((D_MODEL, D_MODEL), lambda b, i: (0, 0)),
            _resident((None, None, 1, D_MODEL), lambda b, i: (layer, sub, 0, 0)),
            _resident((None, None, 1, D_MODEL), lambda b, i: (layer, sub, 0, 0)),
        ],
        out_specs=pl.BlockSpec((None, tm, D_MODEL), lambda b, i: (b, i, 0)),
        out_shape=jax.ShapeDtypeStruct(x.shape, F32),
        compiler_params=_params(("arbitrary", "arbitrary"), 32),
        name=f"projres_l{layer}",
    )(a, x, mods, w, ln_g, ln_b)


def _attn_kernel(q_ref, k_ref, v_ref, lam_ref, sg_ref, o_ref, *, lam_init, tq):
    qi = pl.program_id(2)
    hd = ATTN_HEAD_DIM
    lam = lam_ref[...]
    lam_full = (jnp.exp(jnp.sum(lam[0:1] * lam[1:2], axis=-1, keepdims=True))
                - jnp.exp(jnp.sum(lam[2:3] * lam[3:4], axis=-1, keepdims=True)) + lam_init)

    q = q_ref[...].astype(F32) * (hd ** -0.5)
    lane = lax.broadcasted_iota(jnp.int32, q.shape, 1)
    q1 = jnp.where(lane < hd, q, 0.0).astype(BF16)
    q2 = jnp.where(lane >= hd, q, 0.0).astype(BF16)

    nt = (((1,), (1,)), ((), ()))

    def update(s, v, m, l, acc):
        m_new = jnp.maximum(m, jnp.max(s, axis=-1, keepdims=True))
        p = jnp.exp(s - m_new)
        corr = jnp.exp(m - m_new)
        l_new = corr * l + jnp.sum(p, axis=-1, keepdims=True)
        acc_new = corr * acc + jnp.dot(p.astype(BF16), v, preferred_element_type=F32)
        return m_new, l_new, acc_new

    def tile(j, carry, mask):
        m1, l1, a1, m2, l2, a2 = carry
        start = pl.multiple_of(j * tq, tq)
        k = k_ref[pl.ds(start, tq), :]
        v = v_ref[pl.ds(start, tq), :]
        s1 = lax.dot_general(q1, k, nt, preferred_element_type=F32)
        s2 = lax.dot_general(q2, k, nt, preferred_element_type=F32)
        if mask is not None:
            s1 = jnp.where(mask, s1, -jnp.inf)
            s2 = jnp.where(mask, s2, -jnp.inf)
        m1, l1, a1 = update(s1, v, m1, l1, a1)
        m2, l2, a2 = update(s2, v, m2, l2, a2)
        return m1, l1, a1, m2, l2, a2

    neg = jnp.full((tq, 1), -jnp.inf, F32)
    zero = jnp.zeros((tq, 1), F32)
    zacc = jnp.zeros((tq, 2 * hd), F32)
    carry = (neg, zero, zacc, neg, zero, zacc)
    carry = lax.fori_loop(0, qi, lambda j, c: tile(j, c, None), carry)
    rq = lax.broadcasted_iota(jnp.int32, (tq, tq), 0) // CHUNK
    ck = lax.broadcasted_iota(jnp.int32, (tq, tq), 1) // CHUNK
    m1, l1, a1, m2, l2, a2 = tile(qi, carry, ck <= rq)

    o = a1 / l1 - lam_full * (a2 / l2)
    o = o * lax.rsqrt(jnp.mean(o * o, axis=-1, keepdims=True) + LN_EPS)
    o_ref[...] = (o * sg_ref[...] * (1.0 - lam_init)).astype(o_ref.dtype)


def _attention(qkv, lam, subln_g, *, lam_init, tq=256):
    bsz, seq, _ = qkv.shape
    hw = 2 * ATTN_HEAD_DIM
    return pl.pallas_call(
        functools.partial(_attn_kernel, lam_init=lam_init, tq=tq),
        grid=(bsz, ATTN_HEADS, seq // tq),
        in_specs=[
            pl.BlockSpec((None, tq, hw), lambda b, h, i: (b, i, h)),
            pl.BlockSpec((None, seq, hw), lambda b, h, i: (b, 0, ATTN_HEADS + h)),
            pl.BlockSpec((None, seq, hw), lambda b, h, i: (b, 0, 2 * ATTN_HEADS + h)),
            _resident((4, ATTN_HEAD_DIM), lambda b, h, i: (0, 0)),
            _resident((1, hw), lambda b, h, i: (0, 0)),
        ],
        out_specs=pl.BlockSpec((None, tq, hw), lambda b, h, i: (b, i, h)),
        out_shape=jax.ShapeDtypeStruct((bsz, seq, D_MODEL), BF16),
        compiler_params=_params(("arbitrary", "arbitrary", "arbitrary"), 24),
        name="diff_attention",
    )(qkv, qkv, qkv, lam, subln_g.reshape(1, hw))


ROWS = 2 * CHUNK
PAIRS = SSM_GROUP // 2


def _lane_halves(e, o, lane):
    lo = jnp.where(lane < CHUNK, e, pltpu.roll(o, CHUNK, 1))
    hi = jnp.where(lane < CHUNK, pltpu.roll(e, CHUNK, 1), o)
    return lo, hi


def _s5in_kernel(x_ref, mod_ref, w_ref, o_ref, a_ref, s_ref):
    bsz = x_ref.shape[0]
    shift = mod_ref[3, :, :][:, None, :]
    scale = mod_ref[4, :, :][:, None, :]
    h = (x_ref[...] * (1.0 + scale) + shift).astype(BF16).reshape(bsz * ROWS, D_MODEL)
    a_ref[...] = lax.dot_general(w_ref[...], h, (((1,), (1,)), ((), ())), preferred_element_type=F32)
    lane = lax.broadcasted_iota(jnp.int32, (SSM_GROUPS, LANES), 1)
    half = SSM_GROUPS * bsz
    for q in range(PAIRS):
        for b in range(bsz):
            x0 = a_ref[(2 * q) * SSM_GROUPS:(2 * q + 1) * SSM_GROUPS, b * LANES:(b + 1) * LANES]
            x1 = a_ref[(2 * q + 1) * SSM_GROUPS:(2 * q + 2) * SSM_GROUPS, b * LANES:(b + 1) * LANES]
            c0, c1 = _lane_halves(x0, x1, lane)
            s_ref[q, pl.ds(b, SSM_GROUPS, stride=bsz), :] = c0
            s_ref[q, pl.ds(half + b, SSM_GROUPS, stride=bsz), :] = c1
    for c2 in range(2):
        for q in range(PAIRS):
            o_ref[c2, :, :, q * LANES:(q + 1) * LANES] = (
                s_ref[q, c2 * half:(c2 + 1) * half, :].reshape(SSM_GROUPS, bsz, LANES))


def _s5in(x, mods, w_t, *, layer):
    bsz, seq, _ = x.shape
    width = SSM_GROUP * CHUNK
    return pl.pallas_call(
        _s5in_kernel,
        grid=(seq // ROWS,),
        in_specs=[
            pl.BlockSpec((bsz, ROWS, D_MODEL), lambda j: (0, j, 0)),
            _resident((None, 9, bsz, D_MODEL), lambda j: (layer, 0, 0, 0)),
            _resident((D_MODEL, D_MODEL), lambda j: (0, 0)),
        ],
        out_specs=pl.BlockSpec((2, SSM_GROUPS, bsz, width), lambda j: (j, 0, 0, 0)),
        out_shape=jax.ShapeDtypeStruct((seq // CHUNK, SSM_GROUPS, bsz, width), F32),
        scratch_shapes=[
            pltpu.VMEM((D_MODEL, bsz * ROWS), F32),
            pltpu.VMEM((PAIRS, 2 * SSM_GROUPS * bsz, LANES), F32),
        ],
        compiler_params=_params(("arbitrary",), 40),
        name="s5_in",
    )(x, mods, w_t)


def _s5scan_kernel(u_ref, arr_ref, air_ref, arc_ref, aic_ref, ldt_ref, btr_ref, bti_ref,
                   ctr_ref, cti_ref, d_ref, y_ref, t_ref, ws_ref, wc_ref):
    n_chunks, bsz, width = u_ref.shape
    st = SSM_STATE
    dt = jnp.exp(ldt_ref[...])

    ar, ai = arr_ref[...], air_ref[...]
    mag = jnp.exp(ar * dt)
    abr, abi = mag * jnp.cos(ai * dt), mag * jnp.sin(ai * dt)
    den = ar * ar + ai * ai
    pr, pim = abr - 1.0, abi
    cfr, cfi = (pr * ar + pim * ai) / den, (pim * ar - pr * ai) / den
    btr, bti = btr_ref[...], bti_ref[...]
    bbr, bbi = cfr * btr - cfi * bti, cfr * bti + cfi * btr

    arc, aic = arc_ref[...], aic_ref[...]
    lane = lax.broadcasted_iota(jnp.int32, (1, LANES), 1)
    lag = (lane % CHUNK).astype(F32)
    first = lane < CHUNK
    ctr, cti = ctr_ref[...], cti_ref[...]

    def c_times_power(shift):
        e = lag + shift
        m = jnp.exp(arc * dt * e)
        er, ei = m * jnp.cos(aic * dt * e), m * jnp.sin(aic * dt * e)
        xr, xi = [], []
        for q in range(PAIRS):
            cr = jnp.where(first, ctr[:, 2 * q:2 * q + 1], ctr[:, 2 * q + 1:2 * q + 2])
            ci = jnp.where(first, cti[:, 2 * q:2 * q + 1], cti[:, 2 * q + 1:2 * q + 2])
            xr.append(cr * er - ci * ei)
            xi.append(cr * ei + ci * er)
        return jnp.concatenate(xr, axis=1), jnp.concatenate(xi, axis=1)

    xr, xi = c_times_power(0.0)
    hi = lax.Precision.HIGHEST
    kflat = (jnp.dot(bbr, xr, precision=hi, preferred_element_type=F32)
             - jnp.dot(bbi, xi, precision=hi, preferred_element_type=F32))
    prow = lax.broadcasted_iota(jnp.int32, kflat.shape, 0)
    plane = lax.broadcasted_iota(jnp.int32, kflat.shape, 1)
    kflat = kflat + jnp.where(plane == prow * CHUNK, d_ref[...], 0.0)

    srow = lax.broadcasted_iota(jnp.int32, (CHUNK, LANES), 0)
    keep = (lax.broadcasted_iota(jnp.int32, (CHUNK, LANES), 1) % CHUNK) >= srow
    for p in range(SSM_GROUP):
        rows = jnp.broadcast_to(kflat[p:p + 1, :], (CHUNK, width))
        for q in range(PAIRS):
            blk = pltpu.roll(rows[:, q * LANES:(q + 1) * LANES], 0, 1, stride=1, stride_axis=0)
            t_ref[p * CHUNK:(p + 1) * CHUNK, q * LANES:(q + 1) * LANES] = (
                jnp.where(keep, blk, 0.0).astype(BF16))

    e = (CHUNK - 1 - lax.broadcasted_iota(jnp.int32, (CHUNK, 1), 0)).astype(F32)
    m = jnp.exp(ar * dt * e)
    er, ei = m * jnp.cos(ai * dt * e), m * jnp.sin(ai * dt * e)
    for p in range(SSM_GROUP):
        br, bi = bbr[p:p + 1, :], bbi[p:p + 1, :]
        ws_ref[p * CHUNK:(p + 1) * CHUNK, :] = jnp.concatenate(
            [er * br - ei * bi, er * bi + ei * br], axis=1).astype(BF16)

    pr_, pi_ = c_times_power(1.0)
    wc_ref[...] = jnp.concatenate([pr_, -pi_], axis=0).astype(BF16)

    u = u_ref[...].reshape(n_chunks * bsz, width).astype(BF16)
    s = jnp.dot(u, ws_ref[...], preferred_element_type=F32)
    m64 = jnp.exp(ar * dt * CHUNK)
    a64r, a64i = m64 * jnp.cos(ai * dt * CHUNK), m64 * jnp.sin(ai * dt * CHUNK)
    a_same = jnp.concatenate([a64r, a64r], axis=1)
    a_swap = jnp.concatenate([-a64i, a64i], axis=1)
    h = jnp.zeros((bsz, 2 * st), F32)
    prev = []
    for c in range(n_chunks):
        prev.append(h)
        h = a_same * h + a_swap * pltpu.roll(h, st, 1) + s[c * bsz:(c + 1) * bsz, :]
    hprev = jnp.concatenate(prev, axis=0).astype(BF16)

    y = (jnp.dot(u, t_ref[...], preferred_element_type=F32)
         + jnp.dot(hprev, wc_ref[...], preferred_element_type=F32))
    y_ref[...] = y.reshape(n_chunks, bsz, width)


def _s5scan(u4, a_re, a_im, log_dt, b_re, b_im, c_re, c_im, d):
    n_chunks, groups, bsz, width = u4.shape
    st, pg = SSM_STATE, SSM_GROUP
    per_g = lambda *shape: pl.BlockSpec((None,) + shape, lambda g: (g,) + (0,) * len(shape))
    return pl.pallas_call(
        _s5scan_kernel,
        grid=(groups,),
        in_specs=[
            pl.BlockSpec((n_chunks, None, bsz, width), lambda g: (0, g, 0, 0)),
            per_g(1, st), per_g(1, st), per_g(st, 1), per_g(st, 1), per_g(1, 1),
            per_g(pg, st), per_g(pg, st), per_g(st, pg), per_g(st, pg), per_g(pg, 1),
        ],
        out_specs=pl.BlockSpec((n_chunks, None, bsz, width), lambda g: (0, g, 0, 0)),
        out_shape=jax.ShapeDtypeStruct(u4.shape, F32),
        scratch_shapes=[
            pltpu.VMEM((width, width), BF16),
            pltpu.VMEM((width, 2 * st), BF16),
            pltpu.VMEM((2 * st, width), BF16),
        ],
        compiler_params=_params(("arbitrary",), 32),
        name="s5_scan",
    )(u4,
      a_re.reshape(groups, 1, st), a_im.reshape(groups, 1, st),
      a_re.reshape(groups, st, 1), a_im.reshape(groups, st, 1),
      log_dt.reshape(groups, 1, 1),
      jnp.swapaxes(b_re, 1, 2), jnp.swapaxes(b_im, 1, 2),
      jnp.swapaxes(c_re, 1, 2), jnp.swapaxes(c_im, 1, 2),
      d.reshape(groups, pg, 1))


def _s5out_kernel(y_ref, x_ref, mod_ref, wg_ref, wo_ref, g_ref, b_ref, o_ref, a_ref, s_ref):
    bsz = x_ref.shape[0]
    half = SSM_GROUPS * bsz
    for c2 in range(2):
        for q in range(PAIRS):
            s_ref[q, c2 * half:(c2 + 1) * half, :] = (
                y_ref[c2, :, :, q * LANES:(q + 1) * LANES].reshape(half, LANES))
    lane = lax.broadcasted_iota(jnp.int32, (SSM_GROUPS, LANES), 1)
    for q in range(PAIRS):
        for b in range(bsz):
            c0 = s_ref[q, pl.ds(b, SSM_GROUPS, stride=bsz), :]
            c1 = s_ref[q, pl.ds(half + b, SSM_GROUPS, stride=bsz), :]
            x0, x1 = _lane_halves(c0, c1, lane)
            a_ref[(2 * q) * SSM_GROUPS:(2 * q + 1) * SSM_GROUPS, b * LANES:(b + 1) * LANES] = x0
            a_ref[(2 * q + 1) * SSM_GROUPS:(2 * q + 2) * SSM_GROUPS, b * LANES:(b + 1) * LANES] = x1
    z = jax.nn.gelu(a_ref[...], approximate=True)
    gt = jnp.dot(wg_ref[...], z.astype(BF16), preferred_element_type=F32)
    zz = (z * jax.nn.sigmoid(gt)).astype(BF16)
    y = lax.dot_general(zz, wo_ref[...], (((0,), (0,)), ((), ())), preferred_element_type=F32)
    y = y.reshape(bsz, ROWS, D_MODEL)
    gate = 1.0 + mod_ref[5, :, :][:, None, :]
    o_ref[...] = _layer_norm(ALPHA * x_ref[...] + gate * y, g_ref[...], b_ref[...])


def _s5out(y4, x, mods, wg_t, wo, ln_g, ln_b, *, layer):
    bsz, seq, _ = x.shape
    width = SSM_GROUP * CHUNK
    return pl.pallas_call(
        _s5out_kernel,
        grid=(seq // ROWS,),
        in_specs=[
            pl.BlockSpec((2, SSM_GROUPS, bsz, width), lambda j: (j, 0, 0, 0)),
            pl.BlockSpec((bsz, ROWS, D_MODEL), lambda j: (0, j, 0)),
            _resident((None, 9, bsz, D_MODEL), lambda j: (layer, 0, 0, 0)),
            _resident((D_MODEL, D_MODEL), lambda j: (0, 0)),
            _resident((D_MODEL, D_MODEL), lambda j: (0, 0)),
            _resident((None, None, 1, D_MODEL), lambda j: (layer, 1, 0, 0)),
            _resident((None, None, 1, D_MODEL), lambda j: (layer, 1, 0, 0)),
        ],
        out_specs=pl.BlockSpec((bsz, ROWS, D_MODEL), lambda j: (0, j, 0)),
        out_shape=jax.ShapeDtypeStruct(x.shape, F32),
        scratch_shapes=[
            pltpu.VMEM((D_MODEL, bsz * ROWS), F32),
            pltpu.VMEM((PAIRS, 2 * SSM_GROUPS * bsz, LANES), F32),
        ],
        compiler_params=_params(("arbitrary",), 48),
        name="s5_out",
    )(y4, x, mods, wg_t, wo, ln_g, ln_b)


def kernel(x, c, ada_w, ada_b, ln_g, ln_b, ffn_w1, ffn_w3, ffn_w2, attn_w_in, attn_lam, attn_subln_g, attn_w_out, ssm_w_in, ssm_a_re, ssm_a_im, ssm_log_dt, ssm_b_re, ssm_b_im, ssm_c_re, ssm_c_im, ssm_d, ssm_w_gate, ssm_w_out):
    mods = _ada(c, ada_w, ada_b)
    lng = ln_g.reshape(DEPTH, 3, 1, D_MODEL)
    lnb = ln_b.reshape(DEPTH, 3, 1, D_MODEL)
    w1, w3, w2 = ffn_w1.astype(BF16), ffn_w3.astype(BF16), ffn_w2.astype(BF16)
    perm = jnp.arange(D_MODEL).reshape(SSM_GROUPS, SSM_GROUP).T.reshape(-1)

    for layer in range(DEPTH):
        i = layer // 2
        x = _ffn(x, mods, w1, w3, w2, lng, lnb, layer=layer, half=0, sub=0)
        if layer % 2 == 0:
            lam_init = 0.8 - 0.6 * math.exp(-0.3 * layer)
            qkv = _modproj(x, mods, attn_w_in[i].astype(BF16), layer=layer, sub=1)
            o = _attention(qkv, attn_lam[i], attn_subln_g[i], lam_init=lam_init)
            x = _projres(o, x, mods, attn_w_out[i].astype(BF16), lng, lnb, layer=layer, sub=1)
        else:
            w_in_t = ssm_w_in[i][:, perm].T.astype(BF16)
            wg_t = ssm_w_gate[i][perm][:, perm].T.astype(BF16)
            wo = ssm_w_out[i][perm].astype(BF16)
            u4 = _s5in(x, mods, w_in_t, layer=layer)
            y4 = _s5scan(u4, ssm_a_re[i], ssm_a_im[i], ssm_log_dt[i], ssm_b_re[i], ssm_b_im[i],
                         ssm_c_re[i], ssm_c_im[i], ssm_d[i])
            x = _s5out(y4, x, mods, wg_t, wo, lng, lnb, layer=layer)
        x = _ffn(x, mods, w1, w3, w2, lng, lnb, layer=layer, half=1, sub=2)
    return x
```

```python
import functools
import math

import jax
import jax.numpy as jnp
from jax import lax
from jax.experimental import pallas as pl
from jax.experimental.pallas import tpu as pltpu

D_MODEL = 1024
DEPTH = 2
CHUNK = 64
ATTN_HEADS = 8
ATTN_HEAD_DIM = 64
SSM_GROUP = 16
SSM_GROUPS = D_MODEL // SSM_GROUP
SSM_STATE = 64
D_FF = 2816
ALPHA = (2 * DEPTH) ** 0.25
LN_EPS = 1e-5

LANES = 128
F_CHUNK = 256
MIB = 1024 * 1024

F32 = jnp.float32
BF16 = jnp.bfloat16


def _params(sem, vmem_mib):
    return pltpu.CompilerParams(dimension_semantics=sem, vmem_limit_bytes=vmem_mib * MIB)


def _resident(block_shape, index_map):
    return pl.BlockSpec(block_shape, index_map, pipeline_mode=pl.Buffered(1))


def _layer_norm(r, g, b):
    mu = jnp.mean(r, axis=-1, keepdims=True)
    d = r - mu
    var = jnp.mean(d * d, axis=-1, keepdims=True)
    return d * lax.rsqrt(var + LN_EPS) * g + b


def _mods(mod_ref, sub, bidx):
    shift = mod_ref[3 * sub + 0, pl.ds(bidx, 1), :]
    scale = mod_ref[3 * sub + 1, pl.ds(bidx, 1), :]
    gate = 1.0 + mod_ref[3 * sub + 2, pl.ds(bidx, 1), :]
    return shift, scale, gate


def _ada_kernel(c_ref, w_ref, b_ref, o_ref):
    c = c_ref[...]
    cond = (c * jax.nn.sigmoid(c)).astype(BF16)
    o_ref[...] = jnp.dot(cond, w_ref[...].astype(BF16), preferred_element_type=F32) + b_ref[...]


def _ada(c, ada_w, ada_b):
    bsz = c.shape[0]
    n_blk = ada_w.shape[2] // D_MODEL
    return pl.pallas_call(
        _ada_kernel,
        grid=(DEPTH, n_blk),
        in_specs=[
            pl.BlockSpec((bsz, D_MODEL), lambda l, n: (0, 0)),
            pl.BlockSpec((None, D_MODEL, D_MODEL), lambda l, n: (l, 0, n)),
            pl.BlockSpec((None, None, 1, D_MODEL), lambda l, n: (l, n, 0, 0)),
        ],
        out_specs=pl.BlockSpec((None, None, bsz, D_MODEL), lambda l, n: (l, n, 0, 0)),
        out_shape=jax.ShapeDtypeStruct((DEPTH, n_blk, bsz, D_MODEL), F32),
        compiler_params=_params(("arbitrary", "arbitrary"), 24),
        name="ada_mods",
    )(c, ada_w, ada_b.reshape(DEPTH, n_blk, 1, D_MODEL))


def _ffn_kernel(x_ref, mod_ref, w1_ref, w3_ref, w2_ref, g_ref, b_ref, o_ref, *, sub):
    shift, scale, gate = _mods(mod_ref, sub, pl.program_id(0))
    x = x_ref[...]
    h = (x * (1.0 + scale) + shift).astype(BF16)
    acc = jnp.zeros(x.shape, F32)
    for f in range(D_FF // F_CHUNK):
        sl = slice(f * F_CHUNK, (f + 1) * F_CHUNK)
        a = jnp.dot(h, w1_ref[:, sl], preferred_element_type=F32)
        b = jnp.dot(h, w3_ref[:, sl], preferred_element_type=F32)
        u = (a * jax.nn.sigmoid(a) * b).astype(BF16)
        acc = acc + jnp.dot(u, w2_ref[sl, :], preferred_element_type=F32)
    r = ALPHA * x + gate * (0.5 * acc)
    o_ref[...] = _layer_norm(r, g_ref[...], b_ref[...])


def _ffn(x, mods, w1, w3, w2, ln_g, ln_b, *, layer, half, sub, tm=512):
    bsz, seq, _ = x.shape
    wmap = lambda b, i: (layer, half, 0, 0)
    return pl.pallas_call(
        functools.partial(_ffn_kernel, sub=sub),
        grid=(bsz, seq // tm),
        in_specs=[
            pl.BlockSpec((None, tm, D_MODEL), lambda b, i: (b, i, 0)),
            _resident((None, 9, bsz, D_MODEL), lambda b, i: (layer, 0, 0, 0)),
            _resident((None, None, D_MODEL, D_FF), wmap),
            _resident((None, None, D_MODEL, D_FF), wmap),
            _resident((None, None, D_FF, D_MODEL), wmap),
            _resident((None, None, 1, D_MODEL), lambda b, i: (layer, sub, 0, 0)),
            _resident((None, None, 1, D_MODEL), lambda b, i: (layer, sub, 0, 0)),
        ],
        out_specs=pl.BlockSpec((None, tm, D_MODEL), lambda b, i: (b, i, 0)),
        out_shape=jax.ShapeDtypeStruct(x.shape, F32),
        compiler_params=_params(("arbitrary", "arbitrary"), 48),
        name=f"ffn_l{layer}_h{half}",
    )(x, mods, w1, w3, w2, ln_g, ln_b)


def _modproj_kernel(x_ref, mod_ref, w_ref, o_ref, *, sub):
    shift, scale, _ = _mods(mod_ref, sub, pl.program_id(0))
    h = (x_ref[...] * (1.0 + scale) + shift).astype(BF16)
    o_ref[...] = jnp.dot(h, w_ref[...], preferred_element_type=F32).astype(o_ref.dtype)


def _modproj(x, mods, w, *, layer, sub, tm=512):
    bsz, seq, _ = x.shape
    n_out = w.shape[1]
    return pl.pallas_call(
        functools.partial(_modproj_kernel, sub=sub),
        grid=(bsz, seq // tm),
        in_specs=[
            pl.BlockSpec((None, tm, D_MODEL), lambda b, i: (b, i, 0)),
            _resident((None, 9, bsz, D_MODEL), lambda b, i: (layer, 0, 0, 0)),
            _resident((D_MODEL, n_out), lambda b, i: (0, 0)),
        ],
        out_specs=pl.BlockSpec((None, tm, n_out), lambda b, i: (b, i, 0)),
        out_shape=jax.ShapeDtypeStruct((bsz, seq, n_out), BF16),
        compiler_params=_params(("arbitrary", "arbitrary"), 32),
        name=f"modproj_l{layer}",
    )(x, mods, w)


def _projres_kernel(a_ref, x_ref, mod_ref, w_ref, g_ref, b_ref, o_ref, *, sub):
    _, _, gate = _mods(mod_ref, sub, pl.program_id(0))
    y = jnp.dot(a_ref[...], w_ref[...], preferred_element_type=F32)
    o_ref[...] = _layer_norm(ALPHA * x_ref[...] + gate * y, g_ref[...], b_ref[...])


def _projres(a, x, mods, w, ln_g, ln_b, *, layer, sub, tm=512):
    bsz, seq, _ = x.shape
    return pl.pallas_call(
        functools.partial(_projres_kernel, sub=sub),
        grid=(bsz, seq // tm),
        in_specs=[
            pl.BlockSpec((None, tm, D_MODEL), lambda b, i: (b, i, 0)),
            pl.BlockSpec((None, tm, D_MODEL), lambda b, i: (b, i, 0)),
            _resident((None, 9, bsz, D_MODEL), lambda b, i: (layer, 0, 0, 0)),
            _resident((D_MODEL, D_MODEL), lambda b, i: (0, 0)),
            _resident((None, None, 1, D_MODEL), lambda b, i: (layer, sub, 0, 0)),
            _resident((None, None, 1, D_MODEL), lambda b, i: (layer, sub, 0, 0)),
        ],
        out_specs=pl.BlockSpec((None, tm, D_MODEL), lambda b, i: (b, i, 0)),
        out_shape=jax.ShapeDtypeStruct(x.shape, F32),
        compiler_params=_params(("arbitrary", "arbitrary"), 32),
        name=f"projres_l{layer}",
    )(a, x, mods, w, ln_g, ln_b)


def _attn_kernel(q_ref, k_ref, v_ref, lam_ref, sg_ref, o_ref, *, lam_init, tq):
    seq = q_ref.shape[0]
    hd = ATTN_HEAD_DIM
    lam = lam_ref[...]
    lam_full = (jnp.exp(jnp.sum(lam[0:1] * lam[1:2], axis=-1, keepdims=True))
                - jnp.exp(jnp.sum(lam[2:3] * lam[3:4], axis=-1, keepdims=True)) + lam_init)
    lane = lax.broadcasted_iota(jnp.int32, (tq, 2 * hd), 1)
    rq = lax.broadcasted_iota(jnp.int32, (tq, tq), 0) // CHUNK
    ck = lax.broadcasted_iota(jnp.int32, (tq, tq), 1) // CHUNK
    allowed = ck <= rq
    nt = (((1,), (1,)), ((), ()))

    def softmax_v(qm, k, v, kend):
        s = lax.dot_general(qm, k, nt, preferred_element_type=F32)
        diag = jnp.where(allowed, s[:, kend - tq:], -jnp.inf)
        s = diag if kend == tq else jnp.concatenate([s[:, :kend - tq], diag], axis=1)
        p = jnp.exp(s - jnp.max(s, axis=-1, keepdims=True))
        l = jnp.sum(p, axis=-1, keepdims=True)
        return jnp.dot(p.astype(BF16), v, preferred_element_type=F32) / l

    for i in range(seq // tq):
        kend = (i + 1) * tq
        q = q_ref[i * tq:kend, :].astype(F32) * (hd ** -0.5)
        q1 = jnp.where(lane < hd, q, 0.0).astype(BF16)
        q2 = jnp.where(lane >= hd, q, 0.0).astype(BF16)
        k = k_ref[0:kend, :]
        v = v_ref[0:kend, :]
        o = softmax_v(q1, k, v, kend) - lam_full * softmax_v(q2, k, v, kend)
        o = o * lax.rsqrt(jnp.mean(o * o, axis=-1, keepdims=True) + LN_EPS)
        o_ref[i * tq:kend, :] = (o * sg_ref[...] * (1.0 - lam_init)).astype(o_ref.dtype)


def _attention(qkv, lam, subln_g, *, lam_init, tq=256):
    bsz, seq, _ = qkv.shape
    hw = 2 * ATTN_HEAD_DIM
    return pl.pallas_call(
        functools.partial(_attn_kernel, lam_init=lam_init, tq=tq),
        grid=(bsz, ATTN_HEADS),
        in_specs=[
            pl.BlockSpec((None, seq, hw), lambda b, h: (b, 0, h)),
            pl.BlockSpec((None, seq, hw), lambda b, h: (b, 0, ATTN_HEADS + h)),
            pl.BlockSpec((None, seq, hw), lambda b, h: (b, 0, 2 * ATTN_HEADS + h)),
            _resident((4, ATTN_HEAD_DIM), lambda b, h: (0, 0)),
            _resident((1, hw), lambda b, h: (0, 0)),
        ],
        out_specs=pl.BlockSpec((None, seq, hw), lambda b, h: (b, 0, h)),
        out_shape=jax.ShapeDtypeStruct((bsz, seq, D_MODEL), BF16),
        compiler_params=_params(("arbitrary", "arbitrary"), 40),
        name="diff_attention",
    )(qkv, qkv, qkv, lam, subln_g.reshape(1, hw))


ROWS = 2 * CHUNK
PAIRS = SSM_GROUP // 2


def _lane_halves(e, o, lane):
    lo = jnp.where(lane < CHUNK, e, pltpu.roll(o, CHUNK, 1))
    hi = jnp.where(lane < CHUNK, pltpu.roll(e, CHUNK, 1), o)
    return lo, hi


def _s5in_kernel(x_ref, mod_ref, w_ref, o_ref, a_ref, s_ref):
    bsz = x_ref.shape[0]
    shift = mod_ref[3, :, :][:, None, :]
    scale = mod_ref[4, :, :][:, None, :]
    h = (x_ref[...] * (1.0 + scale) + shift).astype(BF16).reshape(bsz * ROWS, D_MODEL)
    a_ref[...] = lax.dot_general(w_ref[...], h, (((1,), (1,)), ((), ())), preferred_element_type=F32)
    lane = lax.broadcasted_iota(jnp.int32, (SSM_GROUPS, LANES), 1)
    half = SSM_GROUPS * bsz
    for q in range(PAIRS):
        for b in range(bsz):
            x0 = a_ref[(2 * q) * SSM_GROUPS:(2 * q + 1) * SSM_GROUPS, b * LANES:(b + 1) * LANES]
            x1 = a_ref[(2 * q + 1) * SSM_GROUPS:(2 * q + 2) * SSM_GROUPS, b * LANES:(b + 1) * LANES]
            c0, c1 = _lane_halves(x0, x1, lane)
            s_ref[q, pl.ds(b, SSM_GROUPS, stride=bsz), :] = c0
            s_ref[q, pl.ds(half + b, SSM_GROUPS, stride=bsz), :] = c1
    for c2 in range(2):
        for q in range(PAIRS):
            o_ref[c2, :, :, q * LANES:(q + 1) * LANES] = (
                s_ref[q, c2 * half:(c2 + 1) * half, :].reshape(SSM_GROUPS, bsz, LANES))


def _s5in(x, mods, w_t, *, layer):
    bsz, seq, _ = x.shape
    width = SSM_GROUP * CHUNK
    return pl.pallas_call(
        _s5in_kernel,
        grid=(seq // ROWS,),
        in_specs=[
            pl.BlockSpec((bsz, ROWS, D_MODEL), lambda j: (0, j, 0)),
            _resident((None, 9, bsz, D_MODEL), lambda j: (layer, 0, 0, 0)),
            _resident((D_MODEL, D_MODEL), lambda j: (0, 0)),
        ],
        out_specs=pl.BlockSpec((2, SSM_GROUPS, bsz, width), lambda j: (j, 0, 0, 0)),
        out_shape=jax.ShapeDtypeStruct((seq // CHUNK, SSM_GROUPS, bsz, width), F32),
        scratch_shapes=[
            pltpu.VMEM((D_MODEL, bsz * ROWS), F32),
            pltpu.VMEM((PAIRS, 2 * SSM_GROUPS * bsz, LANES), F32),
        ],
        compiler_params=_params(("arbitrary",), 40),
        name="s5_in",
    )(x, mods, w_t)


def _s5scan_kernel(u_ref, arr_ref, air_ref, arc_ref, aic_ref, ldt_ref, btr_ref, bti_ref,
                   ctr_ref, cti_ref, d_ref, y_ref, t_ref, ws_ref, wc_ref):
    n_chunks, bsz, width = u_ref.shape
    st = SSM_STATE
    dt = jnp.exp(ldt_ref[...])

    ar, ai = arr_ref[...], air_ref[...]
    mag = jnp.exp(ar * dt)
    abr, abi = mag * jnp.cos(ai * dt), mag * jnp.sin(ai * dt)
    den = ar * ar + ai * ai
    pr, pim = abr - 1.0, abi
    cfr, cfi = (pr * ar + pim * ai) / den, (pim * ar - pr * ai) / den
    btr, bti = btr_ref[...], bti_ref[...]
    bbr, bbi = cfr * btr - cfi * bti, cfr * bti + cfi * btr

    arc, aic = arc_ref[...], aic_ref[...]
    lane = lax.broadcasted_iota(jnp.int32, (1, LANES), 1)
    lag = (lane % CHUNK).astype(F32)
    first = lane < CHUNK
    ctr, cti = ctr_ref[...], cti_ref[...]

    def c_times_power(shift):
        e = lag + shift
        m = jnp.exp(arc * dt * e)
        er, ei = m * jnp.cos(aic * dt * e), m * jnp.sin(aic * dt * e)
        xr, xi = [], []
        for q in range(PAIRS):
            cr = jnp.where(first, ctr[:, 2 * q:2 * q + 1], ctr[:, 2 * q + 1:2 * q + 2])
            ci = jnp.where(first, cti[:, 2 * q:2 * q + 1], cti[:, 2 * q + 1:2 * q + 2])
            xr.append(cr * er - ci * ei)
            xi.append(cr * ei + ci * er)
        return jnp.concatenate(xr, axis=1), jnp.concatenate(xi, axis=1)

    xr, xi = c_times_power(0.0)
    hi = lax.Precision.HIGHEST
    kflat = (jnp.dot(bbr, xr, precision=hi, preferred_element_type=F32)
             - jnp.dot(bbi, xi, precision=hi, preferred_element_type=F32))
    prow = lax.broadcasted_iota(jnp.int32, kflat.shape, 0)
    plane = lax.broadcasted_iota(jnp.int32, kflat.shape, 1)
    kflat = kflat + jnp.where(plane == prow * CHUNK, d_ref[...], 0.0)

    srow = lax.broadcasted_iota(jnp.int32, (CHUNK, LANES), 0)
    keep = (lax.broadcasted_iota(jnp.int32, (CHUNK, LANES), 1) % CHUNK) >= srow
    for p in range(SSM_GROUP):
        rows = jnp.broadcast_to(kflat[p:p + 1, :], (CHUNK, width))
        for q in range(PAIRS):
            blk = pltpu.roll(rows[:, q * LANES:(q + 1) * LANES], 0, 1, stride=1, stride_axis=0)
            t_ref[p * CHUNK:(p + 1) * CHUNK, q * LANES:(q + 1) * LANES] = (
                jnp.where(keep, blk, 0.0).astype(BF16))

    e = (CHUNK - 1 - lax.broadcasted_iota(jnp.int32, (CHUNK, 1), 0)).astype(F32)
    m = jnp.exp(ar * dt * e)
    er, ei = m * jnp.cos(ai * dt * e), m * jnp.sin(ai * dt * e)
    for p in range(SSM_GROUP):
        br, bi = bbr[p:p + 1, :], bbi[p:p + 1, :]
        ws_ref[p * CHUNK:(p + 1) * CHUNK, :] = jnp.concatenate(
            [er * br - ei * bi, er * bi + ei * br], axis=1).astype(BF16)

    pr_, pi_ = c_times_power(1.0)
    wc_ref[...] = jnp.concatenate([pr_, -pi_], axis=0).astype(BF16)

    u = u_ref[...].reshape(n_chunks * bsz, width).astype(BF16)
    s = jnp.dot(u, ws_ref[...], preferred_element_type=F32)
    m64 = jnp.exp(ar * dt * CHUNK)
    a64r, a64i = m64 * jnp.cos(ai * dt * CHUNK), m64 * jnp.sin(ai * dt * CHUNK)
    a_same = jnp.concatenate([a64r, a64r], axis=1)
    a_swap = jnp.concatenate([-a64i, a64i], axis=1)
    h = jnp.zeros((bsz, 2 * st), F32)
    prev = []
    for c in range(n_chunks):
        prev.append(h)
        h = a_same * h + a_swap * pltpu.roll(h, st, 1) + s[c * bsz:(c + 1) * bsz, :]
    hprev = jnp.concatenate(prev, axis=0).astype(BF16)

    y = (jnp.dot(u, t_ref[...], preferred_element_type=F32)
         + jnp.dot(hprev, wc_ref[...], preferred_element_type=F32))
    y_ref[...] = y.reshape(n_chunks, bsz, width)


def _s5scan(u4, a_re, a_im, log_dt, b_re, b_im, c_re, c_im, d):
    n_chunks, groups, bsz, width = u4.shape
    st, pg = SSM_STATE, SSM_GROUP
    per_g = lambda *shape: pl.BlockSpec((None,) + shape, lambda g: (g,) + (0,) * len(shape))
    return pl.pallas_call(
        _s5scan_kernel,
        grid=(groups,),
        in_specs=[
            pl.BlockSpec((n_chunks, None, bsz, width), lambda g: (0, g, 0, 0)),
            per_g(1, st), per_g(1, st), per_g(st, 1), per_g(st, 1), per_g(1, 1),
            per_g(pg, st), per_g(pg, st), per_g(st, pg), per_g(st, pg), per_g(pg, 1),
        ],
        out_specs=pl.BlockSpec((n_chunks, None, bsz, width), lambda g: (0, g, 0, 0)),
        out_shape=jax.ShapeDtypeStruct(u4.shape, F32),
        scratch_shapes=[
            pltpu.VMEM((width, width), BF16),
            pltpu.VMEM((width, 2 * st), BF16),
            pltpu.VMEM((2 * st, width), BF16),
        ],
        compiler_params=_params(("arbitrary",), 32),
        name="s5_scan",
    )(u4,
      a_re.reshape(groups, 1, st), a_im.reshape(groups, 1, st),
      a_re.reshape(groups, st, 1), a_im.reshape(groups, st, 1),
      log_dt.reshape(groups, 1, 1),
      jnp.swapaxes(b_re, 1, 2), jnp.swapaxes(b_im, 1, 2),
      jnp.swapaxes(c_re, 1, 2), jnp.swapaxes(c_im, 1, 2),
      d.reshape(groups, pg, 1))


def _s5out_kernel(y_ref, x_ref, mod_ref, wg_ref, wo_ref, g_ref, b_ref, o_ref, a_ref, s_ref):
    bsz = x_ref.shape[0]
    half = SSM_GROUPS * bsz
    for c2 in range(2):
        for q in range(PAIRS):
            s_ref[q, c2 * half:(c2 + 1) * half, :] = (
                y_ref[c2, :, :, q * LANES:(q + 1) * LANES].reshape(half, LANES))
    lane = lax.broadcasted_iota(jnp.int32, (SSM_GROUPS, LANES), 1)
    for q in range(PAIRS):
        for b in range(bsz):
            c0 = s_ref[q, pl.ds(b, SSM_GROUPS, stride=bsz), :]
            c1 = s_ref[q, pl.ds(half + b, SSM_GROUPS, stride=bsz), :]
            x0, x1 = _lane_halves(c0, c1, lane)
            a_ref[(2 * q) * SSM_GROUPS:(2 * q + 1) * SSM_GROUPS, b * LANES:(b + 1) * LANES] = x0
            a_ref[(2 * q + 1) * SSM_GROUPS:(2 * q + 2) * SSM_GROUPS, b * LANES:(b + 1) * LANES] = x1
    z = jax.nn.gelu(a_ref[...], approximate=True)
    gt = jnp.dot(wg_ref[...], z.astype(BF16), preferred_element_type=F32)
    zz = (z * jax.nn.sigmoid(gt)).astype(BF16)
    y = lax.dot_general(zz, wo_ref[...], (((0,), (0,)), ((), ())), preferred_element_type=F32)
    y = y.reshape(bsz, ROWS, D_MODEL)
    gate = 1.0 + mod_ref[5, :, :][:, None, :]
    o_ref[...] = _layer_norm(ALPHA * x_ref[...] + gate * y, g_ref[...], b_ref[...])


def _s5out(y4, x, mods, wg_t, wo, ln_g, ln_b, *, layer):
    bsz, seq, _ = x.shape
    width = SSM_GROUP * CHUNK
    return pl.pallas_call(
        _s5out_kernel,
        grid=(seq // ROWS,),
        in_specs=[
            pl.BlockSpec((2, SSM_GROUPS, bsz, width), lambda j: (j, 0, 0, 0)),
            pl.BlockSpec((bsz, ROWS, D_MODEL), lambda j: (0, j, 0)),
            _resident((None, 9, bsz, D_MODEL), lambda j: (layer, 0, 0, 0)),
            _resident((D_MODEL, D_MODEL), lambda j: (0, 0)),
            _resident((D_MODEL, D_MODEL), lambda j: (0, 0)),
            _resident((None, None, 1, D_MODEL), lambda j: (layer, 1, 0, 0)),
            _resident((None, None, 1, D_MODEL), lambda j: (layer, 1, 0, 0)),
        ],
        out_specs=pl.BlockSpec((bsz, ROWS, D_MODEL), lambda j: (0, j, 0)),
        out_shape=jax.ShapeDtypeStruct(x.shape, F32),
        scratch_shapes=[
            pltpu.VMEM((D_MODEL, bsz * ROWS), F32),
            pltpu.VMEM((PAIRS, 2 * SSM_GROUPS * bsz, LANES), F32),
        ],
        compiler_params=_params(("arbitrary",), 48),
        name="s5_out",
    )(y4, x, mods, wg_t, wo, ln_g, ln_b)


def kernel(x, c, ada_w, ada_b, ln_g, ln_b, ffn_w1, ffn_w3, ffn_w2, attn_w_in, attn_lam, attn_subln_g, attn_w_out, ssm_w_in, ssm_a_re, ssm_a_im, ssm_log_dt, ssm_b_re, ssm_b_im, ssm_c_re, ssm_c_im, ssm_d, ssm_w_gate, ssm_w_out):
    mods = _ada(c, ada_w, ada_b)
    lng = ln_g.reshape(DEPTH, 3, 1, D_MODEL)
    lnb = ln_b.reshape(DEPTH, 3, 1, D_MODEL)
    w1, w3, w2 = ffn_w1.astype(BF16), ffn_w3.astype(BF16), ffn_w2.astype(BF16)
    perm = jnp.arange(D_MODEL).reshape(SSM_GROUPS, SSM_GROUP).T.reshape(-1)

    for layer in range(DEPTH):
        i = layer // 2
        x = _ffn(x, mods, w1, w3, w2, lng, lnb, layer=layer, half=0, sub=0)
        if layer % 2 == 0:
            lam_init = 0.8 - 0.6 * math.exp(-0.3 * layer)
            qkv = _modproj(x, mods, attn_w_in[i].astype(BF16), layer=layer, sub=1)
            o = _attention(qkv, attn_lam[i], attn_subln_g[i], lam_init=lam_init)
            x = _projres(o, x, mods, attn_w_out[i].astype(BF16), lng, lnb, layer=layer, sub=1)
        else:
            w_in_t = ssm_w_in[i][:, perm].T.astype(BF16)
            wg_t = ssm_w_gate[i][perm][:, perm].T.astype(BF16)
            wo = ssm_w_out[i][perm].astype(BF16)
            u4 = _s5in(x, mods, w_in_t, layer=layer)
            y4 = _s5scan(u4, ssm_a_re[i], ssm_a_im[i], ssm_log_dt[i], ssm_b_re[i], ssm_b_im[i],
                         ssm_c_re[i], ssm_c_im[i], ssm_d[i])
            x = _s5out(y4, x, mods, wg_t, wo, lng, lnb, layer=layer)
        x = _ffn(x, mods, w1, w3, w2, lng, lnb, layer=layer, half=1, sub=2)
    return x
```

```python
import functools
import math

import jax
import jax.numpy as jnp
from jax import lax
from jax.experimental import pallas as pl
from jax.experimental.pallas import tpu as pltpu

D_MODEL = 1024
DEPTH = 2
CHUNK = 64
ATTN_HEADS = 8
ATTN_HEAD_DIM = 64
SSM_GROUP = 16
SSM_GROUPS = D_MODEL // SSM_GROUP
SSM_STATE = 64
D_FF = 2816
ALPHA = (2 * DEPTH) ** 0.25
LN_EPS = 1e-5

LANES = 128
F_CHUNK = 256
MIB = 1024 * 1024

F32 = jnp.float32
BF16 = jnp.bfloat16


def _params(sem, vmem_mib):
    return pltpu.CompilerParams(dimension_semantics=sem, vmem_limit_bytes=vmem_mib * MIB)


def _resident(block_shape, index_map):
    return pl.BlockSpec(block_shape, index_map, pipeline_mode=pl.Buffered(1))


def _layer_norm(r, g, b):
    mu = jnp.mean(r, axis=-1, keepdims=True)
    d = r - mu
    var = jnp.mean(d * d, axis=-1, keepdims=True)
    return d * lax.rsqrt(var + LN_EPS) * g + b


def _mods(mod_ref, sub, bidx):
    shift = mod_ref[3 * sub + 0, pl.ds(bidx, 1), :]
    scale = mod_ref[3 * sub + 1, pl.ds(bidx, 1), :]
    gate = 1.0 + mod_ref[3 * sub + 2, pl.ds(bidx, 1), :]
    return shift, scale, gate


def _ada_kernel(c_ref, w_ref, b_ref, o_ref):
    c = c_ref[...]
    cond = (c * jax.nn.sigmoid(c)).astype(BF16)
    o_ref[...] = jnp.dot(cond, w_ref[...].astype(BF16), preferred_element_type=F32) + b_ref[...]


def _ada(c, ada_w, ada_b):
    bsz = c.shape[0]
    n_blk = ada_w.shape[2] // D_MODEL
    return pl.pallas_call(
        _ada_kernel,
        grid=(DEPTH, n_blk),
        in_specs=[
            pl.BlockSpec((bsz, D_MODEL), lambda l, n: (0, 0)),
            pl.BlockSpec((None, D_MODEL, D_MODEL), lambda l, n: (l, 0, n)),
            pl.BlockSpec((None, None, 1, D_MODEL), lambda l, n: (l, n, 0, 0)),
        ],
        out_specs=pl.BlockSpec((None, None, bsz, D_MODEL), lambda l, n: (l, n, 0, 0)),
        out_shape=jax.ShapeDtypeStruct((DEPTH, n_blk, bsz, D_MODEL), F32),
        compiler_params=_params(("arbitrary", "arbitrary"), 24),
        name="ada_mods",
    )(c, ada_w, ada_b.reshape(DEPTH, n_blk, 1, D_MODEL))


def _ffn_kernel(x_ref, mod_ref, w1_ref, w3_ref, w2_ref, g_ref, b_ref, o_ref, *, sub):
    shift, scale, gate = _mods(mod_ref, sub, pl.program_id(0))
    x = x_ref[...]
    h = (x * (1.0 + scale) + shift).astype(BF16)
    acc = jnp.zeros(x.shape, F32)
    for f in range(D_FF // F_CHUNK):
        sl = slice(f * F_CHUNK, (f + 1) * F_CHUNK)
        a = jnp.dot(h, w1_ref[:, sl], preferred_element_type=F32)
        b = jnp.dot(h, w3_ref[:, sl], preferred_element_type=F32)
        u = (a * jax.nn.sigmoid(a) * b).astype(BF16)
        acc = acc + jnp.dot(u, w2_ref[sl, :], preferred_element_type=F32)
    r = ALPHA * x + gate * (0.5 * acc)
    o_ref[...] = _layer_norm(r, g_ref[...], b_ref[...])


def _ffn(x, mods, w1, w3, w2, ln_g, ln_b, *, layer, half, sub, tm=512):
    bsz, seq, _ = x.shape
    wmap = lambda b, i: (layer, half, 0, 0)
    return pl.pallas_call(
        functools.partial(_ffn_kernel, sub=sub),
        grid=(bsz, seq // tm),
        in_specs=[
            pl.BlockSpec((None, tm, D_MODEL), lambda b, i: (b, i, 0)),
            _resident((None, 9, bsz, D_MODEL), lambda b, i: (layer, 0, 0, 0)),
            _resident((None, None, D_MODEL, D_FF), wmap),
            _resident((None, None, D_MODEL, D_FF), wmap),
            _resident((None, None, D_FF, D_MODEL), wmap),
            _resident((None, None, 1, D_MODEL), lambda b, i: (layer, sub, 0, 0)),
            _resident((None, None, 1, D_MODEL), lambda b, i: (layer, sub, 0, 0)),
        ],
        out_specs=pl.BlockSpec((None, tm, D_MODEL), lambda b, i: (b, i, 0)),
        out_shape=jax.ShapeDtypeStruct(x.shape, F32),
        compiler_params=_params(("arbitrary", "arbitrary"), 48),
        name=f"ffn_l{layer}_h{half}",
    )(x, mods, w1, w3, w2, ln_g, ln_b)


def _modproj_kernel(x_ref, mod_ref, w_ref, o_ref, *, sub):
    shift, scale, _ = _mods(mod_ref, sub, pl.program_id(0))
    h = (x_ref[...] * (1.0 + scale) + shift).astype(BF16)
    o_ref[...] = jnp.dot(h, w_ref[...], preferred_element_type=F32).astype(o_ref.dtype)


def _modproj(x, mods, w, *, layer, sub, tm=512):
    bsz, seq, _ = x.shape
    n_out = w.shape[1]
    return pl.pallas_call(
        functools.partial(_modproj_kernel, sub=sub),
        grid=(bsz, seq // tm),
        in_specs=[
            pl.BlockSpec((None, tm, D_MODEL), lambda b, i: (b, i, 0)),
            _resident((None, 9, bsz, D_MODEL), lambda b, i: (layer, 0, 0, 0)),
            _resident((D_MODEL, n_out), lambda b, i: (0, 0)),
        ],
        out_specs=pl.BlockSpec((None, tm, n_out), lambda b, i: (b, i, 0)),
        out_shape=jax.ShapeDtypeStruct((bsz, seq, n_out), BF16),
        compiler_params=_params(("arbitrary", "arbitrary"), 32),
        name=f"modproj_l{layer}",
    )(x, mods, w)


def _projres_kernel(a_ref, x_ref, mod_ref, w_ref, g_ref, b_ref, o_ref, *, sub):
    _, _, gate = _mods(mod_ref, sub, pl.program_id(0))
    y = jnp.dot(a_ref[...], w_ref[...], preferred_element_type=F32)
    o_ref[...] = _layer_norm(ALPHA * x_ref[...] + gate * y, g_ref[...], b_ref[...])


def _projres(a, x, mods, w, ln_g, ln_b, *, layer, sub, tm=512):
    bsz, seq, _ = x.shape
    return pl.pallas_call(
        functools.partial(_projres_kernel, sub=sub),
        grid=(bsz, seq // tm),
        in_specs=[
            pl.BlockSpec((None, tm, D_MODEL), lambda b, i: (b, i, 0)),
            pl.BlockSpec((None, tm, D_MODEL), lambda b, i: (b, i, 0)),
            _resident((None, 9, bsz, D_MODEL), lambda b, i: (layer, 0, 0, 0)),
            _resident((D_MODEL, D_MODEL), lambda b, i: (0, 0)),
            _resident((None, None, 1, D_MODEL), lambda b, i: (layer, sub, 0, 0)),
            _resident((None, None, 1, D_MODEL), lambda b, i: (layer, sub, 0, 0)),
        ],
        out_specs=pl.BlockSpec((None, tm, D_MODEL), lambda b, i: (b, i, 0)),
        out_shape=jax.ShapeDtypeStruct(x.shape, F32),
        compiler_params=_params(("arbitrary", "arbitrary"), 32),
        name=f"projres_l{layer}",
    )(a, x, mods, w, ln_g, ln_b)


def _attn_kernel(q_ref, k_ref, v_ref, lam_ref, sg_ref, o_ref, *, lam_init, tq):
    seq = q_ref.shape[0]
    hd = ATTN_HEAD_DIM
    lam = lam_ref[...]
    lam_full = (jnp.exp(jnp.sum(lam[0:1] * lam[1:2], axis=-1, keepdims=True))
                - jnp.exp(jnp.sum(lam[2:3] * lam[3:4], axis=-1, keepdims=True)) + lam_init)
    lane = lax.broadcasted_iota(jnp.int32, (tq, 2 * hd), 1)
    rq = lax.broadcasted_iota(jnp.int32, (tq, tq), 0) // CHUNK
    ck = lax.broadcasted_iota(jnp.int32, (tq, tq), 1) // CHUNK
    allowed = ck <= rq
    nt = (((1,), (1,)), ((), ()))

    def softmax_v(qm, k, v, kend):
        s = lax.dot_general(qm, k, nt, preferred_element_type=F32)
        diag = jnp.where(allowed, s[:, kend - tq:], -jnp.inf)
        s = diag if kend == tq else jnp.concatenate([s[:, :kend - tq], diag], axis=1)
        p = jnp.exp(s - jnp.max(s, axis=-1, keepdims=True))
        l = jnp.sum(p, axis=-1, keepdims=True)
        return jnp.dot(p.astype(BF16), v, preferred_element_type=F32) / l

    for i in range(seq // tq):
        kend = (i + 1) * tq
        q = q_ref[i * tq:kend, :].astype(F32) * (hd ** -0.5)
        q1 = jnp.where(lane < hd, q, 0.0).astype(BF16)
        q2 = jnp.where(lane >= hd, q, 0.0).astype(BF16)
        k = k_ref[0:kend, :]
        v = v_ref[0:kend, :]
        o = softmax_v(q1, k, v, kend) - lam_full * softmax_v(q2, k, v, kend)
        o = o * lax.rsqrt(jnp.mean(o * o, axis=-1, keepdims=True) + LN_EPS)
        o_ref[i * tq:kend, :] = (o * sg_ref[...] * (1.0 - lam_init)).astype(o_ref.dtype)


def _attention(qkv, lam, subln_g, *, lam_init, tq=256):
    bsz, seq, _ = qkv.shape
    hw = 2 * ATTN_HEAD_DIM
    return pl.pallas_call(
        functools.partial(_attn_kernel, lam_init=lam_init, tq=tq),
        grid=(bsz, ATTN_HEADS),
        in_specs=[
            pl.BlockSpec((None, seq, hw), lambda b, h: (b, 0, h)),
            pl.BlockSpec((None, seq, hw), lambda b, h: (b, 0, ATTN_HEADS + h)),
            pl.BlockSpec((None, seq, hw), lambda b, h: (b, 0, 2 * ATTN_HEADS + h)),
            _resident((4, ATTN_HEAD_DIM), lambda b, h: (0, 0)),
            _resident((1, hw), lambda b, h: (0, 0)),
        ],
        out_specs=pl.BlockSpec((None, seq, hw), lambda b, h: (b, 0, h)),
        out_shape=jax.ShapeDtypeStruct((bsz, seq, D_MODEL), BF16),
        compiler_params=_params(("arbitrary", "arbitrary"), 40),
        name="diff_attention",
    )(qkv, qkv, qkv, lam, subln_g.reshape(1, hw))


ROWS = 2 * CHUNK
PAIRS = SSM_GROUP // 2


def _lane_halves(e, o, lane):
    lo = jnp.where(lane < CHUNK, e, pltpu.roll(o, CHUNK, 1))
    hi = jnp.where(lane < CHUNK, pltpu.roll(e, CHUNK, 1), o)
    return lo, hi


def _s5in_kernel(x_ref, mod_ref, w_ref, o_ref, a_ref, s_ref):
    bsz = x_ref.shape[0]
    shift = mod_ref[3, :, :][:, None, :]
    scale = mod_ref[4, :, :][:, None, :]
    h = (x_ref[...] * (1.0 + scale) + shift).astype(BF16).reshape(bsz * ROWS, D_MODEL)
    a_ref[...] = lax.dot_general(w_ref[...], h, (((1,), (1,)), ((), ())), preferred_element_type=F32)
    lane = lax.broadcasted_iota(jnp.int32, (SSM_GROUPS, LANES), 1)
    half = SSM_GROUPS * bsz
    for q in range(PAIRS):
        for b in range(bsz):
            x0 = a_ref[(2 * q) * SSM_GROUPS:(2 * q + 1) * SSM_GROUPS, b * LANES:(b + 1) * LANES]
            x1 = a_ref[(2 * q + 1) * SSM_GROUPS:(2 * q + 2) * SSM_GROUPS, b * LANES:(b + 1) * LANES]
            c0, c1 = _lane_halves(x0, x1, lane)
            s_ref[q, pl.ds(b, SSM_GROUPS, stride=bsz), :] = c0
            s_ref[q, pl.ds(half + b, SSM_GROUPS, stride=bsz), :] = c1
    for c2 in range(2):
        for q in range(PAIRS):
            o_ref[c2, :, :, q * LANES:(q + 1) * LANES] = (
                s_ref[q, c2 * half:(c2 + 1) * half, :].reshape(SSM_GROUPS, bsz, LANES))


def _s5in(x, mods, w_t, *, layer):
    bsz, seq, _ = x.shape
    width = SSM_GROUP * CHUNK
    return pl.pallas_call(
        _s5in_kernel,
        grid=(seq // ROWS,),
        in_specs=[
            pl.BlockSpec((bsz, ROWS, D_MODEL), lambda j: (0, j, 0)),
            _resident((None, 9, bsz, D_MODEL), lambda j: (layer, 0, 0, 0)),
            _resident((D_MODEL, D_MODEL), lambda j: (0, 0)),
        ],
        out_specs=pl.BlockSpec((2, SSM_GROUPS, bsz, width), lambda j: (j, 0, 0, 0)),
        out_shape=jax.ShapeDtypeStruct((seq // CHUNK, SSM_GROUPS, bsz, width), F32),
        scratch_shapes=[
            pltpu.VMEM((D_MODEL, bsz * ROWS), F32),
            pltpu.VMEM((PAIRS, 2 * SSM_GROUPS * bsz, LANES), F32),
        ],
        compiler_params=_params(("arbitrary",), 40),
        name="s5_in",
    )(x, mods, w_t)


def _s5scan_kernel(u_ref, arr_ref, air_ref, arc_ref, aic_ref, ldt_ref, btr_ref, bti_ref,
                   ctr_ref, cti_ref, d_ref, y_ref, t_ref, ws_ref, wc_ref):
    n_chunks, bsz, width = u_ref.shape
    st = SSM_STATE
    dt = jnp.exp(ldt_ref[...])

    ar, ai = arr_ref[...], air_ref[...]
    mag = jnp.exp(ar * dt)
    abr, abi = mag * jnp.cos(ai * dt), mag * jnp.sin(ai * dt)
    den = ar * ar + ai * ai
    pr, pim = abr - 1.0, abi
    cfr, cfi = (pr * ar + pim * ai) / den, (pim * ar - pr * ai) / den
    btr, bti = btr_ref[...], bti_ref[...]
    bbr, bbi = cfr * btr - cfi * bti, cfr * bti + cfi * btr

    arc, aic = arc_ref[...], aic_ref[...]
    lane = lax.broadcasted_iota(jnp.int32, (1, LANES), 1)
    lag = (lane % CHUNK).astype(F32)
    first = lane < CHUNK
    ctr, cti = ctr_ref[...], cti_ref[...]

    def c_times_power(shift):
        e = lag + shift
        m = jnp.exp(arc * dt * e)
        er, ei = m * jnp.cos(aic * dt * e), m * jnp.sin(aic * dt * e)
        xr, xi = [], []
        for q in range(PAIRS):
            cr = jnp.where(first, ctr[:, 2 * q:2 * q + 1], ctr[:, 2 * q + 1:2 * q + 2])
            ci = jnp.where(first, cti[:, 2 * q:2 * q + 1], cti[:, 2 * q + 1:2 * q + 2])
            xr.append(cr * er - ci * ei)
            xi.append(cr * ei + ci * er)
        return jnp.concatenate(xr, axis=1), jnp.concatenate(xi, axis=1)

    xr, xi = c_times_power(0.0)
    hi = lax.Precision.HIGHEST
    kflat = (jnp.dot(bbr, xr, precision=hi, preferred_element_type=F32)
             - jnp.dot(bbi, xi, precision=hi, preferred_element_type=F32))
    prow = lax.broadcasted_iota(jnp.int32, kflat.shape, 0)
    plane = lax.broadcasted_iota(jnp.int32, kflat.shape, 1)
    kflat = kflat + jnp.where(plane == prow * CHUNK, d_ref[...], 0.0)

    srow = lax.broadcasted_iota(jnp.int32, (CHUNK, LANES), 0)
    keep = (lax.broadcasted_iota(jnp.int32, (CHUNK, LANES), 1) % CHUNK) >= srow
    for p in range(SSM_GROUP):
        rows = jnp.broadcast_to(kflat[p:p + 1, :], (CHUNK, width))
        for q in range(PAIRS):
            blk = pltpu.roll(rows[:, q * LANES:(q + 1) * LANES], 0, 1, stride=1, stride_axis=0)
            t_ref[p * CHUNK:(p + 1) * CHUNK, q * LANES:(q + 1) * LANES] = (
                jnp.where(keep, blk, 0.0).astype(BF16))

    e = (CHUNK - 1 - lax.broadcasted_iota(jnp.int32, (CHUNK, 1), 0)).astype(F32)
    m = jnp.exp(ar * dt * e)
    er, ei = m * jnp.cos(ai * dt * e), m * jnp.sin(ai * dt * e)
    zpad = jnp.zeros((CHUNK, LANES - st), F32)
    for p in range(SSM_GROUP):
        br, bi = bbr[p:p + 1, :], bbi[p:p + 1, :]
        ws_ref[p * CHUNK:(p + 1) * CHUNK, :] = jnp.concatenate(
            [er * br - ei * bi, zpad, er * bi + ei * br, zpad], axis=1).astype(BF16)

    pr_, pi_ = c_times_power(1.0)
    zrows = jnp.zeros((LANES - st, width), F32)
    wc_ref[...] = jnp.concatenate([pr_, zrows, -pi_, zrows], axis=0).astype(BF16)

    u = u_ref[...].reshape(n_chunks * bsz, width).astype(BF16)
    s = jnp.dot(u, ws_ref[...], preferred_element_type=F32)
    m64 = jnp.exp(ar * dt * CHUNK)
    zlane = jnp.zeros((1, LANES - st), F32)
    a64r = jnp.concatenate([m64 * jnp.cos(ai * dt * CHUNK), zlane], axis=1)
    a64i = jnp.concatenate([m64 * jnp.sin(ai * dt * CHUNK), zlane], axis=1)
    hr = jnp.zeros((bsz, LANES), F32)
    hi_ = jnp.zeros((bsz, LANES), F32)
    prev = []
    for c in range(n_chunks):
        prev.append(jnp.concatenate([hr, hi_], axis=1))
        sr, si = s[c * bsz:(c + 1) * bsz, :LANES], s[c * bsz:(c + 1) * bsz, LANES:]
        hr, hi_ = a64r * hr - a64i * hi_ + sr, a64r * hi_ + a64i * hr + si
    hprev = jnp.concatenate(prev, axis=0).astype(BF16)

    y = (jnp.dot(u, t_ref[...], preferred_element_type=F32)
         + jnp.dot(hprev, wc_ref[...], preferred_element_type=F32))
    y_ref[...] = y.reshape(n_chunks, bsz, width)


def _s5scan(u4, a_re, a_im, log_dt, b_re, b_im, c_re, c_im, d):
    n_chunks, groups, bsz, width = u4.shape
    st, pg = SSM_STATE, SSM_GROUP
    per_g = lambda *shape: pl.BlockSpec((None,) + shape, lambda g: (g,) + (0,) * len(shape))
    return pl.pallas_call(
        _s5scan_kernel,
        grid=(groups,),
        in_specs=[
            pl.BlockSpec((n_chunks, None, bsz, width), lambda g: (0, g, 0, 0)),
            per_g(1, st), per_g(1, st), per_g(st, 1), per_g(st, 1), per_g(1, 1),
            per_g(pg, st), per_g(pg, st), per_g(st, pg), per_g(st, pg), per_g(pg, 1),
        ],
        out_specs=pl.BlockSpec((n_chunks, None, bsz, width), lambda g: (0, g, 0, 0)),
        out_shape=jax.ShapeDtypeStruct(u4.shape, F32),
        scratch_shapes=[
            pltpu.VMEM((width, width), BF16),
            pltpu.VMEM((width, 2 * LANES), BF16),
            pltpu.VMEM((2 * LANES, width), BF16),
        ],
        compiler_params=_params(("arbitrary",), 32),
        name="s5_scan",
    )(u4,
      a_re.reshape(groups, 1, st), a_im.reshape(groups, 1, st),
      a_re.reshape(groups, st, 1), a_im.reshape(groups, st, 1),
      log_dt.reshape(groups, 1, 1),
      jnp.swapaxes(b_re, 1, 2), jnp.swapaxes(b_im, 1, 2),
      jnp.swapaxes(c_re, 1, 2), jnp.swapaxes(c_im, 1, 2),
      d.reshape(groups, pg, 1))


def _s5out_kernel(y_ref, x_ref, mod_ref, wg_ref, wo_ref, g_ref, b_ref, o_ref, a_ref, s_ref):
    bsz = x_ref.shape[0]
    half = SSM_GROUPS * bsz
    for c2 in range(2):
        for q in range(PAIRS):
            s_ref[q, c2 * half:(c2 + 1) * half, :] = (
                y_ref[c2, :, :, q * LANES:(q + 1) * LANES].reshape(half, LANES))
    lane = lax.broadcasted_iota(jnp.int32, (SSM_GROUPS, LANES), 1)
    for q in range(PAIRS):
        for b in range(bsz):
            c0 = s_ref[q, pl.ds(b, SSM_GROUPS, stride=bsz), :]
            c1 = s_ref[q, pl.ds(half + b, SSM_GROUPS, stride=bsz), :]
            x0, x1 = _lane_halves(c0, c1, lane)
            a_ref[(2 * q) * SSM_GROUPS:(2 * q + 1) * SSM_GROUPS, b * LANES:(b + 1) * LANES] = x0
            a_ref[(2 * q + 1) * SSM_GROUPS:(2 * q + 2) * SSM_GROUPS, b * LANES:(b + 1) * LANES] = x1
    z = jax.nn.gelu(a_ref[...], approximate=True)
    gt = jnp.dot(wg_ref[...], z.astype(BF16), preferred_element_type=F32)
    zz = (z * jax.nn.sigmoid(gt)).astype(BF16)
    y = lax.dot_general(zz, wo_ref[...], (((0,), (0,)), ((), ())), preferred_element_type=F32)
    y = y.reshape(bsz, ROWS, D_MODEL)
    gate = 1.0 + mod_ref[5, :, :][:, None, :]
    o_ref[...] = _layer_norm(ALPHA * x_ref[...] + gate * y, g_ref[...], b_ref[...])


def _s5out(y4, x, mods, wg_t, wo, ln_g, ln_b, *, layer):
    bsz, seq, _ = x.shape
    width = SSM_GROUP * CHUNK
    return pl.pallas_call(
        _s5out_kernel,
        grid=(seq // ROWS,),
        in_specs=[
            pl.BlockSpec((2, SSM_GROUPS, bsz, width), lambda j: (j, 0, 0, 0)),
            pl.BlockSpec((bsz, ROWS, D_MODEL), lambda j: (0, j, 0)),
            _resident((None, 9, bsz, D_MODEL), lambda j: (layer, 0, 0, 0)),
            _resident((D_MODEL, D_MODEL), lambda j: (0, 0)),
            _resident((D_MODEL, D_MODEL), lambda j: (0, 0)),
            _resident((None, None, 1, D_MODEL), lambda j: (layer, 1, 0, 0)),
            _resident((None, None, 1, D_MODEL), lambda j: (layer, 1, 0, 0)),
        ],
        out_specs=pl.BlockSpec((bsz, ROWS, D_MODEL), lambda j: (0, j, 0)),
        out_shape=jax.ShapeDtypeStruct(x.shape, F32),
        scratch_shapes=[
            pltpu.VMEM((D_MODEL, bsz * ROWS), F32),
            pltpu.VMEM((PAIRS, 2 * SSM_GROUPS * bsz, LANES), F32),
        ],
        compiler_params=_params(("arbitrary",), 48),
        name="s5_out",
    )(y4, x, mods, wg_t, wo, ln_g, ln_b)


def kernel(x, c, ada_w, ada_b, ln_g, ln_b, ffn_w1, ffn_w3, ffn_w2, attn_w_in, attn_lam, attn_subln_g, attn_w_out, ssm_w_in, ssm_a_re, ssm_a_im, ssm_log_dt, ssm_b_re, ssm_b_im, ssm_c_re, ssm_c_im, ssm_d, ssm_w_gate, ssm_w_out):
    mods = _ada(c, ada_w, ada_b)
    lng = ln_g.reshape(DEPTH, 3, 1, D_MODEL)
    lnb = ln_b.reshape(DEPTH, 3, 1, D_MODEL)
    w1, w3, w2 = ffn_w1.astype(BF16), ffn_w3.astype(BF16), ffn_w2.astype(BF16)
    gp = (SSM_GROUPS, SSM_GROUP)

    for layer in range(DEPTH):
        i = layer // 2
        x = _ffn(x, mods, w1, w3, w2, lng, lnb, layer=layer, half=0, sub=0)
        if layer % 2 == 0:
            lam_init = 0.8 - 0.6 * math.exp(-0.3 * layer)
            qkv = _modproj(x, mods, attn_w_in[i].astype(BF16), layer=layer, sub=1)
            o = _attention(qkv, attn_lam[i], attn_subln_g[i], lam_init=lam_init)
            x = _projres(o, x, mods, attn_w_out[i].astype(BF16), lng, lnb, layer=layer, sub=1)
        else:
            w_in_t = (ssm_w_in[i].astype(BF16).reshape(D_MODEL, *gp)
                      .transpose(2, 1, 0).reshape(D_MODEL, D_MODEL))
            wg_t = (ssm_w_gate[i].astype(BF16).reshape(*gp, *gp)
                    .transpose(3, 2, 1, 0).reshape(D_MODEL, D_MODEL))
            wo = (ssm_w_out[i].astype(BF16).reshape(*gp, D_MODEL)
                  .transpose(1, 0, 2).reshape(D_MODEL, D_MODEL))
            u4 = _s5in(x, mods, w_in_t, layer=layer)
            y4 = _s5scan(u4, ssm_a_re[i], ssm_a_im[i], ssm_log_dt[i], ssm_b_re[i], ssm_b_im[i],
                         ssm_c_re[i], ssm_c_im[i], ssm_d[i])
            x = _s5out(y4, x, mods, wg_t, wo, lng, lnb, layer=layer)
        x = _ffn(x, mods, w1, w3, w2, lng, lnb, layer=layer, half=1, sub=2)
    return x
```

```python
import functools
import math

import jax
import jax.numpy as jnp
from jax import lax
from jax.experimental import pallas as pl
from jax.experimental.pallas import tpu as pltpu

D_MODEL = 1024
DEPTH = 2
CHUNK = 64
ATTN_HEADS = 8
ATTN_HEAD_DIM = 64
SSM_GROUP = 16
SSM_GROUPS = D_MODEL // SSM_GROUP
SSM_STATE = 64
D_FF = 2816
ALPHA = (2 * DEPTH) ** 0.25
LN_EPS = 1e-5

LANES = 128
F_CHUNK = 256
MIB = 1024 * 1024

F32 = jnp.float32
BF16 = jnp.bfloat16


def _params(sem, vmem_mib):
    return pltpu.CompilerParams(dimension_semantics=sem, vmem_limit_bytes=vmem_mib * MIB)


def _resident(block_shape, index_map):
    return pl.BlockSpec(block_shape, index_map, pipeline_mode=pl.Buffered(1))


def _layer_norm(r, g, b):
    mu = jnp.mean(r, axis=-1, keepdims=True)
    d = r - mu
    var = jnp.mean(d * d, axis=-1, keepdims=True)
    return d * lax.rsqrt(var + LN_EPS) * g + b


def _mods(mod_ref, sub, bidx):
    shift = mod_ref[3 * sub + 0, pl.ds(bidx, 1), :]
    scale = mod_ref[3 * sub + 1, pl.ds(bidx, 1), :]
    gate = 1.0 + mod_ref[3 * sub + 2, pl.ds(bidx, 1), :]
    return shift, scale, gate


def _ada_kernel(c_ref, w_ref, b_ref, o_ref):
    c = c_ref[...]
    cond = (c * jax.nn.sigmoid(c)).astype(BF16)
    o_ref[...] = jnp.dot(cond, w_ref[...].astype(BF16), preferred_element_type=F32) + b_ref[...]


def _ada(c, ada_w, ada_b):
    bsz = c.shape[0]
    n_blk = ada_w.shape[2] // D_MODEL
    return pl.pallas_call(
        _ada_kernel,
        grid=(DEPTH, n_blk),
        in_specs=[
            pl.BlockSpec((bsz, D_MODEL), lambda l, n: (0, 0)),
            pl.BlockSpec((None, D_MODEL, D_MODEL), lambda l, n: (l, 0, n)),
            pl.BlockSpec((None, None, 1, D_MODEL), lambda l, n: (l, n, 0, 0)),
        ],
        out_specs=pl.BlockSpec((None, None, bsz, D_MODEL), lambda l, n: (l, n, 0, 0)),
        out_shape=jax.ShapeDtypeStruct((DEPTH, n_blk, bsz, D_MODEL), F32),
        compiler_params=_params(("arbitrary", "arbitrary"), 24),
        name="ada_mods",
    )(c, ada_w, ada_b.reshape(DEPTH, n_blk, 1, D_MODEL))


def _ffn_kernel(x_ref, mod_ref, w1_ref, w3_ref, w2_ref, g_ref, b_ref, o_ref, *, sub):
    shift, scale, gate = _mods(mod_ref, sub, pl.program_id(0))
    x = x_ref[...]
    h = (x * (1.0 + scale) + shift).astype(BF16)
    acc = jnp.zeros(x.shape, F32)
    for f in range(D_FF // F_CHUNK):
        sl = slice(f * F_CHUNK, (f + 1) * F_CHUNK)
        a = jnp.dot(h, w1_ref[:, sl], preferred_element_type=F32)
        b = jnp.dot(h, w3_ref[:, sl], preferred_element_type=F32)
        u = (a * jax.nn.sigmoid(a) * b).astype(BF16)
        acc = acc + jnp.dot(u, w2_ref[sl, :], preferred_element_type=F32)
    r = ALPHA * x + gate * (0.5 * acc)
    o_ref[...] = _layer_norm(r, g_ref[...], b_ref[...])


def _ffn(x, mods, w1, w3, w2, ln_g, ln_b, *, layer, half, sub, tm=512):
    bsz, seq, _ = x.shape
    wmap = lambda b, i: (layer, half, 0, 0)
    return pl.pallas_call(
        functools.partial(_ffn_kernel, sub=sub),
        grid=(bsz, seq // tm),
        in_specs=[
            pl.BlockSpec((None, tm, D_MODEL), lambda b, i: (b, i, 0)),
            _resident((None, 9, bsz, D_MODEL), lambda b, i: (layer, 0, 0, 0)),
            _resident((None, None, D_MODEL, D_FF), wmap),
            _resident((None, None, D_MODEL, D_FF), wmap),
            _resident((None, None, D_FF, D_MODEL), wmap),
            _resident((None, None, 1, D_MODEL), lambda b, i: (layer, sub, 0, 0)),
            _resident((None, None, 1, D_MODEL), lambda b, i: (layer, sub, 0, 0)),
        ],
        out_specs=pl.BlockSpec((None, tm, D_MODEL), lambda b, i: (b, i, 0)),
        out_shape=jax.ShapeDtypeStruct(x.shape, F32),
        compiler_params=_params(("arbitrary", "arbitrary"), 48),
        name=f"ffn_l{layer}_h{half}",
    )(x, mods, w1, w3, w2, ln_g, ln_b)


def _modproj_kernel(x_ref, mod_ref, w_ref, o_ref, *, sub):
    shift, scale, _ = _mods(mod_ref, sub, pl.program_id(0))
    h = (x_ref[...] * (1.0 + scale) + shift).astype(BF16)
    o_ref[...] = jnp.dot(h, w_ref[...], preferred_element_type=F32).astype(o_ref.dtype)


def _modproj(x, mods, w, *, layer, sub, tm=512):
    bsz, seq, _ = x.shape
    n_out = w.shape[1]
    return pl.pallas_call(
        functools.partial(_modproj_kernel, sub=sub),
        grid=(bsz, seq // tm),
        in_specs=[
            pl.BlockSpec((None, tm, D_MODEL), lambda b, i: (b, i, 0)),
            _resident((None, 9, bsz, D_MODEL), lambda b, i: (layer, 0, 0, 0)),
            _resident((D_MODEL, n_out), lambda b, i: (0, 0)),
        ],
        out_specs=pl.BlockSpec((None, tm, n_out), lambda b, i: (b, i, 0)),
        out_shape=jax.ShapeDtypeStruct((bsz, seq, n_out), BF16),
        compiler_params=_params(("arbitrary", "arbitrary"), 32),
        name=f"modproj_l{layer}",
    )(x, mods, w)


def _projres_kernel(a_ref, x_ref, mod_ref, w_ref, g_ref, b_ref, o_ref, *, sub):
    _, _, gate = _mods(mod_ref, sub, pl.program_id(0))
    y = jnp.dot(a_ref[...], w_ref[...], preferred_element_type=F32)
    o_ref[...] = _layer_norm(ALPHA * x_ref[...] + gate * y, g_ref[...], b_ref[...])


def _projres(a, x, mods, w, ln_g, ln_b, *, layer, sub, tm=512):
    bsz, seq, _ = x.shape
    return pl.pallas_call(
        functools.partial(_projres_kernel, sub=sub),
        grid=(bsz, seq // tm),
        in_specs=[
            pl.BlockSpec((None, tm, D_MODEL), lambda b, i: (b, i, 0)),
            pl.BlockSpec((None, tm, D_MODEL), lambda b, i: (b, i, 0)),
            _resident((None, 9, bsz, D_MODEL), lambda b, i: (layer, 0, 0, 0)),
            _resident((D_MODEL, D_MODEL), lambda b, i: (0, 0)),
            _resident((None, None, 1, D_MODEL), lambda b, i: (layer, sub, 0, 0)),
            _resident((None, None, 1, D_MODEL), lambda b, i: (layer, sub, 0, 0)),
        ],
        out_specs=pl.BlockSpec((None, tm, D_MODEL), lambda b, i: (b, i, 0)),
        out_shape=jax.ShapeDtypeStruct(x.shape, F32),
        compiler_params=_params(("arbitrary", "arbitrary"), 32),
        name=f"projres_l{layer}",
    )(a, x, mods, w, ln_g, ln_b)


def _attn_kernel(q_ref, k_ref, v_ref, lam_ref, sg_ref, o_ref, vx_ref, *, lam_init, tq):
    seq = q_ref.shape[0]
    hd = ATTN_HEAD_DIM
    lam = lam_ref[...]
    lam_full = (jnp.exp(jnp.sum(lam[0:1] * lam[1:2], axis=-1, keepdims=True))
                - jnp.exp(jnp.sum(lam[2:3] * lam[3:4], axis=-1, keepdims=True)) + lam_init)
    lane = lax.broadcasted_iota(jnp.int32, (tq, 2 * hd), 1)
    rq = lax.broadcasted_iota(jnp.int32, (tq, tq), 0) // CHUNK
    ck = lax.broadcasted_iota(jnp.int32, (tq, tq), 1) // CHUNK
    allowed = ck <= rq
    nt = (((1,), (1,)), ((), ()))

    hw = 2 * hd
    vx_ref[:, :hw] = v_ref[...]
    vx_ref[:, hw:] = jnp.ones((seq, hw), BF16)

    def score(i, m):
        kend = (i + 1) * tq
        q = q_ref[i * tq:kend, :].astype(F32) * (hd ** -0.5)
        qm = jnp.where((lane < hd) if m == 0 else (lane >= hd), q, 0.0).astype(BF16)
        s = lax.dot_general(qm, k_ref[0:kend, :], nt, preferred_element_type=F32)
        diag = jnp.where(allowed, s[:, kend - tq:], -jnp.inf)
        return diag if kend == tq else jnp.concatenate([s[:, :kend - tq], diag], axis=1)

    def attend(i, s):
        kend = (i + 1) * tq
        p = jnp.exp(s - jnp.max(s, axis=-1, keepdims=True))
        r = jnp.dot(p.astype(BF16), vx_ref[0:kend, :], preferred_element_type=F32)
        return r[:, :hw] / r[:, hw:]

    def finish(i, o1, o2):
        o = o1 - lam_full * o2
        o = o * lax.rsqrt(jnp.mean(o * o, axis=-1, keepdims=True) + LN_EPS)
        o_ref[i * tq:(i + 1) * tq, :] = (o * sg_ref[...] * (1.0 - lam_init)).astype(o_ref.dtype)

    n = seq // tq
    units = [(i, m) for i in reversed(range(n)) for m in range(2)]
    pend = score(*units[0])
    outs = {}
    for u, (i, m) in enumerate(units):
        cur = pend
        if u + 1 < len(units):
            pend = score(*units[u + 1])
        outs[(i, m)] = attend(i, cur)
        if m == 1:
            finish(i, outs.pop((i, 0)), outs.pop((i, 1)))


def _attention(qkv, lam, subln_g, *, lam_init, tq=256):
    bsz, seq, _ = qkv.shape
    hw = 2 * ATTN_HEAD_DIM
    return pl.pallas_call(
        functools.partial(_attn_kernel, lam_init=lam_init, tq=tq),
        grid=(bsz, ATTN_HEADS),
        in_specs=[
            pl.BlockSpec((None, seq, hw), lambda b, h: (b, 0, h)),
            pl.BlockSpec((None, seq, hw), lambda b, h: (b, 0, ATTN_HEADS + h)),
            pl.BlockSpec((None, seq, hw), lambda b, h: (b, 0, 2 * ATTN_HEADS + h)),
            _resident((4, ATTN_HEAD_DIM), lambda b, h: (0, 0)),
            _resident((1, hw), lambda b, h: (0, 0)),
        ],
        out_specs=pl.BlockSpec((None, seq, hw), lambda b, h: (b, 0, h)),
        out_shape=jax.ShapeDtypeStruct((bsz, seq, D_MODEL), BF16),
        scratch_shapes=[pltpu.VMEM((seq, 2 * hw), BF16)],
        compiler_params=_params(("arbitrary", "arbitrary"), 40),
        name="diff_attention",
    )(qkv, qkv, qkv, lam, subln_g.reshape(1, hw))


ROWS = 2 * CHUNK
PAIRS = SSM_GROUP // 2


def _lane_halves(e, o, lane):
    lo = jnp.where(lane < CHUNK, e, pltpu.roll(o, CHUNK, 1))
    hi = jnp.where(lane < CHUNK, pltpu.roll(e, CHUNK, 1), o)
    return lo, hi


def _s5in_kernel(x_ref, mod_ref, w_ref, o_ref, a_ref, s_ref):
    bsz = x_ref.shape[0]
    shift = mod_ref[3, :, :][:, None, :]
    scale = mod_ref[4, :, :][:, None, :]
    h = (x_ref[...] * (1.0 + scale) + shift).astype(BF16).reshape(bsz * ROWS, D_MODEL)
    a_ref[...] = lax.dot_general(w_ref[...], h, (((1,), (1,)), ((), ())), preferred_element_type=F32)
    lane = lax.broadcasted_iota(jnp.int32, (SSM_GROUPS, LANES), 1)
    half = SSM_GROUPS * bsz
    for q in range(PAIRS):
        for b in range(bsz):
            x0 = a_ref[(2 * q) * SSM_GROUPS:(2 * q + 1) * SSM_GROUPS, b * LANES:(b + 1) * LANES]
            x1 = a_ref[(2 * q + 1) * SSM_GROUPS:(2 * q + 2) * SSM_GROUPS, b * LANES:(b + 1) * LANES]
            c0, c1 = _lane_halves(x0, x1, lane)
            s_ref[q, pl.ds(b, SSM_GROUPS, stride=bsz), :] = c0
            s_ref[q, pl.ds(half + b, SSM_GROUPS, stride=bsz), :] = c1
    for c2 in range(2):
        for q in range(PAIRS):
            o_ref[c2, :, :, q * LANES:(q + 1) * LANES] = (
                s_ref[q, c2 * half:(c2 + 1) * half, :].reshape(SSM_GROUPS, bsz, LANES))


def _s5in(x, mods, w_t, *, layer):
    bsz, seq, _ = x.shape
    width = SSM_GROUP * CHUNK
    return pl.pallas_call(
        _s5in_kernel,
        grid=(seq // ROWS,),
        in_specs=[
            pl.BlockSpec((bsz, ROWS, D_MODEL), lambda j: (0, j, 0)),
            _resident((None, 9, bsz, D_MODEL), lambda j: (layer, 0, 0, 0)),
            _resident((D_MODEL, D_MODEL), lambda j: (0, 0)),
        ],
        out_specs=pl.BlockSpec((2, SSM_GROUPS, bsz, width), lambda j: (j, 0, 0, 0)),
        out_shape=jax.ShapeDtypeStruct((seq // CHUNK, SSM_GROUPS, bsz, width), F32),
        scratch_shapes=[
            pltpu.VMEM((D_MODEL, bsz * ROWS), F32),
            pltpu.VMEM((PAIRS, 2 * SSM_GROUPS * bsz, LANES), F32),
        ],
        compiler_params=_params(("arbitrary",), 40),
        name="s5_in",
    )(x, mods, w_t)


def _s5scan_kernel(u_ref, arr_ref, air_ref, arc_ref, aic_ref, ldt_ref, btr_ref, bti_ref,
                   ctr_ref, cti_ref, d_ref, y_ref, t_ref, ws_ref, wc_ref):
    n_chunks, bsz, width = u_ref.shape
    st = SSM_STATE
    dt = jnp.exp(ldt_ref[...])

    ar, ai = arr_ref[...], air_ref[...]
    mag = jnp.exp(ar * dt)
    abr, abi = mag * jnp.cos(ai * dt), mag * jnp.sin(ai * dt)
    den = ar * ar + ai * ai
    pr, pim = abr - 1.0, abi
    cfr, cfi = (pr * ar + pim * ai) / den, (pim * ar - pr * ai) / den
    btr, bti = btr_ref[...], bti_ref[...]
    bbr, bbi = cfr * btr - cfi * bti, cfr * bti + cfi * btr

    arc, aic = arc_ref[...], aic_ref[...]
    lane = lax.broadcasted_iota(jnp.int32, (1, LANES), 1)
    lag = (lane % CHUNK).astype(F32)
    first = lane < CHUNK
    ctr, cti = ctr_ref[...], cti_ref[...]

    def c_times_power(shift):
        e = lag + shift
        m = jnp.exp(arc * dt * e)
        er, ei = m * jnp.cos(aic * dt * e), m * jnp.sin(aic * dt * e)
        xr, xi = [], []
        for q in range(PAIRS):
            cr = jnp.where(first, ctr[:, 2 * q:2 * q + 1], ctr[:, 2 * q + 1:2 * q + 2])
            ci = jnp.where(first, cti[:, 2 * q:2 * q + 1], cti[:, 2 * q + 1:2 * q + 2])
            xr.append(cr * er - ci * ei)
            xi.append(cr * ei + ci * er)
        return jnp.concatenate(xr, axis=1), jnp.concatenate(xi, axis=1)

    xr, xi = c_times_power(0.0)
    hi = lax.Precision.HIGHEST
    kflat = (jnp.dot(bbr, xr, precision=hi, preferred_element_type=F32)
             - jnp.dot(bbi, xi, precision=hi, preferred_element_type=F32))
    prow = lax.broadcasted_iota(jnp.int32, kflat.shape, 0)
    plane = lax.broadcasted_iota(jnp.int32, kflat.shape, 1)
    kflat = kflat + jnp.where(plane == prow * CHUNK, d_ref[...], 0.0)

    srow = lax.broadcasted_iota(jnp.int32, (CHUNK, LANES), 0)
    keep = (lax.broadcasted_iota(jnp.int32, (CHUNK, LANES), 1) % CHUNK) >= srow
    for p in range(SSM_GROUP):
        rows = jnp.broadcast_to(kflat[p:p + 1, :], (CHUNK, width))
        for q in range(PAIRS):
            blk = pltpu.roll(rows[:, q * LANES:(q + 1) * LANES], 0, 1, stride=1, stride_axis=0)
            t_ref[p * CHUNK:(p + 1) * CHUNK, q * LANES:(q + 1) * LANES] = (
                jnp.where(keep, blk, 0.0).astype(BF16))

    e = (CHUNK - 1 - lax.broadcasted_iota(jnp.int32, (CHUNK, 1), 0)).astype(F32)
    m = jnp.exp(ar * dt * e)
    er, ei = m * jnp.cos(ai * dt * e), m * jnp.sin(ai * dt * e)
    zpad = jnp.zeros((CHUNK, LANES - st), F32)
    for p in range(SSM_GROUP):
        br, bi = bbr[p:p + 1, :], bbi[p:p + 1, :]
        ws_ref[p * CHUNK:(p + 1) * CHUNK, :] = jnp.concatenate(
            [er * br - ei * bi, zpad, er * bi + ei * br, zpad], axis=1).astype(BF16)

    pr_, pi_ = c_times_power(1.0)
    zrows = jnp.zeros((LANES - st, width), F32)
    wc_ref[...] = jnp.concatenate([pr_, zrows, -pi_, zrows], axis=0).astype(BF16)

    u = u_ref[...].reshape(n_chunks * bsz, width).astype(BF16)
    s = jnp.dot(u, ws_ref[...], preferred_element_type=F32)
    m64 = jnp.exp(ar * dt * CHUNK)
    zlane = jnp.zeros((1, LANES - st), F32)
    a64r = jnp.concatenate([m64 * jnp.cos(ai * dt * CHUNK), zlane], axis=1)
    a64i = jnp.concatenate([m64 * jnp.sin(ai * dt * CHUNK), zlane], axis=1)
    hr = jnp.zeros((bsz, LANES), F32)
    hi_ = jnp.zeros((bsz, LANES), F32)
    prev = []
    for c in range(n_chunks):
        prev.append(jnp.concatenate([hr, hi_], axis=1))
        sr, si = s[c * bsz:(c + 1) * bsz, :LANES], s[c * bsz:(c + 1) * bsz, LANES:]
        hr, hi_ = a64r * hr - a64i * hi_ + sr, a64r * hi_ + a64i * hr + si
    hprev = jnp.concatenate(prev, axis=0).astype(BF16)

    y = (jnp.dot(u, t_ref[...], preferred_element_type=F32)
         + jnp.dot(hprev, wc_ref[...], preferred_element_type=F32))
    y_ref[...] = y.reshape(n_chunks, bsz, width)


def _s5scan(u4, a_re, a_im, log_dt, b_re, b_im, c_re, c_im, d):
    n_chunks, groups, bsz, width = u4.shape
    st, pg = SSM_STATE, SSM_GROUP
    per_g = lambda *shape: pl.BlockSpec((None,) + shape, lambda g: (g,) + (0,) * len(shape))
    return pl.pallas_call(
        _s5scan_kernel,
        grid=(groups,),
        in_specs=[
            pl.BlockSpec((n_chunks, None, bsz, width), lambda g: (0, g, 0, 0)),
            per_g(1, st), per_g(1, st), per_g(st, 1), per_g(st, 1), per_g(1, 1),
            per_g(pg, st), per_g(pg, st), per_g(st, pg), per_g(st, pg), per_g(pg, 1),
        ],
        out_specs=pl.BlockSpec((n_chunks, None, bsz, width), lambda g: (0, g, 0, 0)),
        out_shape=jax.ShapeDtypeStruct(u4.shape, F32),
        scratch_shapes=[
            pltpu.VMEM((width, width), BF16),
            pltpu.VMEM((width, 2 * LANES), BF16),
            pltpu.VMEM((2 * LANES, width), BF16),
        ],
        compiler_params=_params(("arbitrary",), 32),
        name="s5_scan",
    )(u4,
      a_re.reshape(groups, 1, st), a_im.reshape(groups, 1, st),
      a_re.reshape(groups, st, 1), a_im.reshape(groups, st, 1),
      log_dt.reshape(groups, 1, 1),
      jnp.swapaxes(b_re, 1, 2), jnp.swapaxes(b_im, 1, 2),
      jnp.swapaxes(c_re, 1, 2), jnp.swapaxes(c_im, 1, 2),
      d.reshape(groups, pg, 1))


def _s5out_kernel(y_ref, x_ref, mod_ref, wg_ref, wo_ref, g_ref, b_ref, o_ref, a_ref, s_ref):
    bsz = x_ref.shape[0]
    half = SSM_GROUPS * bsz
    for c2 in range(2):
        for q in range(PAIRS):
            s_ref[q, c2 * half:(c2 + 1) * half, :] = (
                y_ref[c2, :, :, q * LANES:(q + 1) * LANES].reshape(half, LANES))
    lane = lax.broadcasted_iota(jnp.int32, (SSM_GROUPS, LANES), 1)
    for q in range(PAIRS):
        for b in range(bsz):
            c0 = s_ref[q, pl.ds(b, SSM_GROUPS, stride=bsz), :]
            c1 = s_ref[q, pl.ds(half + b, SSM_GROUPS, stride=bsz), :]
            x0, x1 = _lane_halves(c0, c1, lane)
            a_ref[(2 * q) * SSM_GROUPS:(2 * q + 1) * SSM_GROUPS, b * LANES:(b + 1) * LANES] = x0
            a_ref[(2 * q + 1) * SSM_GROUPS:(2 * q + 2) * SSM_GROUPS, b * LANES:(b + 1) * LANES] = x1
    z = jax.nn.gelu(a_ref[...], approximate=True)
    gt = jnp.dot(wg_ref[...], z.astype(BF16), preferred_element_type=F32)
    zz = (z * jax.nn.sigmoid(gt)).astype(BF16)
    y = lax.dot_general(zz, wo_ref[...], (((0,), (0,)), ((), ())), preferred_element_type=F32)
    y = y.reshape(bsz, ROWS, D_MODEL)
    gate = 1.0 + mod_ref[5, :, :][:, None, :]
    o_ref[...] = _layer_norm(ALPHA * x_ref[...] + gate * y, g_ref[...], b_ref[...])


def _s5out(y4, x, mods, wg_t, wo, ln_g, ln_b, *, layer):
    bsz, seq, _ = x.shape
    width = SSM_GROUP * CHUNK
    return pl.pallas_call(
        _s5out_kernel,
        grid=(seq // ROWS,),
        in_specs=[
            pl.BlockSpec((2, SSM_GROUPS, bsz, width), lambda j: (j, 0, 0, 0)),
            pl.BlockSpec((bsz, ROWS, D_MODEL), lambda j: (0, j, 0)),
            _resident((None, 9, bsz, D_MODEL), lambda j: (layer, 0, 0, 0)),
            _resident((D_MODEL, D_MODEL), lambda j: (0, 0)),
            _resident((D_MODEL, D_MODEL), lambda j: (0, 0)),
            _resident((None, None, 1, D_MODEL), lambda j: (layer, 1, 0, 0)),
            _resident((None, None, 1, D_MODEL), lambda j: (layer, 1, 0, 0)),
        ],
        out_specs=pl.BlockSpec((bsz, ROWS, D_MODEL), lambda j: (0, j, 0)),
        out_shape=jax.ShapeDtypeStruct(x.shape, F32),
        scratch_shapes=[
            pltpu.VMEM((D_MODEL, bsz * ROWS), F32),
            pltpu.VMEM((PAIRS, 2 * SSM_GROUPS * bsz, LANES), F32),
        ],
        compiler_params=_params(("arbitrary",), 48),
        name="s5_out",
    )(y4, x, mods, wg_t, wo, ln_g, ln_b)


def kernel(x, c, ada_w, ada_b, ln_g, ln_b, ffn_w1, ffn_w3, ffn_w2, attn_w_in, attn_lam, attn_subln_g, attn_w_out, ssm_w_in, ssm_a_re, ssm_a_im, ssm_log_dt, ssm_b_re, ssm_b_im, ssm_c_re, ssm_c_im, ssm_d, ssm_w_gate, ssm_w_out):
    mods = _ada(c, ada_w, ada_b)
    lng = ln_g.reshape(DEPTH, 3, 1, D_MODEL)
    lnb = ln_b.reshape(DEPTH, 3, 1, D_MODEL)
    w1, w3, w2 = ffn_w1.astype(BF16), ffn_w3.astype(BF16), ffn_w2.astype(BF16)
    gp = (SSM_GROUPS, SSM_GROUP)

    for layer in range(DEPTH):
        i = layer // 2
        x = _ffn(x, mods, w1, w3, w2, lng, lnb, layer=layer, half=0, sub=0)
        if layer % 2 == 0:
            lam_init = 0.8 - 0.6 * math.exp(-0.3 * layer)
            qkv = _modproj(x, mods, attn_w_in[i].astype(BF16), layer=layer, sub=1)
            o = _attention(qkv, attn_lam[i], attn_subln_g[i], lam_init=lam_init)
            x = _projres(o, x, mods, attn_w_out[i].astype(BF16), lng, lnb, layer=layer, sub=1)
        else:
            rows_pg = lambda w: w.reshape(*gp, D_MODEL).transpose(1, 0, 2).reshape(D_MODEL, D_MODEL)
            w_in_t = rows_pg(ssm_w_in[i].astype(BF16).T)
            wg_t = rows_pg(rows_pg(ssm_w_gate[i].astype(BF16)).T)
            wo = rows_pg(ssm_w_out[i].astype(BF16))
            u4 = _s5in(x, mods, w_in_t, layer=layer)
            y4 = _s5scan(u4, ssm_a_re[i], ssm_a_im[i], ssm_log_dt[i], ssm_b_re[i], ssm_b_im[i],
                         ssm_c_re[i], ssm_c_im[i], ssm_d[i])
            x = _s5out(y4, x, mods, wg_t, wo, lng, lnb, layer=layer)
        x = _ffn(x, mods, w1, w3, w2, lng, lnb, layer=layer, half=1, sub=2)
    return x
```

```python
import functools
import math

import jax
import jax.numpy as jnp
from jax import lax
from jax.experimental import pallas as pl
from jax.experimental.pallas import tpu as pltpu

D_MODEL = 1024
DEPTH = 2
CHUNK = 64
ATTN_HEADS = 8
ATTN_HEAD_DIM = 64
SSM_GROUP = 16
SSM_GROUPS = D_MODEL // SSM_GROUP
SSM_STATE = 64
D_FF = 2816
ALPHA = (2 * DEPTH) ** 0.25
LN_EPS = 1e-5

LANES = 128
F_CHUNK = 256
MIB = 1024 * 1024

F32 = jnp.float32
BF16 = jnp.bfloat16


def _params(sem, vmem_mib):
    return pltpu.CompilerParams(dimension_semantics=sem, vmem_limit_bytes=vmem_mib * MIB)


def _resident(block_shape, index_map):
    return pl.BlockSpec(block_shape, index_map, pipeline_mode=pl.Buffered(1))


def _layer_norm(r, g, b):
    mu = jnp.mean(r, axis=-1, keepdims=True)
    d = r - mu
    var = jnp.mean(d * d, axis=-1, keepdims=True)
    return d * lax.rsqrt(var + LN_EPS) * g + b


def _mods(mod_ref, sub, bidx):
    shift = mod_ref[3 * sub + 0, pl.ds(bidx, 1), :]
    scale = mod_ref[3 * sub + 1, pl.ds(bidx, 1), :]
    gate = 1.0 + mod_ref[3 * sub + 2, pl.ds(bidx, 1), :]
    return shift, scale, gate


def _ada_kernel(c_ref, w_ref, b_ref, o_ref):
    c = c_ref[...]
    cond = (c * jax.nn.sigmoid(c)).astype(BF16)
    o_ref[...] = jnp.dot(cond, w_ref[...].astype(BF16), preferred_element_type=F32) + b_ref[...]


def _ada(c, ada_w, ada_b):
    bsz = c.shape[0]
    n_blk = ada_w.shape[2] // D_MODEL
    return pl.pallas_call(
        _ada_kernel,
        grid=(DEPTH, n_blk),
        in_specs=[
            pl.BlockSpec((bsz, D_MODEL), lambda l, n: (0, 0)),
            pl.BlockSpec((None, D_MODEL, D_MODEL), lambda l, n: (l, 0, n)),
            pl.BlockSpec((None, None, 1, D_MODEL), lambda l, n: (l, n, 0, 0)),
        ],
        out_specs=pl.BlockSpec((None, None, bsz, D_MODEL), lambda l, n: (l, n, 0, 0)),
        out_shape=jax.ShapeDtypeStruct((DEPTH, n_blk, bsz, D_MODEL), F32),
        compiler_params=_params(("arbitrary", "arbitrary"), 24),
        name="ada_mods",
    )(c, ada_w, ada_b.reshape(DEPTH, n_blk, 1, D_MODEL))


def _ffn_kernel(x_ref, mod_ref, w1_ref, w3_ref, w2_ref, g_ref, b_ref, o_ref, *, sub, ts):
    shift, scale, gate = _mods(mod_ref, sub, pl.program_id(0))
    n_sub = x_ref.shape[0] // ts
    n_f = D_FF // F_CHUNK

    def prologue(t):
        return (x_ref[t * ts:(t + 1) * ts, :] * (1.0 + scale) + shift).astype(BF16)

    def up(h, f):
        sl = slice(f * F_CHUNK, (f + 1) * F_CHUNK)
        return (jnp.dot(h, w1_ref[:, sl], preferred_element_type=F32),
                jnp.dot(h, w3_ref[:, sl], preferred_element_type=F32))

    def down(ab, acc, f):
        a, b = ab
        u = (a * jax.nn.sigmoid(a) * b).astype(BF16)
        d = jnp.dot(u, w2_ref[f * F_CHUNK:(f + 1) * F_CHUNK, :], preferred_element_type=F32)
        return d if acc is None else acc + d

    def epilogue(t, acc):
        r = ALPHA * x_ref[t * ts:(t + 1) * ts, :] + gate * (0.5 * acc)
        o_ref[t * ts:(t + 1) * ts, :] = _layer_norm(r, g_ref[...], b_ref[...])

    h = prologue(0)
    nxt = up(h, 0)
    for t in range(n_sub):
        acc = None
        for f in range(n_f):
            cur = nxt
            if f + 1 < n_f:
                nxt = up(h, f + 1)
            elif t + 1 < n_sub:
                nxt = up(h_next, 0)
            if f == n_f // 2 and t + 1 < n_sub:
                h_next = prologue(t + 1)
            acc = down(cur, acc, f)
        epilogue(t, acc)
        if t + 1 < n_sub:
            h = h_next


def _ffn(x, mods, w1, w3, w2, ln_g, ln_b, *, layer, half, sub, tm=1024, ts=256):
    bsz, seq, _ = x.shape
    wmap = lambda b, i: (layer, half, 0, 0)
    return pl.pallas_call(
        functools.partial(_ffn_kernel, sub=sub, ts=ts),
        grid=(bsz, seq // tm),
        in_specs=[
            pl.BlockSpec((None, tm, D_MODEL), lambda b, i: (b, i, 0)),
            _resident((None, 9, bsz, D_MODEL), lambda b, i: (layer, 0, 0, 0)),
            _resident((None, None, D_MODEL, D_FF), wmap),
            _resident((None, None, D_MODEL, D_FF), wmap),
            _resident((None, None, D_FF, D_MODEL), wmap),
            _resident((None, None, 1, D_MODEL), lambda b, i: (layer, sub, 0, 0)),
            _resident((None, None, 1, D_MODEL), lambda b, i: (layer, sub, 0, 0)),
        ],
        out_specs=pl.BlockSpec((None, tm, D_MODEL), lambda b, i: (b, i, 0)),
        out_shape=jax.ShapeDtypeStruct(x.shape, F32),
        compiler_params=_params(("arbitrary", "arbitrary"), 56),
        name=f"ffn_l{layer}_h{half}",
    )(x, mods, w1, w3, w2, ln_g, ln_b)


def _modproj_kernel(x_ref, mod_ref, w_ref, o_ref, *, sub):
    shift, scale, _ = _mods(mod_ref, sub, pl.program_id(0))
    h = (x_ref[...] * (1.0 + scale) + shift).astype(BF16)
    for n in range(o_ref.shape[1] // D_MODEL):
        cols = slice(n * D_MODEL, (n + 1) * D_MODEL)
        o_ref[:, cols] = jnp.dot(h, w_ref[:, cols], preferred_element_type=F32).astype(o_ref.dtype)


def _modproj(x, mods, w, *, layer, sub, tm=1024):
    bsz, seq, _ = x.shape
    n_out = w.shape[1]
    return pl.pallas_call(
        functools.partial(_modproj_kernel, sub=sub),
        grid=(bsz, seq // tm),
        in_specs=[
            pl.BlockSpec((None, tm, D_MODEL), lambda b, i: (b, i, 0)),
            _resident((None, 9, bsz, D_MODEL), lambda b, i: (layer, 0, 0, 0)),
            _resident((D_MODEL, n_out), lambda b, i: (0, 0)),
        ],
        out_specs=pl.BlockSpec((None, tm, n_out), lambda b, i: (b, i, 0)),
        out_shape=jax.ShapeDtypeStruct((bsz, seq, n_out), BF16),
        compiler_params=_params(("arbitrary", "arbitrary"), 44),
        name=f"modproj_l{layer}",
    )(x, mods, w)


def _projres_kernel(a_ref, x_ref, mod_ref, w_ref, g_ref, b_ref, o_ref, *, sub, ts):
    _, _, gate = _mods(mod_ref, sub, pl.program_id(0))
    n_sub = x_ref.shape[0] // ts
    proj = lambda t: jnp.dot(a_ref[t * ts:(t + 1) * ts, :], w_ref[...], preferred_element_type=F32)
    nxt = proj(0)
    for t in range(n_sub):
        y, rows = nxt, slice(t * ts, (t + 1) * ts)
        if t + 1 < n_sub:
            nxt = proj(t + 1)
        o_ref[rows, :] = _layer_norm(ALPHA * x_ref[rows, :] + gate * y, g_ref[...], b_ref[...])


def _projres(a, x, mods, w, ln_g, ln_b, *, layer, sub, tm=1024, ts=256):
    bsz, seq, _ = x.shape
    return pl.pallas_call(
        functools.partial(_projres_kernel, sub=sub, ts=ts),
        grid=(bsz, seq // tm),
        in_specs=[
            pl.BlockSpec((None, tm, D_MODEL), lambda b, i: (b, i, 0)),
            pl.BlockSpec((None, tm, D_MODEL), lambda b, i: (b, i, 0)),
            _resident((None, 9, bsz, D_MODEL), lambda b, i: (layer, 0, 0, 0)),
            _resident((D_MODEL, D_MODEL), lambda b, i: (0, 0)),
            _resident((None, None, 1, D_MODEL), lambda b, i: (layer, sub, 0, 0)),
            _resident((None, None, 1, D_MODEL), lambda b, i: (layer, sub, 0, 0)),
        ],
        out_specs=pl.BlockSpec((None, tm, D_MODEL), lambda b, i: (b, i, 0)),
        out_shape=jax.ShapeDtypeStruct(x.shape, F32),
        compiler_params=_params(("arbitrary", "arbitrary"), 36),
        name=f"projres_l{layer}",
    )(a, x, mods, w, ln_g, ln_b)


def _attn_kernel(q_ref, k_ref, v_ref, lam_ref, sg_ref, o_ref, vx_ref, *, lam_init, tq):
    seq = q_ref.shape[0]
    hd = ATTN_HEAD_DIM
    lam = lam_ref[...]
    lam_full = (jnp.exp(jnp.sum(lam[0:1] * lam[1:2], axis=-1, keepdims=True))
                - jnp.exp(jnp.sum(lam[2:3] * lam[3:4], axis=-1, keepdims=True)) + lam_init)
    lane = lax.broadcasted_iota(jnp.int32, (tq, 2 * hd), 1)
    rq = lax.broadcasted_iota(jnp.int32, (tq, tq), 0) // CHUNK
    ck = lax.broadcasted_iota(jnp.int32, (tq, tq), 1) // CHUNK
    allowed = ck <= rq
    nt = (((1,), (1,)), ((), ()))

    hw = 2 * hd
    vx_ref[:, :hw] = v_ref[...]
    vx_ref[:, hw:] = jnp.ones((seq, hw), BF16)

    def score(i, m):
        kend = (i + 1) * tq
        q = q_ref[i * tq:kend, :].astype(F32) * (hd ** -0.5)
        qm = jnp.where((lane < hd) if m == 0 else (lane >= hd), q, 0.0).astype(BF16)
        s = lax.dot_general(qm, k_ref[0:kend, :], nt, preferred_element_type=F32)
        diag = jnp.where(allowed, s[:, kend - tq:], -jnp.inf)
        return diag if kend == tq else jnp.concatenate([s[:, :kend - tq], diag], axis=1)

    def attend(i, s):
        kend = (i + 1) * tq
        p = jnp.exp(s - jnp.max(s, axis=-1, keepdims=True))
        r = jnp.dot(p.astype(BF16), vx_ref[0:kend, :], preferred_element_type=F32)
        return r[:, :hw] / r[:, hw:]

    def finish(i, o1, o2):
        o = o1 - lam_full * o2
        o = o * lax.rsqrt(jnp.mean(o * o, axis=-1, keepdims=True) + LN_EPS)
        o_ref[i * tq:(i + 1) * tq, :] = (o * sg_ref[...] * (1.0 - lam_init)).astype(o_ref.dtype)

    n = seq // tq
    units = [(i, m) for i in reversed(range(n)) for m in range(2)]
    pend = score(*units[0])
    outs = {}
    for u, (i, m) in enumerate(units):
        cur = pend
        if u + 1 < len(units):
            pend = score(*units[u + 1])
        outs[(i, m)] = attend(i, cur)
        if m == 1:
            finish(i, outs.pop((i, 0)), outs.pop((i, 1)))


def _attention(qkv, lam, subln_g, *, lam_init, tq=256):
    bsz, seq, _ = qkv.shape
    hw = 2 * ATTN_HEAD_DIM
    return pl.pallas_call(
        functools.partial(_attn_kernel, lam_init=lam_init, tq=tq),
        grid=(bsz, ATTN_HEADS),
        in_specs=[
            pl.BlockSpec((None, seq, hw), lambda b, h: (b, 0, h)),
            pl.BlockSpec((None, seq, hw), lambda b, h: (b, 0, ATTN_HEADS + h)),
            pl.BlockSpec((None, seq, hw), lambda b, h: (b, 0, 2 * ATTN_HEADS + h)),
            _resident((4, ATTN_HEAD_DIM), lambda b, h: (0, 0)),
            _resident((1, hw), lambda b, h: (0, 0)),
        ],
        out_specs=pl.BlockSpec((None, seq, hw), lambda b, h: (b, 0, h)),
        out_shape=jax.ShapeDtypeStruct((bsz, seq, D_MODEL), BF16),
        scratch_shapes=[pltpu.VMEM((seq, 2 * hw), BF16)],
        compiler_params=_params(("arbitrary", "arbitrary"), 40),
        name="diff_attention",
    )(qkv, qkv, qkv, lam, subln_g.reshape(1, hw))


ROWS = 2 * CHUNK
PAIRS = SSM_GROUP // 2


def _lane_halves(e, o, lane):
    lo = jnp.where(lane < CHUNK, e, pltpu.roll(o, CHUNK, 1))
    hi = jnp.where(lane < CHUNK, pltpu.roll(e, CHUNK, 1), o)
    return lo, hi


def _s5in_kernel(x_ref, mod_ref, w_ref, o_ref, a_ref, s_ref):
    bsz = x_ref.shape[0]
    shift = mod_ref[3, :, :][:, None, :]
    scale = mod_ref[4, :, :][:, None, :]
    h = (x_ref[...] * (1.0 + scale) + shift).astype(BF16).reshape(bsz * ROWS, D_MODEL)
    a_ref[...] = lax.dot_general(w_ref[...], h, (((1,), (1,)), ((), ())), preferred_element_type=F32)
    lane = lax.broadcasted_iota(jnp.int32, (SSM_GROUPS, LANES), 1)
    half = SSM_GROUPS * bsz
    for q in range(PAIRS):
        for b in range(bsz):
            x0 = a_ref[(2 * q) * SSM_GROUPS:(2 * q + 1) * SSM_GROUPS, b * LANES:(b + 1) * LANES]
            x1 = a_ref[(2 * q + 1) * SSM_GROUPS:(2 * q + 2) * SSM_GROUPS, b * LANES:(b + 1) * LANES]
            c0, c1 = _lane_halves(x0, x1, lane)
            s_ref[q, pl.ds(b, SSM_GROUPS, stride=bsz), :] = c0
            s_ref[q, pl.ds(half + b, SSM_GROUPS, stride=bsz), :] = c1
    for c2 in range(2):
        for q in range(PAIRS):
            o_ref[c2, :, :, q * LANES:(q + 1) * LANES] = (
                s_ref[q, c2 * half:(c2 + 1) * half, :].reshape(SSM_GROUPS, bsz, LANES))


def _s5in(x, mods, w_t, *, layer):
    bsz, seq, _ = x.shape
    width = SSM_GROUP * CHUNK
    return pl.pallas_call(
        _s5in_kernel,
        grid=(seq // ROWS,),
        in_specs=[
            pl.BlockSpec((bsz, ROWS, D_MODEL), lambda j: (0, j, 0)),
            _resident((None, 9, bsz, D_MODEL), lambda j: (layer, 0, 0, 0)),
            _resident((D_MODEL, D_MODEL), lambda j: (0, 0)),
        ],
        out_specs=pl.BlockSpec((2, SSM_GROUPS, bsz, width), lambda j: (j, 0, 0, 0)),
        out_shape=jax.ShapeDtypeStruct((seq // CHUNK, SSM_GROUPS, bsz, width), F32),
        scratch_shapes=[
            pltpu.VMEM((D_MODEL, bsz * ROWS), F32),
            pltpu.VMEM((PAIRS, 2 * SSM_GROUPS * bsz, LANES), F32),
        ],
        compiler_params=_params(("arbitrary",), 40),
        name="s5_in",
    )(x, mods, w_t)


def _s5scan_kernel(u_ref, arr_ref, air_ref, arc_ref, aic_ref, ldt_ref, btr_ref, bti_ref,
                   ctr_ref, cti_ref, d_ref, y_ref, t_ref, ws_ref, wc_ref):
    n_chunks, bsz, width = u_ref.shape
    st = SSM_STATE
    dt = jnp.exp(ldt_ref[...])

    ar, ai = arr_ref[...], air_ref[...]
    mag = jnp.exp(ar * dt)
    abr, abi = mag * jnp.cos(ai * dt), mag * jnp.sin(ai * dt)
    den = ar * ar + ai * ai
    pr, pim = abr - 1.0, abi
    cfr, cfi = (pr * ar + pim * ai) / den, (pim * ar - pr * ai) / den
    btr, bti = btr_ref[...], bti_ref[...]
    bbr, bbi = cfr * btr - cfi * bti, cfr * bti + cfi * btr

    arc, aic = arc_ref[...], aic_ref[...]
    lane = lax.broadcasted_iota(jnp.int32, (1, LANES), 1)
    lag = (lane % CHUNK).astype(F32)
    first = lane < CHUNK
    ctr, cti = ctr_ref[...], cti_ref[...]

    def c_times_power(shift):
        e = lag + shift
        m = jnp.exp(arc * dt * e)
        er, ei = m * jnp.cos(aic * dt * e), m * jnp.sin(aic * dt * e)
        xr, xi = [], []
        for q in range(PAIRS):
            cr = jnp.where(first, ctr[:, 2 * q:2 * q + 1], ctr[:, 2 * q + 1:2 * q + 2])
            ci = jnp.where(first, cti[:, 2 * q:2 * q + 1], cti[:, 2 * q + 1:2 * q + 2])
            xr.append(cr * er - ci * ei)
            xi.append(cr * ei + ci * er)
        return jnp.concatenate(xr, axis=1), jnp.concatenate(xi, axis=1)

    xr, xi = c_times_power(0.0)
    hi = lax.Precision.HIGHEST
    kflat = (jnp.dot(bbr, xr, precision=hi, preferred_element_type=F32)
             - jnp.dot(bbi, xi, precision=hi, preferred_element_type=F32))
    prow = lax.broadcasted_iota(jnp.int32, kflat.shape, 0)
    plane = lax.broadcasted_iota(jnp.int32, kflat.shape, 1)
    kflat = kflat + jnp.where(plane == prow * CHUNK, d_ref[...], 0.0)

    srow = lax.broadcasted_iota(jnp.int32, (CHUNK, LANES), 0)
    keep = (lax.broadcasted_iota(jnp.int32, (CHUNK, LANES), 1) % CHUNK) >= srow
    for p in range(SSM_GROUP):
        rows = jnp.broadcast_to(kflat[p:p + 1, :], (CHUNK, width))
        for q in range(PAIRS):
            blk = pltpu.roll(rows[:, q * LANES:(q + 1) * LANES], 0, 1, stride=1, stride_axis=0)
            t_ref[p * CHUNK:(p + 1) * CHUNK, q * LANES:(q + 1) * LANES] = (
                jnp.where(keep, blk, 0.0).astype(BF16))

    e = (CHUNK - 1 - lax.broadcasted_iota(jnp.int32, (CHUNK, 1), 0)).astype(F32)
    m = jnp.exp(ar * dt * e)
    er, ei = m * jnp.cos(ai * dt * e), m * jnp.sin(ai * dt * e)
    zpad = jnp.zeros((CHUNK, LANES - st), F32)
    for p in range(SSM_GROUP):
        br, bi = bbr[p:p + 1, :], bbi[p:p + 1, :]
        ws_ref[p * CHUNK:(p + 1) * CHUNK, :] = jnp.concatenate(
            [er * br - ei * bi, zpad, er * bi + ei * br, zpad], axis=1).astype(BF16)

    pr_, pi_ = c_times_power(1.0)
    zrows = jnp.zeros((LANES - st, width), F32)
    wc_ref[...] = jnp.concatenate([pr_, zrows, -pi_, zrows], axis=0).astype(BF16)

    u = u_ref[...].reshape(n_chunks * bsz, width).astype(BF16)
    s = jnp.dot(u, ws_ref[...], preferred_element_type=F32)
    m64 = jnp.exp(ar * dt * CHUNK)
    zlane = jnp.zeros((1, LANES - st), F32)
    a64r = jnp.concatenate([m64 * jnp.cos(ai * dt * CHUNK), zlane], axis=1)
    a64i = jnp.concatenate([m64 * jnp.sin(ai * dt * CHUNK), zlane], axis=1)
    hr = jnp.zeros((bsz, LANES), F32)
    hi_ = jnp.zeros((bsz, LANES), F32)
    prev = []
    for c in range(n_chunks):
        prev.append(jnp.concatenate([hr, hi_], axis=1))
        sr, si = s[c * bsz:(c + 1) * bsz, :LANES], s[c * bsz:(c + 1) * bsz, LANES:]
        hr, hi_ = a64r * hr - a64i * hi_ + sr, a64r * hi_ + a64i * hr + si
    hprev = jnp.concatenate(prev, axis=0).astype(BF16)

    y = (jnp.dot(u, t_ref[...], preferred_element_type=F32)
         + jnp.dot(hprev, wc_ref[...], preferred_element_type=F32))
    y_ref[...] = y.reshape(n_chunks, bsz, width)


def _s5scan(u4, a_re, a_im, log_dt, b_re, b_im, c_re, c_im, d):
    n_chunks, groups, bsz, width = u4.shape
    st, pg = SSM_STATE, SSM_GROUP
    per_g = lambda *shape: pl.BlockSpec((None,) + shape, lambda g: (g,) + (0,) * len(shape))
    return pl.pallas_call(
        _s5scan_kernel,
        grid=(groups,),
        in_specs=[
            pl.BlockSpec((n_chunks, None, bsz, width), lambda g: (0, g, 0, 0)),
            per_g(1, st), per_g(1, st), per_g(st, 1), per_g(st, 1), per_g(1, 1),
            per_g(pg, st), per_g(pg, st), per_g(st, pg), per_g(st, pg), per_g(pg, 1),
        ],
        out_specs=pl.BlockSpec((n_chunks, None, bsz, width), lambda g: (0, g, 0, 0)),
        out_shape=jax.ShapeDtypeStruct(u4.shape, F32),
        scratch_shapes=[
            pltpu.VMEM((width, width), BF16),
            pltpu.VMEM((width, 2 * LANES), BF16),
            pltpu.VMEM((2 * LANES, width), BF16),
        ],
        compiler_params=_params(("arbitrary",), 32),
        name="s5_scan",
    )(u4,
      a_re.reshape(groups, 1, st), a_im.reshape(groups, 1, st),
      a_re.reshape(groups, st, 1), a_im.reshape(groups, st, 1),
      log_dt.reshape(groups, 1, 1),
      jnp.swapaxes(b_re, 1, 2), jnp.swapaxes(b_im, 1, 2),
      jnp.swapaxes(c_re, 1, 2), jnp.swapaxes(c_im, 1, 2),
      d.reshape(groups, pg, 1))


def _s5out_kernel(y_ref, x_ref, mod_ref, wg_ref, wo_ref, g_ref, b_ref, o_ref, a_ref, s_ref):
    bsz = x_ref.shape[0]
    half = SSM_GROUPS * bsz
    for c2 in range(2):
        for q in range(PAIRS):
            s_ref[q, c2 * half:(c2 + 1) * half, :] = (
                y_ref[c2, :, :, q * LANES:(q + 1) * LANES].reshape(half, LANES))
    lane = lax.broadcasted_iota(jnp.int32, (SSM_GROUPS, LANES), 1)
    for q in range(PAIRS):
        for b in range(bsz):
            c0 = s_ref[q, pl.ds(b, SSM_GROUPS, stride=bsz), :]
            c1 = s_ref[q, pl.ds(half + b, SSM_GROUPS, stride=bsz), :]
            x0, x1 = _lane_halves(c0, c1, lane)
            a_ref[(2 * q) * SSM_GROUPS:(2 * q + 1) * SSM_GROUPS, b * LANES:(b + 1) * LANES] = x0
            a_ref[(2 * q + 1) * SSM_GROUPS:(2 * q + 2) * SSM_GROUPS, b * LANES:(b + 1) * LANES] = x1
    z = jax.nn.gelu(a_ref[...], approximate=True)
    gt = jnp.dot(wg_ref[...], z.astype(BF16), preferred_element_type=F32)
    zz = (z * jax.nn.sigmoid(gt)).astype(BF16)
    y = lax.dot_general(zz, wo_ref[...], (((0,), (0,)), ((), ())), preferred_element_type=F32)
    y = y.reshape(bsz, ROWS, D_MODEL)
    gate = 1.0 + mod_ref[5, :, :][:, None, :]
    o_ref[...] = _layer_norm(ALPHA * x_ref[...] + gate * y, g_ref[...], b_ref[...])


def _s5out(y4, x, mods, wg_t, wo, ln_g, ln_b, *, layer):
    bsz, seq, _ = x.shape
    width = SSM_GROUP * CHUNK
    return pl.pallas_call(
        _s5out_kernel,
        grid=(seq // ROWS,),
        in_specs=[
            pl.BlockSpec((2, SSM_GROUPS, bsz, width), lambda j: (j, 0, 0, 0)),
            pl.BlockSpec((bsz, ROWS, D_MODEL), lambda j: (0, j, 0)),
            _resident((None, 9, bsz, D_MODEL), lambda j: (layer, 0, 0, 0)),
            _resident((D_MODEL, D_MODEL), lambda j: (0, 0)),
            _resident((D_MODEL, D_MODEL), lambda j: (0, 0)),
            _resident((None, None, 1, D_MODEL), lambda j: (layer, 1, 0, 0)),
            _resident((None, None, 1, D_MODEL), lambda j: (layer, 1, 0, 0)),
        ],
        out_specs=pl.BlockSpec((bsz, ROWS, D_MODEL), lambda j: (0, j, 0)),
        out_shape=jax.ShapeDtypeStruct(x.shape, F32),
        scratch_shapes=[
            pltpu.VMEM((D_MODEL, bsz * ROWS), F32),
            pltpu.VMEM((PAIRS, 2 * SSM_GROUPS * bsz, LANES), F32),
        ],
        compiler_params=_params(("arbitrary",), 48),
        name="s5_out",
    )(y4, x, mods, wg_t, wo, ln_g, ln_b)


def kernel(x, c, ada_w, ada_b, ln_g, ln_b, ffn_w1, ffn_w3, ffn_w2, attn_w_in, attn_lam, attn_subln_g, attn_w_out, ssm_w_in, ssm_a_re, ssm_a_im, ssm_log_dt, ssm_b_re, ssm_b_im, ssm_c_re, ssm_c_im, ssm_d, ssm_w_gate, ssm_w_out):
    mods = _ada(c, ada_w, ada_b)
    lng = ln_g.reshape(DEPTH, 3, 1, D_MODEL)
    lnb = ln_b.reshape(DEPTH, 3, 1, D_MODEL)
    w1, w3, w2 = ffn_w1.astype(BF16), ffn_w3.astype(BF16), ffn_w2.astype(BF16)
    gp = (SSM_GROUPS, SSM_GROUP)

    for layer in range(DEPTH):
        i = layer // 2
        x = _ffn(x, mods, w1, w3, w2, lng, lnb, layer=layer, half=0, sub=0)
        if layer % 2 == 0:
            lam_init = 0.8 - 0.6 * math.exp(-0.3 * layer)
            qkv = _modproj(x, mods, attn_w_in[i].astype(BF16), layer=layer, sub=1)
            o = _attention(qkv, attn_lam[i], attn_subln_g[i], lam_init=lam_init)
            x = _projres(o, x, mods, attn_w_out[i].astype(BF16), lng, lnb, layer=layer, sub=1)
        else:
            rows_pg = lambda w: w.reshape(*gp, D_MODEL).transpose(1, 0, 2).reshape(D_MODEL, D_MODEL)
            w_in_t = rows_pg(ssm_w_in[i].astype(BF16).T)
            wg_t = rows_pg(rows_pg(ssm_w_gate[i].astype(BF16)).T)
            wo = rows_pg(ssm_w_out[i].astype(BF16))
            u4 = _s5in(x, mods, w_in_t, layer=layer)
            y4 = _s5scan(u4, ssm_a_re[i], ssm_a_im[i], ssm_log_dt[i], ssm_b_re[i], ssm_b_im[i],
                         ssm_c_re[i], ssm_c_im[i], ssm_d[i])
            x = _s5out(y4, x, mods, wg_t, wo, lng, lnb, layer=layer)
        x = _ffn(x, mods, w1, w3, w2, lng, lnb, layer=layer, half=1, sub=2)
    return x
```

```python
import functools
import math

import jax
import jax.numpy as jnp
from jax import lax
from jax.experimental import pallas as pl
from jax.experimental.pallas import tpu as pltpu

D_MODEL = 1024
DEPTH = 2
CHUNK = 64
ATTN_HEADS = 8
ATTN_HEAD_DIM = 64
SSM_GROUP = 16
SSM_GROUPS = D_MODEL // SSM_GROUP
SSM_STATE = 64
D_FF = 2816
ALPHA = (2 * DEPTH) ** 0.25
LN_EPS = 1e-5

LANES = 128
F_CHUNK = 256
MIB = 1024 * 1024

F32 = jnp.float32
BF16 = jnp.bfloat16


def _params(sem, vmem_mib):
    return pltpu.CompilerParams(dimension_semantics=sem, vmem_limit_bytes=vmem_mib * MIB)


def _resident(block_shape, index_map):
    return pl.BlockSpec(block_shape, index_map, pipeline_mode=pl.Buffered(1))


def _layer_norm(r, g, b):
    mu = jnp.mean(r, axis=-1, keepdims=True)
    d = r - mu
    var = jnp.mean(d * d, axis=-1, keepdims=True)
    return d * lax.rsqrt(var + LN_EPS) * g + b


def _mods(mod_ref, sub, bidx):
    shift = mod_ref[3 * sub + 0, pl.ds(bidx, 1), :]
    scale = mod_ref[3 * sub + 1, pl.ds(bidx, 1), :]
    gate = 1.0 + mod_ref[3 * sub + 2, pl.ds(bidx, 1), :]
    return shift, scale, gate


def _ada_kernel(c_ref, w_ref, b_ref, o_ref):
    c = c_ref[...]
    cond = (c * jax.nn.sigmoid(c)).astype(BF16)
    o_ref[...] = jnp.dot(cond, w_ref[...].astype(BF16), preferred_element_type=F32) + b_ref[...]


def _ada(c, ada_w, ada_b):
    bsz = c.shape[0]
    n_blk = ada_w.shape[2] // D_MODEL
    return pl.pallas_call(
        _ada_kernel,
        grid=(DEPTH, n_blk),
        in_specs=[
            pl.BlockSpec((bsz, D_MODEL), lambda l, n: (0, 0)),
            pl.BlockSpec((None, D_MODEL, D_MODEL), lambda l, n: (l, 0, n)),
            pl.BlockSpec((None, None, 1, D_MODEL), lambda l, n: (l, n, 0, 0)),
        ],
        out_specs=pl.BlockSpec((None, None, bsz, D_MODEL), lambda l, n: (l, n, 0, 0)),
        out_shape=jax.ShapeDtypeStruct((DEPTH, n_blk, bsz, D_MODEL), F32),
        compiler_params=_params(("arbitrary", "arbitrary"), 24),
        name="ada_mods",
    )(c, ada_w, ada_b.reshape(DEPTH, n_blk, 1, D_MODEL))


def _ffn_kernel(x_ref, mod_ref, w1_hbm, w3_hbm, w2_hbm, g_ref, b_ref, o_ref,
                w1_ref, w3_ref, w2_ref, s1_ref, s3_ref, s2_ref, sem, *, sub, ts, layer, half):
    shift, scale, gate = _mods(mod_ref, sub, pl.program_id(0))
    n_f = D_FF // F_CHUNK
    chunk = lambda f: slice(f * F_CHUNK, (f + 1) * F_CHUNK)

    def modulated(rows):
        return (x_ref[rows, :] * (1.0 + scale) + shift).astype(BF16)

    def up(h, f):
        return (jnp.dot(h, w1_ref[:, chunk(f)], preferred_element_type=F32),
                jnp.dot(h, w3_ref[:, chunk(f)], preferred_element_type=F32))

    def down(ab, acc, f):
        a, b = ab
        u = (a * jax.nn.sigmoid(a) * b).astype(BF16)
        d = jnp.dot(u, w2_ref[chunk(f), :], preferred_element_type=F32)
        return d if acc is None else acc + d

    def epilogue(rows, acc):
        r = ALPHA * x_ref[rows, :] + gate * (0.5 * acc)
        o_ref[rows, :] = _layer_norm(r, g_ref[...], b_ref[...])

    def weight_copies(f, slot):
        return (pltpu.make_async_copy(w1_hbm.at[layer, half, :, chunk(f)], s1_ref.at[slot], sem.at[0, slot]),
                pltpu.make_async_copy(w3_hbm.at[layer, half, :, chunk(f)], s3_ref.at[slot], sem.at[1, slot]),
                pltpu.make_async_copy(w2_hbm.at[layer, half, chunk(f), :], s2_ref.at[slot], sem.at[2, slot]))

    first = jnp.logical_and(pl.program_id(0) == 0, pl.program_id(1) == 0)

    n_sub = x_ref.shape[0] // ts
    tile = lambda t: slice(t * ts, (t + 1) * ts)

    def sub_tiles(t0):
        h = modulated(tile(t0))
        nxt = up(h, 0)
        for t in range(t0, n_sub):
            acc = None
            for f in range(n_f):
                cur = nxt
                if f + 1 < n_f:
                    nxt = up(h, f + 1)
                elif t + 1 < n_sub:
                    nxt = up(h_next, 0)
                if f == n_f // 2 and t + 1 < n_sub:
                    h_next = modulated(tile(t + 1))
                acc = down(cur, acc, f)
            epilogue(tile(t), acc)
            if t + 1 < n_sub:
                h = h_next

    @pl.when(first)
    def _():
        n_lead = n_sub // 2
        rows = slice(0, n_lead * ts)
        for c in weight_copies(0, 0):
            c.start()
        h = modulated(rows)
        acc = None
        for f in range(n_f):
            slot = f % 2
            if f + 1 < n_f:
                for c in weight_copies(f + 1, 1 - slot):
                    c.start()
            for c in weight_copies(f, slot):
                c.wait()
            w1_ref[:, chunk(f)] = s1_ref[slot].astype(BF16)
            w3_ref[:, chunk(f)] = s3_ref[slot].astype(BF16)
            w2_ref[chunk(f), :] = s2_ref[slot].astype(BF16)
            acc = down(up(h, f), acc, f)
        epilogue(rows, acc)
        sub_tiles(n_lead)

    @pl.when(jnp.logical_not(first))
    def _():
        sub_tiles(0)


def _ffn(x, mods, w1, w3, w2, ln_g, ln_b, *, layer, half, sub, tm=1024, ts=256):
    bsz, seq, _ = x.shape
    stage = lambda *shape: pltpu.VMEM((2,) + shape, F32)
    return pl.pallas_call(
        functools.partial(_ffn_kernel, sub=sub, ts=ts, layer=layer, half=half),
        grid=(bsz, seq // tm),
        in_specs=[
            pl.BlockSpec((None, tm, D_MODEL), lambda b, i: (b, i, 0)),
            _resident((None, 9, bsz, D_MODEL), lambda b, i: (layer, 0, 0, 0)),
            pl.BlockSpec(memory_space=pl.ANY),
            pl.BlockSpec(memory_space=pl.ANY),
            pl.BlockSpec(memory_space=pl.ANY),
            _resident((None, None, 1, D_MODEL), lambda b, i: (layer, sub, 0, 0)),
            _resident((None, None, 1, D_MODEL), lambda b, i: (layer, sub, 0, 0)),
        ],
        out_specs=pl.BlockSpec((None, tm, D_MODEL), lambda b, i: (b, i, 0)),
        out_shape=jax.ShapeDtypeStruct(x.shape, F32),
        scratch_shapes=[
            pltpu.VMEM((D_MODEL, D_FF), BF16), pltpu.VMEM((D_MODEL, D_FF), BF16),
            pltpu.VMEM((D_FF, D_MODEL), BF16),
            stage(D_MODEL, F_CHUNK), stage(D_MODEL, F_CHUNK), stage(F_CHUNK, D_MODEL),
            pltpu.SemaphoreType.DMA((3, 2)),
        ],
        compiler_params=_params(("arbitrary", "arbitrary"), 58),
        name=f"ffn_l{layer}_h{half}",
    )(x, mods, w1, w3, w2, ln_g, ln_b)


def _modproj_kernel(x_ref, mod_ref, w_ref, o_ref, *, sub):
    shift, scale, _ = _mods(mod_ref, sub, pl.program_id(0))
    h = (x_ref[...] * (1.0 + scale) + shift).astype(BF16)
    for n in range(o_ref.shape[1] // D_MODEL):
        cols = slice(n * D_MODEL, (n + 1) * D_MODEL)
        o_ref[:, cols] = jnp.dot(h, w_ref[:, cols], preferred_element_type=F32).astype(o_ref.dtype)


def _modproj(x, mods, w, *, layer, sub, tm=1024):
    bsz, seq, _ = x.shape
    n_out = w.shape[1]
    return pl.pallas_call(
        functools.partial(_modproj_kernel, sub=sub),
        grid=(bsz, seq // tm),
        in_specs=[
            pl.BlockSpec((None, tm, D_MODEL), lambda b, i: (b, i, 0)),
            _resident((None, 9, bsz, D_MODEL), lambda b, i: (layer, 0, 0, 0)),
            _resident((D_MODEL, n_out), lambda b, i: (0, 0)),
        ],
        out_specs=pl.BlockSpec((None, tm, n_out), lambda b, i: (b, i, 0)),
        out_shape=jax.ShapeDtypeStruct((bsz, seq, n_out), BF16),
        compiler_params=_params(("arbitrary", "arbitrary"), 44),
        name=f"modproj_l{layer}",
    )(x, mods, w)


def _projres_kernel(a_ref, x_ref, mod_ref, w_ref, g_ref, b_ref, o_ref, *, sub, ts):
    _, _, gate = _mods(mod_ref, sub, pl.program_id(0))
    n_sub = x_ref.shape[0] // ts
    proj = lambda t: jnp.dot(a_ref[t * ts:(t + 1) * ts, :], w_ref[...], preferred_element_type=F32)
    nxt = proj(0)
    for t in range(n_sub):
        y, rows = nxt, slice(t * ts, (t + 1) * ts)
        if t + 1 < n_sub:
            nxt = proj(t + 1)
        o_ref[rows, :] = _layer_norm(ALPHA * x_ref[rows, :] + gate * y, g_ref[...], b_ref[...])


def _projres(a, x, mods, w, ln_g, ln_b, *, layer, sub, tm=1024, ts=256):
    bsz, seq, _ = x.shape
    return pl.pallas_call(
        functools.partial(_projres_kernel, sub=sub, ts=ts),
        grid=(bsz, seq // tm),
        in_specs=[
            pl.BlockSpec((None, tm, D_MODEL), lambda b, i: (b, i, 0)),
            pl.BlockSpec((None, tm, D_MODEL), lambda b, i: (b, i, 0)),
            _resident((None, 9, bsz, D_MODEL), lambda b, i: (layer, 0, 0, 0)),
            _resident((D_MODEL, D_MODEL), lambda b, i: (0, 0)),
            _resident((None, None, 1, D_MODEL), lambda b, i: (layer, sub, 0, 0)),
            _resident((None, None, 1, D_MODEL), lambda b, i: (layer, sub, 0, 0)),
        ],
        out_specs=pl.BlockSpec((None, tm, D_MODEL), lambda b, i: (b, i, 0)),
        out_shape=jax.ShapeDtypeStruct(x.shape, F32),
        compiler_params=_params(("arbitrary", "arbitrary"), 36),
        name=f"projres_l{layer}",
    )(a, x, mods, w, ln_g, ln_b)


def _attn_kernel(q_ref, k_ref, v_ref, lam_ref, sg_ref, o_ref, vx_ref, *, lam_init, tq):
    seq = q_ref.shape[0]
    hd = ATTN_HEAD_DIM
    lam = lam_ref[...]
    lam_full = (jnp.exp(jnp.sum(lam[0:1] * lam[1:2], axis=-1, keepdims=True))
                - jnp.exp(jnp.sum(lam[2:3] * lam[3:4], axis=-1, keepdims=True)) + lam_init)
    lane = lax.broadcasted_iota(jnp.int32, (tq, 2 * hd), 1)
    rq = lax.broadcasted_iota(jnp.int32, (tq, tq), 0) // CHUNK
    ck = lax.broadcasted_iota(jnp.int32, (tq, tq), 1) // CHUNK
    allowed = ck <= rq
    nt = (((1,), (1,)), ((), ()))

    hw = 2 * hd
    vx_ref[:, :hw] = v_ref[...]
    vx_ref[:, hw:] = jnp.ones((seq, hw), BF16)

    def score(i, m):
        kend = (i + 1) * tq
        q = q_ref[i * tq:kend, :].astype(F32) * (hd ** -0.5)
        qm = jnp.where((lane < hd) if m == 0 else (lane >= hd), q, 0.0).astype(BF16)
        s = lax.dot_general(qm, k_ref[0:kend, :], nt, preferred_element_type=F32)
        diag = jnp.where(allowed, s[:, kend - tq:], -jnp.inf)
        return diag if kend == tq else jnp.concatenate([s[:, :kend - tq], diag], axis=1)

    def attend(i, s):
        kend = (i + 1) * tq
        p = jnp.exp(s - jnp.max(s, axis=-1, keepdims=True))
        r = jnp.dot(p.astype(BF16), vx_ref[0:kend, :], preferred_element_type=F32)
        return r[:, :hw] / r[:, hw:]

    def finish(i, o1, o2):
        o = o1 - lam_full * o2
        o = o * lax.rsqrt(jnp.mean(o * o, axis=-1, keepdims=True) + LN_EPS)
        o_ref[i * tq:(i + 1) * tq, :] = (o * sg_ref[...] * (1.0 - lam_init)).astype(o_ref.dtype)

    n = seq // tq
    units = [(i, m) for i in reversed(range(n)) for m in range(2)]
    pend = score(*units[0])
    outs = {}
    for u, (i, m) in enumerate(units):
        cur = pend
        if u + 1 < len(units):
            pend = score(*units[u + 1])
        outs[(i, m)] = attend(i, cur)
        if m == 1:
            finish(i, outs.pop((i, 0)), outs.pop((i, 1)))


def _attention(qkv, lam, subln_g, *, lam_init, tq=256):
    bsz, seq, _ = qkv.shape
    hw = 2 * ATTN_HEAD_DIM
    return pl.pallas_call(
        functools.partial(_attn_kernel, lam_init=lam_init, tq=tq),
        grid=(bsz, ATTN_HEADS),
        in_specs=[
            pl.BlockSpec((None, seq, hw), lambda b, h: (b, 0, h)),
            pl.BlockSpec((None, seq, hw), lambda b, h: (b, 0, ATTN_HEADS + h)),
            pl.BlockSpec((None, seq, hw), lambda b, h: (b, 0, 2 * ATTN_HEADS + h)),
            _resident((4, ATTN_HEAD_DIM), lambda b, h: (0, 0)),
            _resident((1, hw), lambda b, h: (0, 0)),
        ],
        out_specs=pl.BlockSpec((None, seq, hw), lambda b, h: (b, 0, h)),
        out_shape=jax.ShapeDtypeStruct((bsz, seq, D_MODEL), BF16),
        scratch_shapes=[pltpu.VMEM((seq, 2 * hw), BF16)],
        compiler_params=_params(("arbitrary", "arbitrary"), 40),
        name="diff_attention",
    )(qkv, qkv, qkv, lam, subln_g.reshape(1, hw))


ROWS = 2 * CHUNK
PAIRS = SSM_GROUP // 2


def _lane_halves(e, o, lane):
    lo = jnp.where(lane < CHUNK, e, pltpu.roll(o, CHUNK, 1))
    hi = jnp.where(lane < CHUNK, pltpu.roll(e, CHUNK, 1), o)
    return lo, hi


def _s5in_kernel(x_ref, mod_ref, w_ref, o_ref, a_ref, s_ref):
    bsz = x_ref.shape[0]
    shift = mod_ref[3, :, :][:, None, :]
    scale = mod_ref[4, :, :][:, None, :]
    h = (x_ref[...] * (1.0 + scale) + shift).astype(BF16).reshape(bsz * ROWS, D_MODEL)
    a_ref[...] = lax.dot_general(w_ref[...], h, (((1,), (1,)), ((), ())), preferred_element_type=F32)
    lane = lax.broadcasted_iota(jnp.int32, (SSM_GROUPS, LANES), 1)
    half = SSM_GROUPS * bsz
    for q in range(PAIRS):
        for b in range(bsz):
            x0 = a_ref[(2 * q) * SSM_GROUPS:(2 * q + 1) * SSM_GROUPS, b * LANES:(b + 1) * LANES]
            x1 = a_ref[(2 * q + 1) * SSM_GROUPS:(2 * q + 2) * SSM_GROUPS, b * LANES:(b + 1) * LANES]
            c0, c1 = _lane_halves(x0, x1, lane)
            s_ref[q, pl.ds(b, SSM_GROUPS, stride=bsz), :] = c0
            s_ref[q, pl.ds(half + b, SSM_GROUPS, stride=bsz), :] = c1
    for c2 in range(2):
        for q in range(PAIRS):
            o_ref[c2, :, :, q * LANES:(q + 1) * LANES] = (
                s_ref[q, c2 * half:(c2 + 1) * half, :].reshape(SSM_GROUPS, bsz, LANES))


def _s5in(x, mods, w_t, *, layer):
    bsz, seq, _ = x.shape
    width = SSM_GROUP * CHUNK
    return pl.pallas_call(
        _s5in_kernel,
        grid=(seq // ROWS,),
        in_specs=[
            pl.BlockSpec((bsz, ROWS, D_MODEL), lambda j: (0, j, 0)),
            _resident((None, 9, bsz, D_MODEL), lambda j: (layer, 0, 0, 0)),
            _resident((D_MODEL, D_MODEL), lambda j: (0, 0)),
        ],
        out_specs=pl.BlockSpec((2, SSM_GROUPS, bsz, width), lambda j: (j, 0, 0, 0)),
        out_shape=jax.ShapeDtypeStruct((seq // CHUNK, SSM_GROUPS, bsz, width), F32),
        scratch_shapes=[
            pltpu.VMEM((D_MODEL, bsz * ROWS), F32),
            pltpu.VMEM((PAIRS, 2 * SSM_GROUPS * bsz, LANES), F32),
        ],
        compiler_params=_params(("arbitrary",), 40),
        name="s5_in",
    )(x, mods, w_t)


def _s5scan_kernel(u_ref, arr_ref, air_ref, arc_ref, aic_ref, ldt_ref, btr_ref, bti_ref,
                   ctr_ref, cti_ref, d_ref, y_ref, t_ref, ws_ref, wc_ref):
    n_chunks, bsz, width = u_ref.shape
    st = SSM_STATE
    dt = jnp.exp(ldt_ref[...])

    ar, ai = arr_ref[...], air_ref[...]
    mag = jnp.exp(ar * dt)
    abr, abi = mag * jnp.cos(ai * dt), mag * jnp.sin(ai * dt)
    den = ar * ar + ai * ai
    pr, pim = abr - 1.0, abi
    cfr, cfi = (pr * ar + pim * ai) / den, (pim * ar - pr * ai) / den
    btr, bti = btr_ref[...], bti_ref[...]
    bbr, bbi = cfr * btr - cfi * bti, cfr * bti + cfi * btr

    arc, aic = arc_ref[...], aic_ref[...]
    lane = lax.broadcasted_iota(jnp.int32, (1, LANES), 1)
    lag = (lane % CHUNK).astype(F32)
    first = lane < CHUNK
    ctr, cti = ctr_ref[...], cti_ref[...]

    def c_times_power(shift):
        e = lag + shift
        m = jnp.exp(arc * dt * e)
        er, ei = m * jnp.cos(aic * dt * e), m * jnp.sin(aic * dt * e)
        xr, xi = [], []
        for q in range(PAIRS):
            cr = jnp.where(first, ctr[:, 2 * q:2 * q + 1], ctr[:, 2 * q + 1:2 * q + 2])
            ci = jnp.where(first, cti[:, 2 * q:2 * q + 1], cti[:, 2 * q + 1:2 * q + 2])
            xr.append(cr * er - ci * ei)
            xi.append(cr * ei + ci * er)
        return jnp.concatenate(xr, axis=1), jnp.concatenate(xi, axis=1)

    xr, xi = c_times_power(0.0)
    hi = lax.Precision.HIGHEST
    kflat = (jnp.dot(bbr, xr, precision=hi, preferred_element_type=F32)
             - jnp.dot(bbi, xi, precision=hi, preferred_element_type=F32))
    prow = lax.broadcasted_iota(jnp.int32, kflat.shape, 0)
    plane = lax.broadcasted_iota(jnp.int32, kflat.shape, 1)
    kflat = kflat + jnp.where(plane == prow * CHUNK, d_ref[...], 0.0)

    srow = lax.broadcasted_iota(jnp.int32, (CHUNK, LANES), 0)
    keep = (lax.broadcasted_iota(jnp.int32, (CHUNK, LANES), 1) % CHUNK) >= srow
    for p in range(SSM_GROUP):
        rows = jnp.broadcast_to(kflat[p:p + 1, :], (CHUNK, width))
        for q in range(PAIRS):
            blk = pltpu.roll(rows[:, q * LANES:(q + 1) * LANES], 0, 1, stride=1, stride_axis=0)
            t_ref[p * CHUNK:(p + 1) * CHUNK, q * LANES:(q + 1) * LANES] = (
                jnp.where(keep, blk, 0.0).astype(BF16))

    e = (CHUNK - 1 - lax.broadcasted_iota(jnp.int32, (CHUNK, 1), 0)).astype(F32)
    m = jnp.exp(ar * dt * e)
    er, ei = m * jnp.cos(ai * dt * e), m * jnp.sin(ai * dt * e)
    zpad = jnp.zeros((CHUNK, LANES - st), F32)
    for p in range(SSM_GROUP):
        br, bi = bbr[p:p + 1, :], bbi[p:p + 1, :]
        ws_ref[p * CHUNK:(p + 1) * CHUNK, :] = jnp.concatenate(
            [er * br - ei * bi, zpad, er * bi + ei * br, zpad], axis=1).astype(BF16)

    pr_, pi_ = c_times_power(1.0)
    zrows = jnp.zeros((LANES - st, width), F32)
    wc_ref[...] = jnp.concatenate([pr_, zrows, -pi_, zrows], axis=0).astype(BF16)

    u = u_ref[...].reshape(n_chunks * bsz, width).astype(BF16)
    s = jnp.dot(u, ws_ref[...], preferred_element_type=F32)
    m64 = jnp.exp(ar * dt * CHUNK)
    zlane = jnp.zeros((1, LANES - st), F32)
    a64r = jnp.concatenate([m64 * jnp.cos(ai * dt * CHUNK), zlane], axis=1)
    a64i = jnp.concatenate([m64 * jnp.sin(ai * dt * CHUNK), zlane], axis=1)
    hr = jnp.zeros((bsz, LANES), F32)
    hi_ = jnp.zeros((bsz, LANES), F32)
    prev = []
    for c in range(n_chunks):
        prev.append(jnp.concatenate([hr, hi_], axis=1))
        sr, si = s[c * bsz:(c + 1) * bsz, :LANES], s[c * bsz:(c + 1) * bsz, LANES:]
        hr, hi_ = a64r * hr - a64i * hi_ + sr, a64r * hi_ + a64i * hr + si
    hprev = jnp.concatenate(prev, axis=0).astype(BF16)

    y = (jnp.dot(u, t_ref[...], preferred_element_type=F32)
         + jnp.dot(hprev, wc_ref[...], preferred_element_type=F32))
    y_ref[...] = y.reshape(n_chunks, bsz, width)


def _s5scan(u4, a_re, a_im, log_dt, b_re, b_im, c_re, c_im, d):
    n_chunks, groups, bsz, width = u4.shape
    st, pg = SSM_STATE, SSM_GROUP
    per_g = lambda *shape: pl.BlockSpec((None,) + shape, lambda g: (g,) + (0,) * len(shape))
    return pl.pallas_call(
        _s5scan_kernel,
        grid=(groups,),
        in_specs=[
            pl.BlockSpec((n_chunks, None, bsz, width), lambda g: (0, g, 0, 0)),
            per_g(1, st), per_g(1, st), per_g(st, 1), per_g(st, 1), per_g(1, 1),
            per_g(pg, st), per_g(pg, st), per_g(st, pg), per_g(st, pg), per_g(pg, 1),
        ],
        out_specs=pl.BlockSpec((n_chunks, None, bsz, width), lambda g: (0, g, 0, 0)),
        out_shape=jax.ShapeDtypeStruct(u4.shape, F32),
        scratch_shapes=[
            pltpu.VMEM((width, width), BF16),
            pltpu.VMEM((width, 2 * LANES), BF16),
            pltpu.VMEM((2 * LANES, width), BF16),
        ],
        compiler_params=_params(("arbitrary",), 32),
        name="s5_scan",
    )(u4,
      a_re.reshape(groups, 1, st), a_im.reshape(groups, 1, st),
      a_re.reshape(groups, st, 1), a_im.reshape(groups, st, 1),
      log_dt.reshape(groups, 1, 1),
      jnp.swapaxes(b_re, 1, 2), jnp.swapaxes(b_im, 1, 2),
      jnp.swapaxes(c_re, 1, 2), jnp.swapaxes(c_im, 1, 2),
      d.reshape(groups, pg, 1))


def _s5out_kernel(y_ref, x_ref, mod_ref, wg_ref, wo_ref, g_ref, b_ref, o_ref, a_ref, s_ref):
    bsz = x_ref.shape[0]
    half = SSM_GROUPS * bsz
    for c2 in range(2):
        for q in range(PAIRS):
            s_ref[q, c2 * half:(c2 + 1) * half, :] = (
                y_ref[c2, :, :, q * LANES:(q + 1) * LANES].reshape(half, LANES))
    lane = lax.broadcasted_iota(jnp.int32, (SSM_GROUPS, LANES), 1)
    for q in range(PAIRS):
        for b in range(bsz):
            c0 = s_ref[q, pl.ds(b, SSM_GROUPS, stride=bsz), :]
            c1 = s_ref[q, pl.ds(half + b, SSM_GROUPS, stride=bsz), :]
            x0, x1 = _lane_halves(c0, c1, lane)
            a_ref[(2 * q) * SSM_GROUPS:(2 * q + 1) * SSM_GROUPS, b * LANES:(b + 1) * LANES] = x0
            a_ref[(2 * q + 1) * SSM_GROUPS:(2 * q + 2) * SSM_GROUPS, b * LANES:(b + 1) * LANES] = x1
    z = jax.nn.gelu(a_ref[...], approximate=True)
    gt = jnp.dot(wg_ref[...], z.astype(BF16), preferred_element_type=F32)
    zz = (z * jax.nn.sigmoid(gt)).astype(BF16)
    y = lax.dot_general(zz, wo_ref[...], (((0,), (0,)), ((), ())), preferred_element_type=F32)
    y = y.reshape(bsz, ROWS, D_MODEL)
    gate = 1.0 + mod_ref[5, :, :][:, None, :]
    o_ref[...] = _layer_norm(ALPHA * x_ref[...] + gate * y, g_ref[...], b_ref[...])


def _s5out(y4, x, mods, wg_t, wo, ln_g, ln_b, *, layer):
    bsz, seq, _ = x.shape
    width = SSM_GROUP * CHUNK
    return pl.pallas_call(
        _s5out_kernel,
        grid=(seq // ROWS,),
        in_specs=[
            pl.BlockSpec((2, SSM_GROUPS, bsz, width), lambda j: (j, 0, 0, 0)),
            pl.BlockSpec((bsz, ROWS, D_MODEL), lambda j: (0, j, 0)),
            _resident((None, 9, bsz, D_MODEL), lambda j: (layer, 0, 0, 0)),
            _resident((D_MODEL, D_MODEL), lambda j: (0, 0)),
            _resident((D_MODEL, D_MODEL), lambda j: (0, 0)),
            _resident((None, None, 1, D_MODEL), lambda j: (layer, 1, 0, 0)),
            _resident((None, None, 1, D_MODEL), lambda j: (layer, 1, 0, 0)),
        ],
        out_specs=pl.BlockSpec((bsz, ROWS, D_MODEL), lambda j: (0, j, 0)),
        out_shape=jax.ShapeDtypeStruct(x.shape, F32),
        scratch_shapes=[
            pltpu.VMEM((D_MODEL, bsz * ROWS), F32),
            pltpu.VMEM((PAIRS, 2 * SSM_GROUPS * bsz, LANES), F32),
        ],
        compiler_params=_params(("arbitrary",), 48),
        name="s5_out",
    )(y4, x, mods, wg_t, wo, ln_g, ln_b)


def kernel(x, c, ada_w, ada_b, ln_g, ln_b, ffn_w1, ffn_w3, ffn_w2, attn_w_in, attn_lam, attn_subln_g, attn_w_out, ssm_w_in, ssm_a_re, ssm_a_im, ssm_log_dt, ssm_b_re, ssm_b_im, ssm_c_re, ssm_c_im, ssm_d, ssm_w_gate, ssm_w_out):
    mods = _ada(c, ada_w, ada_b)
    lng = ln_g.reshape(DEPTH, 3, 1, D_MODEL)
    lnb = ln_b.reshape(DEPTH, 3, 1, D_MODEL)
    w1, w3, w2 = ffn_w1, ffn_w3, ffn_w2
    gp = (SSM_GROUPS, SSM_GROUP)

    for layer in range(DEPTH):
        i = layer // 2
        x = _ffn(x, mods, w1, w3, w2, lng, lnb, layer=layer, half=0, sub=0)
        if layer % 2 == 0:
            lam_init = 0.8 - 0.6 * math.exp(-0.3 * layer)
            qkv = _modproj(x, mods, attn_w_in[i].astype(BF16), layer=layer, sub=1)
            o = _attention(qkv, attn_lam[i], attn_subln_g[i], lam_init=lam_init)
            x = _projres(o, x, mods, attn_w_out[i].astype(BF16), lng, lnb, layer=layer, sub=1)
        else:
            rows_pg = lambda w: w.reshape(*gp, D_MODEL).transpose(1, 0, 2).reshape(D_MODEL, D_MODEL)
            w_in_t = rows_pg(ssm_w_in[i].astype(BF16).T)
            wg_t = rows_pg(rows_pg(ssm_w_gate[i].astype(BF16)).T)
            wo = rows_pg(ssm_w_out[i].astype(BF16))
            u4 = _s5in(x, mods, w_in_t, layer=layer)
            y4 = _s5scan(u4, ssm_a_re[i], ssm_a_im[i], ssm_log_dt[i], ssm_b_re[i], ssm_b_im[i],
                         ssm_c_re[i], ssm_c_im[i], ssm_d[i])
            x = _s5out(y4, x, mods, wg_t, wo, lng, lnb, layer=layer)
        x = _ffn(x, mods, w1, w3, w2, lng, lnb, layer=layer, half=1, sub=2)
    return x
```

```python
import functools
import math

import jax
import jax.numpy as jnp
from jax import lax
from jax.experimental import pallas as pl
from jax.experimental.pallas import tpu as pltpu

D_MODEL = 1024
DEPTH = 2
CHUNK = 64
ATTN_HEADS = 8
ATTN_HEAD_DIM = 64
SSM_GROUP = 16
SSM_GROUPS = D_MODEL // SSM_GROUP
SSM_STATE = 64
D_FF = 2816
ALPHA = (2 * DEPTH) ** 0.25
LN_EPS = 1e-5

LANES = 128
F_CHUNK = 256
MIB = 1024 * 1024

F32 = jnp.float32
BF16 = jnp.bfloat16


def _params(sem, vmem_mib):
    return pltpu.CompilerParams(dimension_semantics=sem, vmem_limit_bytes=vmem_mib * MIB)


def _resident(block_shape, index_map):
    return pl.BlockSpec(block_shape, index_map, pipeline_mode=pl.Buffered(1))


def _layer_norm(r, g, b):
    mu = jnp.mean(r, axis=-1, keepdims=True)
    d = r - mu
    var = jnp.mean(d * d, axis=-1, keepdims=True)
    return d * lax.rsqrt(var + LN_EPS) * g + b


def _mods(mod_ref, sub, bidx):
    shift = mod_ref[3 * sub + 0, pl.ds(bidx, 1), :]
    scale = mod_ref[3 * sub + 1, pl.ds(bidx, 1), :]
    gate = 1.0 + mod_ref[3 * sub + 2, pl.ds(bidx, 1), :]
    return shift, scale, gate


def _ada_kernel(c_ref, w_ref, b_ref, o_ref):
    c = c_ref[...]
    cond = (c * jax.nn.sigmoid(c)).astype(BF16)
    o_ref[...] = jnp.dot(cond, w_ref[...].astype(BF16), preferred_element_type=F32) + b_ref[...]


def _ada(c, ada_w, ada_b):
    bsz = c.shape[0]
    n_blk = ada_w.shape[2] // D_MODEL
    return pl.pallas_call(
        _ada_kernel,
        grid=(DEPTH, n_blk),
        in_specs=[
            pl.BlockSpec((bsz, D_MODEL), lambda l, n: (0, 0)),
            pl.BlockSpec((None, D_MODEL, D_MODEL), lambda l, n: (l, 0, n)),
            pl.BlockSpec((None, None, 1, D_MODEL), lambda l, n: (l, n, 0, 0)),
        ],
        out_specs=pl.BlockSpec((None, None, bsz, D_MODEL), lambda l, n: (l, n, 0, 0)),
        out_shape=jax.ShapeDtypeStruct((DEPTH, n_blk, bsz, D_MODEL), F32),
        compiler_params=_params(("arbitrary", "arbitrary"), 24),
        name="ada_mods",
    )(c, ada_w, ada_b.reshape(DEPTH, n_blk, 1, D_MODEL))


W_ROWS = 256


def _ffn_kernel(x_ref, mod_ref, w1_hbm, w3_hbm, w2_hbm, g_ref, b_ref, o_ref,
                w1_ref, w3_ref, w2_ref, wide_ref, narrow_ref, sem, *, sub, ts, layer, half):
    shift, scale, gate = _mods(mod_ref, sub, pl.program_id(0))
    n_f = D_FF // F_CHUNK
    chunk = lambda f: slice(f * F_CHUNK, (f + 1) * F_CHUNK)

    @pl.when(jnp.logical_and(pl.program_id(0) == 0, pl.program_id(1) == 0))
    def _():
        chunks = [(src, dst, stage, kind, slice(r, r + W_ROWS))
                  for src, dst, stage, kind in ((w1_hbm, w1_ref, wide_ref, 0), (w3_hbm, w3_ref, wide_ref, 0),
                                                (w2_hbm, w2_ref, narrow_ref, 1))
                  for r in range(0, dst.shape[0], W_ROWS)]

        def copy(i):
            src, _, stage, kind, rows = chunks[i]
            return pltpu.make_async_copy(src.at[layer, half, rows, :], stage.at[i % 2], sem.at[kind, i % 2])

        copy(0).start()
        for i, (_, dst, stage, _, rows) in enumerate(chunks):
            if i + 1 < len(chunks):
                copy(i + 1).start()
            copy(i).wait()
            dst[rows, :] = stage[i % 2].astype(BF16)

    def modulated(rows):
        return (x_ref[rows, :] * (1.0 + scale) + shift).astype(BF16)

    def up(h, f):
        return (jnp.dot(h, w1_ref[:, chunk(f)], preferred_element_type=F32),
                jnp.dot(h, w3_ref[:, chunk(f)], preferred_element_type=F32))

    def down(ab, acc, f):
        a, b = ab
        u = (a * jax.nn.sigmoid(a) * b).astype(BF16)
        d = jnp.dot(u, w2_ref[chunk(f), :], preferred_element_type=F32)
        return d if acc is None else acc + d

    def epilogue(rows, acc):
        r = ALPHA * x_ref[rows, :] + gate * (0.5 * acc)
        o_ref[rows, :] = _layer_norm(r, g_ref[...], b_ref[...])

    n_sub = x_ref.shape[0] // ts
    tile = lambda t: slice(t * ts, (t + 1) * ts)
    h = modulated(tile(0))
    nxt = up(h, 0)
    for t in range(n_sub):
        acc = None
        for f in range(n_f):
            cur = nxt
            if f + 1 < n_f:
                nxt = up(h, f + 1)
            elif t + 1 < n_sub:
                nxt = up(h_next, 0)
            if f == n_f // 2 and t + 1 < n_sub:
                h_next = modulated(tile(t + 1))
            acc = down(cur, acc, f)
        epilogue(tile(t), acc)
        if t + 1 < n_sub:
            h = h_next


def _ffn(x, mods, w1, w3, w2, ln_g, ln_b, *, layer, half, sub, tm=1024, ts=256):
    bsz, seq, _ = x.shape
    return pl.pallas_call(
        functools.partial(_ffn_kernel, sub=sub, ts=ts, layer=layer, half=half),
        grid=(bsz, seq // tm),
        in_specs=[
            pl.BlockSpec((None, tm, D_MODEL), lambda b, i: (b, i, 0)),
            _resident((None, 9, bsz, D_MODEL), lambda b, i: (layer, 0, 0, 0)),
            pl.BlockSpec(memory_space=pl.ANY),
            pl.BlockSpec(memory_space=pl.ANY),
            pl.BlockSpec(memory_space=pl.ANY),
            _resident((None, None, 1, D_MODEL), lambda b, i: (layer, sub, 0, 0)),
            _resident((None, None, 1, D_MODEL), lambda b, i: (layer, sub, 0, 0)),
        ],
        out_specs=pl.BlockSpec((None, tm, D_MODEL), lambda b, i: (b, i, 0)),
        out_shape=jax.ShapeDtypeStruct(x.shape, F32),
        scratch_shapes=[
            pltpu.VMEM((D_MODEL, D_FF), BF16), pltpu.VMEM((D_MODEL, D_FF), BF16),
            pltpu.VMEM((D_FF, D_MODEL), BF16),
            pltpu.VMEM((2, W_ROWS, D_FF), F32), pltpu.VMEM((2, W_ROWS, D_MODEL), F32),
            pltpu.SemaphoreType.DMA((2, 2)),
        ],
        compiler_params=_params(("arbitrary", "arbitrary"), 52),
        name=f"ffn_l{layer}_h{half}",
    )(x, mods, w1, w3, w2, ln_g, ln_b)


def _modproj_kernel(x_ref, mod_ref, w_ref, o_ref, *, sub):
    shift, scale, _ = _mods(mod_ref, sub, pl.program_id(0))
    h = (x_ref[...] * (1.0 + scale) + shift).astype(BF16)
    for n in range(o_ref.shape[1] // D_MODEL):
        cols = slice(n * D_MODEL, (n + 1) * D_MODEL)
        o_ref[:, cols] = jnp.dot(h, w_ref[:, cols], preferred_element_type=F32).astype(o_ref.dtype)


def _modproj(x, mods, w, *, layer, sub, tm=1024):
    bsz, seq, _ = x.shape
    n_out = w.shape[1]
    return pl.pallas_call(
        functools.partial(_modproj_kernel, sub=sub),
        grid=(bsz, seq // tm),
        in_specs=[
            pl.BlockSpec((None, tm, D_MODEL), lambda b, i: (b, i, 0)),
            _resident((None, 9, bsz, D_MODEL), lambda b, i: (layer, 0, 0, 0)),
            _resident((D_MODEL, n_out), lambda b, i: (0, 0)),
        ],
        out_specs=pl.BlockSpec((None, tm, n_out), lambda b, i: (b, i, 0)),
        out_shape=jax.ShapeDtypeStruct((bsz, seq, n_out), BF16),
        compiler_params=_params(("arbitrary", "arbitrary"), 44),
        name=f"modproj_l{layer}",
    )(x, mods, w)


def _projres_kernel(a_ref, x_ref, mod_ref, w_ref, g_ref, b_ref, o_ref, *, sub, ts):
    _, _, gate = _mods(mod_ref, sub, pl.program_id(0))
    n_sub = x_ref.shape[0] // ts
    proj = lambda t: jnp.dot(a_ref[t * ts:(t + 1) * ts, :], w_ref[...], preferred_element_type=F32)
    nxt = proj(0)
    for t in range(n_sub):
        y, rows = nxt, slice(t * ts, (t + 1) * ts)
        if t + 1 < n_sub:
            nxt = proj(t + 1)
        o_ref[rows, :] = _layer_norm(ALPHA * x_ref[rows, :] + gate * y, g_ref[...], b_ref[...])


def _projres(a, x, mods, w, ln_g, ln_b, *, layer, sub, tm=1024, ts=256):
    bsz, seq, _ = x.shape
    return pl.pallas_call(
        functools.partial(_projres_kernel, sub=sub, ts=ts),
        grid=(bsz, seq // tm),
        in_specs=[
            pl.BlockSpec((None, tm, D_MODEL), lambda b, i: (b, i, 0)),
            pl.BlockSpec((None, tm, D_MODEL), lambda b, i: (b, i, 0)),
            _resident((None, 9, bsz, D_MODEL), lambda b, i: (layer, 0, 0, 0)),
            _resident((D_MODEL, D_MODEL), lambda b, i: (0, 0)),
            _resident((None, None, 1, D_MODEL), lambda b, i: (layer, sub, 0, 0)),
            _resident((None, None, 1, D_MODEL), lambda b, i: (layer, sub, 0, 0)),
        ],
        out_specs=pl.BlockSpec((None, tm, D_MODEL), lambda b, i: (b, i, 0)),
        out_shape=jax.ShapeDtypeStruct(x.shape, F32),
        compiler_params=_params(("arbitrary", "arbitrary"), 36),
        name=f"projres_l{layer}",
    )(a, x, mods, w, ln_g, ln_b)


def _attn_kernel(q_ref, k_ref, v_ref, lam_ref, sg_ref, o_ref, vx_ref, *, lam_init, tq):
    seq = q_ref.shape[0]
    hd = ATTN_HEAD_DIM
    lam = lam_ref[...]
    lam_full = (jnp.exp(jnp.sum(lam[0:1] * lam[1:2], axis=-1, keepdims=True))
                - jnp.exp(jnp.sum(lam[2:3] * lam[3:4], axis=-1, keepdims=True)) + lam_init)
    lane = lax.broadcasted_iota(jnp.int32, (tq, 2 * hd), 1)
    rq = lax.broadcasted_iota(jnp.int32, (tq, tq), 0) // CHUNK
    ck = lax.broadcasted_iota(jnp.int32, (tq, tq), 1) // CHUNK
    allowed = ck <= rq
    nt = (((1,), (1,)), ((), ()))

    hw = 2 * hd
    vx_ref[:, :hw] = v_ref[...]
    vx_ref[:, hw:] = jnp.ones((seq, hw), BF16)

    def score(i, m):
        kend = (i + 1) * tq
        q = q_ref[i * tq:kend, :].astype(F32) * (hd ** -0.5)
        qm = jnp.where((lane < hd) if m == 0 else (lane >= hd), q, 0.0).astype(BF16)
        s = lax.dot_general(qm, k_ref[0:kend, :], nt, preferred_element_type=F32)
        diag = jnp.where(allowed, s[:, kend - tq:], -jnp.inf)
        return diag if kend == tq else jnp.concatenate([s[:, :kend - tq], diag], axis=1)

    def attend(i, s):
        kend = (i + 1) * tq
        p = jnp.exp(s - jnp.max(s, axis=-1, keepdims=True))
        r = jnp.dot(p.astype(BF16), vx_ref[0:kend, :], preferred_element_type=F32)
        return r[:, :hw] / r[:, hw:]

    def finish(i, o1, o2):
        o = o1 - lam_full * o2
        o = o * lax.rsqrt(jnp.mean(o * o, axis=-1, keepdims=True) + LN_EPS)
        o_ref[i * tq:(i + 1) * tq, :] = (o * sg_ref[...] * (1.0 - lam_init)).astype(o_ref.dtype)

    n = seq // tq
    units = [(i, m) for i in reversed(range(n)) for m in range(2)]
    pend = score(*units[0])
    outs = {}
    for u, (i, m) in enumerate(units):
        cur = pend
        if u + 1 < len(units):
            pend = score(*units[u + 1])
        outs[(i, m)] = attend(i, cur)
        if m == 1:
            finish(i, outs.pop((i, 0)), outs.pop((i, 1)))


def _attention(qkv, lam, subln_g, *, lam_init, tq=256):
    bsz, seq, _ = qkv.shape
    hw = 2 * ATTN_HEAD_DIM
    return pl.pallas_call(
        functools.partial(_attn_kernel, lam_init=lam_init, tq=tq),
        grid=(bsz, ATTN_HEADS),
        in_specs=[
            pl.BlockSpec((None, seq, hw), lambda b, h: (b, 0, h)),
            pl.BlockSpec((None, seq, hw), lambda b, h: (b, 0, ATTN_HEADS + h)),
            pl.BlockSpec((None, seq, hw), lambda b, h: (b, 0, 2 * ATTN_HEADS + h)),
            _resident((4, ATTN_HEAD_DIM), lambda b, h: (0, 0)),
            _resident((1, hw), lambda b, h: (0, 0)),
        ],
        out_specs=pl.BlockSpec((None, seq, hw), lambda b, h: (b, 0, h)),
        out_shape=jax.ShapeDtypeStruct((bsz, seq, D_MODEL), BF16),
        scratch_shapes=[pltpu.VMEM((seq, 2 * hw), BF16)],
        compiler_params=_params(("arbitrary", "arbitrary"), 40),
        name="diff_attention",
    )(qkv, qkv, qkv, lam, subln_g.reshape(1, hw))


ROWS = 2 * CHUNK
PAIRS = SSM_GROUP // 2


def _lane_halves(e, o, lane):
    lo = jnp.where(lane < CHUNK, e, pltpu.roll(o, CHUNK, 1))
    hi = jnp.where(lane < CHUNK, pltpu.roll(e, CHUNK, 1), o)
    return lo, hi


def _s5in_kernel(x_ref, mod_ref, w_ref, o_ref, a_ref, s_ref):
    bsz = x_ref.shape[0]
    shift = mod_ref[3, :, :][:, None, :]
    scale = mod_ref[4, :, :][:, None, :]
    h = (x_ref[...] * (1.0 + scale) + shift).astype(BF16).reshape(bsz * ROWS, D_MODEL)
    a_ref[...] = lax.dot_general(w_ref[...], h, (((1,), (1,)), ((), ())), preferred_element_type=F32)
    lane = lax.broadcasted_iota(jnp.int32, (SSM_GROUPS, LANES), 1)
    half = SSM_GROUPS * bsz
    for q in range(PAIRS):
        for b in range(bsz):
            x0 = a_ref[(2 * q) * SSM_GROUPS:(2 * q + 1) * SSM_GROUPS, b * LANES:(b + 1) * LANES]
            x1 = a_ref[(2 * q + 1) * SSM_GROUPS:(2 * q + 2) * SSM_GROUPS, b * LANES:(b + 1) * LANES]
            c0, c1 = _lane_halves(x0, x1, lane)
            s_ref[q, pl.ds(b, SSM_GROUPS, stride=bsz), :] = c0
            s_ref[q, pl.ds(half + b, SSM_GROUPS, stride=bsz), :] = c1
    for c2 in range(2):
        for q in range(PAIRS):
            o_ref[c2, :, :, q * LANES:(q + 1) * LANES] = (
                s_ref[q, c2 * half:(c2 + 1) * half, :].reshape(SSM_GROUPS, bsz, LANES))


def _s5in(x, mods, w_t, *, layer):
    bsz, seq, _ = x.shape
    width = SSM_GROUP * CHUNK
    return pl.pallas_call(
        _s5in_kernel,
        grid=(seq // ROWS,),
        in_specs=[
            pl.BlockSpec((bsz, ROWS, D_MODEL), lambda j: (0, j, 0)),
            _resident((None, 9, bsz, D_MODEL), lambda j: (layer, 0, 0, 0)),
            _resident((D_MODEL, D_MODEL), lambda j: (0, 0)),
        ],
        out_specs=pl.BlockSpec((2, SSM_GROUPS, bsz, width), lambda j: (j, 0, 0, 0)),
        out_shape=jax.ShapeDtypeStruct((seq // CHUNK, SSM_GROUPS, bsz, width), F32),
        scratch_shapes=[
            pltpu.VMEM((D_MODEL, bsz * ROWS), F32),
            pltpu.VMEM((PAIRS, 2 * SSM_GROUPS * bsz, LANES), F32),
        ],
        compiler_params=_params(("arbitrary",), 40),
        name="s5_in",
    )(x, mods, w_t)


def _s5scan_kernel(u_ref, arr_ref, air_ref, arc_ref, aic_ref, ldt_ref, btr_ref, bti_ref,
                   ctr_ref, cti_ref, d_ref, y_ref, t_ref, ws_ref, wc_ref):
    n_chunks, bsz, width = u_ref.shape
    st = SSM_STATE
    dt = jnp.exp(ldt_ref[...])

    ar, ai = arr_ref[...], air_ref[...]
    mag = jnp.exp(ar * dt)
    abr, abi = mag * jnp.cos(ai * dt), mag * jnp.sin(ai * dt)
    den = ar * ar + ai * ai
    pr, pim = abr - 1.0, abi
    cfr, cfi = (pr * ar + pim * ai) / den, (pim * ar - pr * ai) / den
    btr, bti = btr_ref[...], bti_ref[...]
    bbr, bbi = cfr * btr - cfi * bti, cfr * bti + cfi * btr

    arc, aic = arc_ref[...], aic_ref[...]
    lane = lax.broadcasted_iota(jnp.int32, (1, LANES), 1)
    lag = (lane % CHUNK).astype(F32)
    first = lane < CHUNK
    ctr, cti = ctr_ref[...], cti_ref[...]

    def c_times_power(shift):
        e = lag + shift
        m = jnp.exp(arc * dt * e)
        er, ei = m * jnp.cos(aic * dt * e), m * jnp.sin(aic * dt * e)
        xr, xi = [], []
        for q in range(PAIRS):
            cr = jnp.where(first, ctr[:, 2 * q:2 * q + 1], ctr[:, 2 * q + 1:2 * q + 2])
            ci = jnp.where(first, cti[:, 2 * q:2 * q + 1], cti[:, 2 * q + 1:2 * q + 2])
            xr.append(cr * er - ci * ei)
            xi.append(cr * ei + ci * er)
        return jnp.concatenate(xr, axis=1), jnp.concatenate(xi, axis=1)

    xr, xi = c_times_power(0.0)
    hi = lax.Precision.HIGHEST
    kflat = (jnp.dot(bbr, xr, precision=hi, preferred_element_type=F32)
             - jnp.dot(bbi, xi, precision=hi, preferred_element_type=F32))
    prow = lax.broadcasted_iota(jnp.int32, kflat.shape, 0)
    plane = lax.broadcasted_iota(jnp.int32, kflat.shape, 1)
    kflat = kflat + jnp.where(plane == prow * CHUNK, d_ref[...], 0.0)

    srow = lax.broadcasted_iota(jnp.int32, (CHUNK, LANES), 0)
    keep = (lax.broadcasted_iota(jnp.int32, (CHUNK, LANES), 1) % CHUNK) >= srow
    for p in range(SSM_GROUP):
        rows = jnp.broadcast_to(kflat[p:p + 1, :], (CHUNK, width))
        for q in range(PAIRS):
            blk = pltpu.roll(rows[:, q * LANES:(q + 1) * LANES], 0, 1, stride=1, stride_axis=0)
            t_ref[p * CHUNK:(p + 1) * CHUNK, q * LANES:(q + 1) * LANES] = (
                jnp.where(keep, blk, 0.0).astype(BF16))

    e = (CHUNK - 1 - lax.broadcasted_iota(jnp.int32, (CHUNK, 1), 0)).astype(F32)
    m = jnp.exp(ar * dt * e)
    er, ei = m * jnp.cos(ai * dt * e), m * jnp.sin(ai * dt * e)
    zpad = jnp.zeros((CHUNK, LANES - st), F32)
    for p in range(SSM_GROUP):
        br, bi = bbr[p:p + 1, :], bbi[p:p + 1, :]
        ws_ref[p * CHUNK:(p + 1) * CHUNK, :] = jnp.concatenate(
            [er * br - ei * bi, zpad, er * bi + ei * br, zpad], axis=1).astype(BF16)

    pr_, pi_ = c_times_power(1.0)
    zrows = jnp.zeros((LANES - st, width), F32)
    wc_ref[...] = jnp.concatenate([pr_, zrows, -pi_, zrows], axis=0).astype(BF16)

    u = u_ref[...].reshape(n_chunks * bsz, width).astype(BF16)
    s = jnp.dot(u, ws_ref[...], preferred_element_type=F32)
    m64 = jnp.exp(ar * dt * CHUNK)
    zlane = jnp.zeros((1, LANES - st), F32)
    a64r = jnp.concatenate([m64 * jnp.cos(ai * dt * CHUNK), zlane], axis=1)
    a64i = jnp.concatenate([m64 * jnp.sin(ai * dt * CHUNK), zlane], axis=1)
    hr = jnp.zeros((bsz, LANES), F32)
    hi_ = jnp.zeros((bsz, LANES), F32)
    prev = []
    for c in range(n_chunks):
        prev.append(jnp.concatenate([hr, hi_], axis=1))
        sr, si = s[c * bsz:(c + 1) * bsz, :LANES], s[c * bsz:(c + 1) * bsz, LANES:]
        hr, hi_ = a64r * hr - a64i * hi_ + sr, a64r * hi_ + a64i * hr + si
    hprev = jnp.concatenate(prev, axis=0).astype(BF16)

    y = (jnp.dot(u, t_ref[...], preferred_element_type=F32)
         + jnp.dot(hprev, wc_ref[...], preferred_element_type=F32))
    y_ref[...] = y.reshape(n_chunks, bsz, width)


def _s5scan(u4, a_re, a_im, log_dt, b_re, b_im, c_re, c_im, d):
    n_chunks, groups, bsz, width = u4.shape
    st, pg = SSM_STATE, SSM_GROUP
    per_g = lambda *shape: pl.BlockSpec((None,) + shape, lambda g: (g,) + (0,) * len(shape))
    return pl.pallas_call(
        _s5scan_kernel,
        grid=(groups,),
        in_specs=[
            pl.BlockSpec((n_chunks, None, bsz, width), lambda g: (0, g, 0, 0)),
            per_g(1, st), per_g(1, st), per_g(st, 1), per_g(st, 1), per_g(1, 1),
            per_g(pg, st), per_g(pg, st), per_g(st, pg), per_g(st, pg), per_g(pg, 1),
        ],
        out_specs=pl.BlockSpec((n_chunks, None, bsz, width), lambda g: (0, g, 0, 0)),
        out_shape=jax.ShapeDtypeStruct(u4.shape, F32),
        scratch_shapes=[
            pltpu.VMEM((width, width), BF16),
            pltpu.VMEM((width, 2 * LANES), BF16),
            pltpu.VMEM((2 * LANES, width), BF16),
        ],
        compiler_params=_params(("arbitrary",), 32),
        name="s5_scan",
    )(u4,
      a_re.reshape(groups, 1, st), a_im.reshape(groups, 1, st),
      a_re.reshape(groups, st, 1), a_im.reshape(groups, st, 1),
      log_dt.reshape(groups, 1, 1),
      jnp.swapaxes(b_re, 1, 2), jnp.swapaxes(b_im, 1, 2),
      jnp.swapaxes(c_re, 1, 2), jnp.swapaxes(c_im, 1, 2),
      d.reshape(groups, pg, 1))


def _s5out_kernel(y_ref, x_ref, mod_ref, wg_ref, wo_ref, g_ref, b_ref, o_ref, a_ref, s_ref):
    bsz = x_ref.shape[0]
    half = SSM_GROUPS * bsz
    for c2 in range(2):
        for q in range(PAIRS):
            s_ref[q, c2 * half:(c2 + 1) * half, :] = (
                y_ref[c2, :, :, q * LANES:(q + 1) * LANES].reshape(half, LANES))
    lane = lax.broadcasted_iota(jnp.int32, (SSM_GROUPS, LANES), 1)
    for q in range(PAIRS):
        for b in range(bsz):
            c0 = s_ref[q, pl.ds(b, SSM_GROUPS, stride=bsz), :]
            c1 = s_ref[q, pl.ds(half + b, SSM_GROUPS, stride=bsz), :]
            x0, x1 = _lane_halves(c0, c1, lane)
            a_ref[(2 * q) * SSM_GROUPS:(2 * q + 1) * SSM_GROUPS, b * LANES:(b + 1) * LANES] = x0
            a_ref[(2 * q + 1) * SSM_GROUPS:(2 * q + 2) * SSM_GROUPS, b * LANES:(b + 1) * LANES] = x1
    z = jax.nn.gelu(a_ref[...], approximate=True)
    gt = jnp.dot(wg_ref[...], z.astype(BF16), preferred_element_type=F32)
    zz = (z * jax.nn.sigmoid(gt)).astype(BF16)
    y = lax.dot_general(zz, wo_ref[...], (((0,), (0,)), ((), ())), preferred_element_type=F32)
    y = y.reshape(bsz, ROWS, D_MODEL)
    gate = 1.0 + mod_ref[5, :, :][:, None, :]
    o_ref[...] = _layer_norm(ALPHA * x_ref[...] + gate * y, g_ref[...], b_ref[...])


def _s5out(y4, x, mods, wg_t, wo, ln_g, ln_b, *, layer):
    bsz, seq, _ = x.shape
    width = SSM_GROUP * CHUNK
    return pl.pallas_call(
        _s5out_kernel,
        grid=(seq // ROWS,),
        in_specs=[
            pl.BlockSpec((2, SSM_GROUPS, bsz, width), lambda j: (j, 0, 0, 0)),
            pl.BlockSpec((bsz, ROWS, D_MODEL), lambda j: (0, j, 0)),
            _resident((None, 9, bsz, D_MODEL), lambda j: (layer, 0, 0, 0)),
            _resident((D_MODEL, D_MODEL), lambda j: (0, 0)),
            _resident((D_MODEL, D_MODEL), lambda j: (0, 0)),
            _resident((None, None, 1, D_MODEL), lambda j: (layer, 1, 0, 0)),
            _resident((None, None, 1, D_MODEL), lambda j: (layer, 1, 0, 0)),
        ],
        out_specs=pl.BlockSpec((bsz, ROWS, D_MODEL), lambda j: (0, j, 0)),
        out_shape=jax.ShapeDtypeStruct(x.shape, F32),
        scratch_shapes=[
            pltpu.VMEM((D_MODEL, bsz * ROWS), F32),
            pltpu.VMEM((PAIRS, 2 * SSM_GROUPS * bsz, LANES), F32),
        ],
        compiler_params=_params(("arbitrary",), 48),
        name="s5_out",
    )(y4, x, mods, wg_t, wo, ln_g, ln_b)


def kernel(x, c, ada_w, ada_b, ln_g, ln_b, ffn_w1, ffn_w3, ffn_w2, attn_w_in, attn_lam, attn_subln_g, attn_w_out, ssm_w_in, ssm_a_re, ssm_a_im, ssm_log_dt, ssm_b_re, ssm_b_im, ssm_c_re, ssm_c_im, ssm_d, ssm_w_gate, ssm_w_out):
    mods = _ada(c, ada_w, ada_b)
    lng = ln_g.reshape(DEPTH, 3, 1, D_MODEL)
    lnb = ln_b.reshape(DEPTH, 3, 1, D_MODEL)
    w1, w3, w2 = ffn_w1, ffn_w3, ffn_w2
    gp = (SSM_GROUPS, SSM_GROUP)

    for layer in range(DEPTH):
        i = layer // 2
        x = _ffn(x, mods, w1, w3, w2, lng, lnb, layer=layer, half=0, sub=0)
        if layer % 2 == 0:
            lam_init = 0.8 - 0.6 * math.exp(-0.3 * layer)
            qkv = _modproj(x, mods, attn_w_in[i].astype(BF16), layer=layer, sub=1)
            o = _attention(qkv, attn_lam[i], attn_subln_g[i], lam_init=lam_init)
            x = _projres(o, x, mods, attn_w_out[i].astype(BF16), lng, lnb, layer=layer, sub=1)
        else:
            rows_pg = lambda w: w.reshape(*gp, D_MODEL).transpose(1, 0, 2).reshape(D_MODEL, D_MODEL)
            w_in_t = rows_pg(ssm_w_in[i].astype(BF16).T)
            wg_t = rows_pg(rows_pg(ssm_w_gate[i].astype(BF16)).T)
            wo = rows_pg(ssm_w_out[i].astype(BF16))
            u4 = _s5in(x, mods, w_in_t, layer=layer)
            y4 = _s5scan(u4, ssm_a_re[i], ssm_a_im[i], ssm_log_dt[i], ssm_b_re[i], ssm_b_im[i],
                         ssm_c_re[i], ssm_c_im[i], ssm_d[i])
            x = _s5out(y4, x, mods, wg_t, wo, lng, lnb, layer=layer)
        x = _ffn(x, mods, w1, w3, w2, lng, lnb, layer=layer, half=1, sub=2)
    return x
```

```python
import functools
import math

import jax
import jax.numpy as jnp
from jax import lax
from jax.experimental import pallas as pl
from jax.experimental.pallas import tpu as pltpu

D_MODEL = 1024
DEPTH = 2
CHUNK = 64
ATTN_HEADS = 8
ATTN_HEAD_DIM = 64
SSM_GROUP = 16
SSM_GROUPS = D_MODEL // SSM_GROUP
SSM_STATE = 64
D_FF = 2816
ALPHA = (2 * DEPTH) ** 0.25
LN_EPS = 1e-5

LANES = 128
F_CHUNK = 256
MIB = 1024 * 1024

F32 = jnp.float32
BF16 = jnp.bfloat16


def _params(sem, vmem_mib):
    return pltpu.CompilerParams(dimension_semantics=sem, vmem_limit_bytes=vmem_mib * MIB)


def _resident(block_shape, index_map):
    return pl.BlockSpec(block_shape, index_map, pipeline_mode=pl.Buffered(1))


def _layer_norm(r, g, b):
    mu = jnp.mean(r, axis=-1, keepdims=True)
    d = r - mu
    var = jnp.mean(d * d, axis=-1, keepdims=True)
    return d * lax.rsqrt(var + LN_EPS) * g + b


def _mods(mod_ref, sub, bidx):
    shift = mod_ref[3 * sub + 0, pl.ds(bidx, 1), :]
    scale = mod_ref[3 * sub + 1, pl.ds(bidx, 1), :]
    gate = 1.0 + mod_ref[3 * sub + 2, pl.ds(bidx, 1), :]
    return shift, scale, gate


def _ada_kernel(c_ref, w_ref, b_ref, o_ref):
    c = c_ref[...]
    cond = (c * jax.nn.sigmoid(c)).astype(BF16)
    o_ref[...] = jnp.dot(cond, w_ref[...].astype(BF16), preferred_element_type=F32) + b_ref[...]


def _ada(c, ada_w, ada_b):
    bsz = c.shape[0]
    n_blk = ada_w.shape[2] // D_MODEL
    return pl.pallas_call(
        _ada_kernel,
        grid=(DEPTH, n_blk),
        in_specs=[
            pl.BlockSpec((bsz, D_MODEL), lambda l, n: (0, 0)),
            pl.BlockSpec((None, D_MODEL, D_MODEL), lambda l, n: (l, 0, n)),
            pl.BlockSpec((None, None, 1, D_MODEL), lambda l, n: (l, n, 0, 0)),
        ],
        out_specs=pl.BlockSpec((None, None, bsz, D_MODEL), lambda l, n: (l, n, 0, 0)),
        out_shape=jax.ShapeDtypeStruct((DEPTH, n_blk, bsz, D_MODEL), F32),
        compiler_params=_params(("arbitrary", "arbitrary"), 24),
        name="ada_mods",
    )(c, ada_w, ada_b.reshape(DEPTH, n_blk, 1, D_MODEL))


W_ROWS = 256


def _ffn_kernel(x_ref, mod_ref, w1_hbm, w3_hbm, w2_hbm, g_ref, b_ref, o_ref,
                w1_ref, w3_ref, w2_ref, wide_ref, narrow_ref, sem, *, sub, ts, layer, half):
    shift, scale, gate = _mods(mod_ref, sub, pl.program_id(0))
    n_f = D_FF // F_CHUNK
    chunk = lambda f: slice(f * F_CHUNK, (f + 1) * F_CHUNK)

    @pl.when(jnp.logical_and(pl.program_id(0) == 0, pl.program_id(1) == 0))
    def _():
        chunks = [(src, dst, stage, kind, slice(r, r + W_ROWS))
                  for src, dst, stage, kind in ((w1_hbm, w1_ref, wide_ref, 0), (w3_hbm, w3_ref, wide_ref, 0),
                                                (w2_hbm, w2_ref, narrow_ref, 1))
                  for r in range(0, dst.shape[0], W_ROWS)]

        def copy(i):
            src, _, stage, kind, rows = chunks[i]
            return pltpu.make_async_copy(src.at[layer, half, rows, :], stage.at[i % 2], sem.at[kind, i % 2])

        copy(0).start()
        for i, (_, dst, stage, _, rows) in enumerate(chunks):
            if i + 1 < len(chunks):
                copy(i + 1).start()
            copy(i).wait()
            dst[rows, :] = stage[i % 2].astype(BF16)

    def modulated(rows):
        return (x_ref[rows, :] * (1.0 + scale) + shift).astype(BF16)

    def up(h, f):
        return (jnp.dot(h, w1_ref[:, chunk(f)], preferred_element_type=F32),
                jnp.dot(h, w3_ref[:, chunk(f)], preferred_element_type=F32))

    def down(ab, acc, f):
        a, b = ab
        u = (a * jax.nn.sigmoid(a) * b).astype(BF16)
        d = jnp.dot(u, w2_ref[chunk(f), :], preferred_element_type=F32)
        return d if acc is None else acc + d

    def epilogue(rows, acc):
        r = ALPHA * x_ref[rows, :] + gate * (0.5 * acc)
        o_ref[rows, :] = _layer_norm(r, g_ref[...], b_ref[...])

    n_sub = x_ref.shape[0] // ts
    tile = lambda t: slice(t * ts, (t + 1) * ts)
    h = modulated(tile(0))
    nxt = up(h, 0)
    for t in range(n_sub):
        acc = None
        for f in range(n_f):
            cur = nxt
            if f + 1 < n_f:
                nxt = up(h, f + 1)
            elif t + 1 < n_sub:
                nxt = up(h_next, 0)
            if f == n_f // 2 and t + 1 < n_sub:
                h_next = modulated(tile(t + 1))
            acc = down(cur, acc, f)
        epilogue(tile(t), acc)
        if t + 1 < n_sub:
            h = h_next


def _ffn(x, mods, w1, w3, w2, ln_g, ln_b, *, layer, half, sub, tm=1024, ts=256):
    bsz, seq, _ = x.shape
    return pl.pallas_call(
        functools.partial(_ffn_kernel, sub=sub, ts=ts, layer=layer, half=half),
        grid=(bsz, seq // tm),
        in_specs=[
            pl.BlockSpec((None, tm, D_MODEL), lambda b, i: (b, i, 0)),
            _resident((None, 9, bsz, D_MODEL), lambda b, i: (layer, 0, 0, 0)),
            pl.BlockSpec(memory_space=pl.ANY),
            pl.BlockSpec(memory_space=pl.ANY),
            pl.BlockSpec(memory_space=pl.ANY),
            _resident((None, None, 1, D_MODEL), lambda b, i: (layer, sub, 0, 0)),
            _resident((None, None, 1, D_MODEL), lambda b, i: (layer, sub, 0, 0)),
        ],
        out_specs=pl.BlockSpec((None, tm, D_MODEL), lambda b, i: (b, i, 0)),
        out_shape=jax.ShapeDtypeStruct(x.shape, F32),
        scratch_shapes=[
            pltpu.VMEM((D_MODEL, D_FF), BF16), pltpu.VMEM((D_MODEL, D_FF), BF16),
            pltpu.VMEM((D_FF, D_MODEL), BF16),
            pltpu.VMEM((2, W_ROWS, D_FF), F32), pltpu.VMEM((2, W_ROWS, D_MODEL), F32),
            pltpu.SemaphoreType.DMA((2, 2)),
        ],
        compiler_params=_params(("arbitrary", "arbitrary"), 52),
        name=f"ffn_l{layer}_h{half}",
    )(x, mods, w1, w3, w2, ln_g, ln_b)


def _modproj_kernel(x_ref, mod_ref, w_ref, o_ref, *, sub):
    shift, scale, _ = _mods(mod_ref, sub, pl.program_id(0))
    h = (x_ref[...] * (1.0 + scale) + shift).astype(BF16)
    for n in range(o_ref.shape[1] // D_MODEL):
        cols = slice(n * D_MODEL, (n + 1) * D_MODEL)
        o_ref[:, cols] = jnp.dot(h, w_ref[:, cols], preferred_element_type=F32).astype(o_ref.dtype)


def _modproj(x, mods, w, *, layer, sub, tm=1024):
    bsz, seq, _ = x.shape
    n_out = w.shape[1]
    return pl.pallas_call(
        functools.partial(_modproj_kernel, sub=sub),
        grid=(bsz, seq // tm),
        in_specs=[
            pl.BlockSpec((None, tm, D_MODEL), lambda b, i: (b, i, 0)),
            _resident((None, 9, bsz, D_MODEL), lambda b, i: (layer, 0, 0, 0)),
            _resident((D_MODEL, n_out), lambda b, i: (0, 0)),
        ],
        out_specs=pl.BlockSpec((None, tm, n_out), lambda b, i: (b, i, 0)),
        out_shape=jax.ShapeDtypeStruct((bsz, seq, n_out), BF16),
        compiler_params=_params(("arbitrary", "arbitrary"), 44),
        name=f"modproj_l{layer}",
    )(x, mods, w)


def _projres_kernel(a_ref, x_ref, mod_ref, w_ref, g_ref, b_ref, o_ref, *, sub, ts):
    _, _, gate = _mods(mod_ref, sub, pl.program_id(0))
    n_sub = x_ref.shape[0] // ts
    proj = lambda t: jnp.dot(a_ref[t * ts:(t + 1) * ts, :], w_ref[...], preferred_element_type=F32)
    nxt = proj(0)
    for t in range(n_sub):
        y, rows = nxt, slice(t * ts, (t + 1) * ts)
        if t + 1 < n_sub:
            nxt = proj(t + 1)
        o_ref[rows, :] = _layer_norm(ALPHA * x_ref[rows, :] + gate * y, g_ref[...], b_ref[...])


def _projres(a, x, mods, w, ln_g, ln_b, *, layer, sub, tm=1024, ts=256):
    bsz, seq, _ = x.shape
    return pl.pallas_call(
        functools.partial(_projres_kernel, sub=sub, ts=ts),
        grid=(bsz, seq // tm),
        in_specs=[
            pl.BlockSpec((None, tm, D_MODEL), lambda b, i: (b, i, 0)),
            pl.BlockSpec((None, tm, D_MODEL), lambda b, i: (b, i, 0)),
            _resident((None, 9, bsz, D_MODEL), lambda b, i: (layer, 0, 0, 0)),
            _resident((D_MODEL, D_MODEL), lambda b, i: (0, 0)),
            _resident((None, None, 1, D_MODEL), lambda b, i: (layer, sub, 0, 0)),
            _resident((None, None, 1, D_MODEL), lambda b, i: (layer, sub, 0, 0)),
        ],
        out_specs=pl.BlockSpec((None, tm, D_MODEL), lambda b, i: (b, i, 0)),
        out_shape=jax.ShapeDtypeStruct(x.shape, F32),
        compiler_params=_params(("arbitrary", "arbitrary"), 36),
        name=f"projres_l{layer}",
    )(a, x, mods, w, ln_g, ln_b)


def _attn_kernel(q_ref, k_ref, v_ref, lam_ref, sg_ref, o_ref, vx_ref, *, lam_init, tq):
    seq = q_ref.shape[0]
    hd = ATTN_HEAD_DIM
    lam = lam_ref[...]
    lam_full = (jnp.exp(jnp.sum(lam[0:1] * lam[1:2], axis=-1, keepdims=True))
                - jnp.exp(jnp.sum(lam[2:3] * lam[3:4], axis=-1, keepdims=True)) + lam_init)
    lane = lax.broadcasted_iota(jnp.int32, (tq, 2 * hd), 1)
    rq = lax.broadcasted_iota(jnp.int32, (tq, tq), 0) // CHUNK
    ck = lax.broadcasted_iota(jnp.int32, (tq, tq), 1) // CHUNK
    allowed = ck <= rq
    nt = (((1,), (1,)), ((), ()))

    hw = 2 * hd
    vx_ref[:, :hw] = v_ref[...]
    vx_ref[:, hw:] = jnp.ones((seq, hw), BF16)

    def score(i, m):
        kend = (i + 1) * tq
        q = q_ref[i * tq:kend, :].astype(F32) * (hd ** -0.5)
        qm = jnp.where((lane < hd) if m == 0 else (lane >= hd), q, 0.0).astype(BF16)
        s = lax.dot_general(qm, k_ref[0:kend, :], nt, preferred_element_type=F32)
        diag = jnp.where(allowed, s[:, kend - tq:], -jnp.inf)
        return diag if kend == tq else jnp.concatenate([s[:, :kend - tq], diag], axis=1)

    def attend(i, s):
        kend = (i + 1) * tq
        p = jnp.exp(s - jnp.max(s, axis=-1, keepdims=True))
        r = jnp.dot(p.astype(BF16), vx_ref[0:kend, :], preferred_element_type=F32)
        return r[:, :hw] / r[:, hw:]

    def finish(i, o1, o2):
        o = o1 - lam_full * o2
        o = o * lax.rsqrt(jnp.mean(o * o, axis=-1, keepdims=True) + LN_EPS)
        o_ref[i * tq:(i + 1) * tq, :] = (o * sg_ref[...] * (1.0 - lam_init)).astype(o_ref.dtype)

    n = seq // tq
    units = [(i, m) for i in reversed(range(n)) for m in range(2)]
    pend = score(*units[0])
    outs = {}
    for u, (i, m) in enumerate(units):
        cur = pend
        if u + 1 < len(units):
            pend = score(*units[u + 1])
        outs[(i, m)] = attend(i, cur)
        if m == 1:
            finish(i, outs.pop((i, 0)), outs.pop((i, 1)))


def _attention(qkv, lam, subln_g, *, lam_init, tq=256):
    bsz, seq, _ = qkv.shape
    hw = 2 * ATTN_HEAD_DIM
    return pl.pallas_call(
        functools.partial(_attn_kernel, lam_init=lam_init, tq=tq),
        grid=(bsz, ATTN_HEADS),
        in_specs=[
            pl.BlockSpec((None, seq, hw), lambda b, h: (b, 0, h)),
            pl.BlockSpec((None, seq, hw), lambda b, h: (b, 0, ATTN_HEADS + h)),
            pl.BlockSpec((None, seq, hw), lambda b, h: (b, 0, 2 * ATTN_HEADS + h)),
            _resident((4, ATTN_HEAD_DIM), lambda b, h: (0, 0)),
            _resident((1, hw), lambda b, h: (0, 0)),
        ],
        out_specs=pl.BlockSpec((None, seq, hw), lambda b, h: (b, 0, h)),
        out_shape=jax.ShapeDtypeStruct((bsz, seq, D_MODEL), BF16),
        scratch_shapes=[pltpu.VMEM((seq, 2 * hw), BF16)],
        compiler_params=_params(("arbitrary", "arbitrary"), 40),
        name="diff_attention",
    )(qkv, qkv, qkv, lam, subln_g.reshape(1, hw))


ROWS = 2 * CHUNK
PAIRS = SSM_GROUP // 2


def _lane_halves(e, o, lane):
    lo = jnp.where(lane < CHUNK, e, pltpu.roll(o, CHUNK, 1))
    hi = jnp.where(lane < CHUNK, pltpu.roll(e, CHUNK, 1), o)
    return lo, hi


def _s5in_kernel(x_ref, mod_ref, w_ref, o_ref, a_ref, s_ref):
    bsz = x_ref.shape[0]
    shift = mod_ref[3, :, :][:, None, :]
    scale = mod_ref[4, :, :][:, None, :]
    h = (x_ref[...] * (1.0 + scale) + shift).astype(BF16).reshape(bsz * ROWS, D_MODEL)
    a_ref[...] = lax.dot_general(w_ref[...], h, (((1,), (1,)), ((), ())), preferred_element_type=F32)
    lane = lax.broadcasted_iota(jnp.int32, (SSM_GROUPS, LANES), 1)
    half = SSM_GROUPS * bsz
    for q in range(PAIRS):
        for b in range(bsz):
            x0 = a_ref[(2 * q) * SSM_GROUPS:(2 * q + 1) * SSM_GROUPS, b * LANES:(b + 1) * LANES]
            x1 = a_ref[(2 * q + 1) * SSM_GROUPS:(2 * q + 2) * SSM_GROUPS, b * LANES:(b + 1) * LANES]
            c0, c1 = _lane_halves(x0, x1, lane)
            s_ref[q, pl.ds(b, SSM_GROUPS, stride=bsz), :] = c0
            s_ref[q, pl.ds(half + b, SSM_GROUPS, stride=bsz), :] = c1
    for c2 in range(2):
        for q in range(PAIRS):
            o_ref[c2, :, :, q * LANES:(q + 1) * LANES] = (
                s_ref[q, c2 * half:(c2 + 1) * half, :].reshape(SSM_GROUPS, bsz, LANES))


def _s5in(x, mods, w_t, *, layer):
    bsz, seq, _ = x.shape
    width = SSM_GROUP * CHUNK
    return pl.pallas_call(
        _s5in_kernel,
        grid=(seq // ROWS,),
        in_specs=[
            pl.BlockSpec((bsz, ROWS, D_MODEL), lambda j: (0, j, 0)),
            _resident((None, 9, bsz, D_MODEL), lambda j: (layer, 0, 0, 0)),
            _resident((D_MODEL, D_MODEL), lambda j: (0, 0)),
        ],
        out_specs=pl.BlockSpec((2, SSM_GROUPS, bsz, width), lambda j: (j, 0, 0, 0)),
        out_shape=jax.ShapeDtypeStruct((seq // CHUNK, SSM_GROUPS, bsz, width), F32),
        scratch_shapes=[
            pltpu.VMEM((D_MODEL, bsz * ROWS), F32),
            pltpu.VMEM((PAIRS, 2 * SSM_GROUPS * bsz, LANES), F32),
        ],
        compiler_params=_params(("arbitrary",), 40),
        name="s5_in",
    )(x, mods, w_t)


def _s5scan_kernel(u_ref, arr_ref, air_ref, arc_ref, aic_ref, ldt_ref, btr_ref, bti_ref,
                   ctr_ref, cti_ref, d_ref, y_ref, t_ref, ws_ref, wc_ref):
    n_chunks, bsz, width = u_ref.shape
    st = SSM_STATE
    dt = jnp.exp(ldt_ref[...])

    ar, ai = arr_ref[...], air_ref[...]
    mag = jnp.exp(ar * dt)
    abr, abi = mag * jnp.cos(ai * dt), mag * jnp.sin(ai * dt)
    den = ar * ar + ai * ai
    pr, pim = abr - 1.0, abi
    cfr, cfi = (pr * ar + pim * ai) / den, (pim * ar - pr * ai) / den
    btr, bti = btr_ref[...], bti_ref[...]
    bbr, bbi = cfr * btr - cfi * bti, cfr * bti + cfi * btr

    arc, aic = arc_ref[...], aic_ref[...]
    lane = lax.broadcasted_iota(jnp.int32, (1, LANES), 1)
    first = lane < CHUNK
    ctr, cti = ctr_ref[...], cti_ref[...]

    expo = jnp.where(first, lane, 2 * CHUNK - 1 - lane).astype(F32)
    m = jnp.exp(arc * dt * expo)
    pw_r, pw_i = m * jnp.cos(aic * dt * expo), m * jnp.sin(aic * dt * expo)
    asc_r = jnp.where(first, pw_r, pltpu.roll(pw_r, CHUNK, 1))
    asc_i = jnp.where(first, pw_i, pltpu.roll(pw_i, CHUNK, 1))
    ab_r, ab_i = pw_r[:, 1:2], pw_i[:, 1:2]
    nxt_r, nxt_i = asc_r * ab_r - asc_i * ab_i, asc_r * ab_i + asc_i * ab_r

    def c_times(er, ei):
        xr, xi = [], []
        for q in range(PAIRS):
            cr = jnp.where(first, ctr[:, 2 * q:2 * q + 1], ctr[:, 2 * q + 1:2 * q + 2])
            ci = jnp.where(first, cti[:, 2 * q:2 * q + 1], cti[:, 2 * q + 1:2 * q + 2])
            xr.append(cr * er - ci * ei)
            xi.append(cr * ei + ci * er)
        return jnp.concatenate(xr, axis=1), jnp.concatenate(xi, axis=1)

    xr, xi = c_times(asc_r, asc_i)
    hi = lax.Precision.HIGHEST
    kflat = (jnp.dot(bbr, xr, precision=hi, preferred_element_type=F32)
             - jnp.dot(bbi, xi, precision=hi, preferred_element_type=F32))
    prow = lax.broadcasted_iota(jnp.int32, kflat.shape, 0)
    plane = lax.broadcasted_iota(jnp.int32, kflat.shape, 1)
    kflat = kflat + jnp.where(plane == prow * CHUNK, d_ref[...], 0.0)

    srow = lax.broadcasted_iota(jnp.int32, (CHUNK, LANES), 0)
    keep = (lax.broadcasted_iota(jnp.int32, (CHUNK, LANES), 1) % CHUNK) >= srow
    for p in range(SSM_GROUP):
        rows = jnp.broadcast_to(kflat[p:p + 1, :], (CHUNK, width))
        for q in range(PAIRS):
            blk = pltpu.roll(rows[:, q * LANES:(q + 1) * LANES], 0, 1, stride=1, stride_axis=0)
            t_ref[p * CHUNK:(p + 1) * CHUNK, q * LANES:(q + 1) * LANES] = (
                jnp.where(keep, blk, 0.0).astype(BF16))

    er, ei = pw_r.T[CHUNK:, :], pw_i.T[CHUNK:, :]
    zpad = jnp.zeros((CHUNK, LANES - st), F32)
    for p in range(SSM_GROUP):
        br, bi = bbr[p:p + 1, :], bbi[p:p + 1, :]
        ws_ref[p * CHUNK:(p + 1) * CHUNK, :] = jnp.concatenate(
            [er * br - ei * bi, zpad, er * bi + ei * br, zpad], axis=1).astype(BF16)

    pr_, pi_ = c_times(nxt_r, nxt_i)
    zrows = jnp.zeros((LANES - st, width), F32)
    wc_ref[...] = jnp.concatenate([pr_, zrows, -pi_, zrows], axis=0).astype(BF16)

    u = u_ref[...].reshape(n_chunks * bsz, width).astype(BF16)
    s = jnp.dot(u, ws_ref[...], preferred_element_type=F32)
    m64 = jnp.exp(ar * dt * CHUNK)
    zlane = jnp.zeros((1, LANES - st), F32)
    a64r = jnp.concatenate([m64 * jnp.cos(ai * dt * CHUNK), zlane], axis=1)
    a64i = jnp.concatenate([m64 * jnp.sin(ai * dt * CHUNK), zlane], axis=1)
    hr = jnp.zeros((bsz, LANES), F32)
    hi_ = jnp.zeros((bsz, LANES), F32)
    prev = []
    for c in range(n_chunks):
        prev.append(jnp.concatenate([hr, hi_], axis=1))
        sr, si = s[c * bsz:(c + 1) * bsz, :LANES], s[c * bsz:(c + 1) * bsz, LANES:]
        hr, hi_ = a64r * hr - a64i * hi_ + sr, a64r * hi_ + a64i * hr + si
    hprev = jnp.concatenate(prev, axis=0).astype(BF16)

    y = (jnp.dot(u, t_ref[...], preferred_element_type=F32)
         + jnp.dot(hprev, wc_ref[...], preferred_element_type=F32))
    y_ref[...] = y.reshape(n_chunks, bsz, width)


def _s5scan(u4, a_re, a_im, log_dt, b_re, b_im, c_re, c_im, d):
    n_chunks, groups, bsz, width = u4.shape
    st, pg = SSM_STATE, SSM_GROUP
    per_g = lambda *shape: pl.BlockSpec((None,) + shape, lambda g: (g,) + (0,) * len(shape))
    return pl.pallas_call(
        _s5scan_kernel,
        grid=(groups,),
        in_specs=[
            pl.BlockSpec((n_chunks, None, bsz, width), lambda g: (0, g, 0, 0)),
            per_g(1, st), per_g(1, st), per_g(st, 1), per_g(st, 1), per_g(1, 1),
            per_g(pg, st), per_g(pg, st), per_g(st, pg), per_g(st, pg), per_g(pg, 1),
        ],
        out_specs=pl.BlockSpec((n_chunks, None, bsz, width), lambda g: (0, g, 0, 0)),
        out_shape=jax.ShapeDtypeStruct(u4.shape, F32),
        scratch_shapes=[
            pltpu.VMEM((width, width), BF16),
            pltpu.VMEM((width, 2 * LANES), BF16),
            pltpu.VMEM((2 * LANES, width), BF16),
        ],
        compiler_params=_params(("arbitrary",), 32),
        name="s5_scan",
    )(u4,
      a_re.reshape(groups, 1, st), a_im.reshape(groups, 1, st),
      a_re.reshape(groups, st, 1), a_im.reshape(groups, st, 1),
      log_dt.reshape(groups, 1, 1),
      jnp.swapaxes(b_re, 1, 2), jnp.swapaxes(b_im, 1, 2),
      jnp.swapaxes(c_re, 1, 2), jnp.swapaxes(c_im, 1, 2),
      d.reshape(groups, pg, 1))


def _s5out_kernel(y_ref, x_ref, mod_ref, wg_ref, wo_ref, g_ref, b_ref, o_ref, a_ref, s_ref):
    bsz = x_ref.shape[0]
    half = SSM_GROUPS * bsz
    for c2 in range(2):
        for q in range(PAIRS):
            s_ref[q, c2 * half:(c2 + 1) * half, :] = (
                y_ref[c2, :, :, q * LANES:(q + 1) * LANES].reshape(half, LANES))
    lane = lax.broadcasted_iota(jnp.int32, (SSM_GROUPS, LANES), 1)
    for q in range(PAIRS):
        for b in range(bsz):
            c0 = s_ref[q, pl.ds(b, SSM_GROUPS, stride=bsz), :]
            c1 = s_ref[q, pl.ds(half + b, SSM_GROUPS, stride=bsz), :]
            x0, x1 = _lane_halves(c0, c1, lane)
            a_ref[(2 * q) * SSM_GROUPS:(2 * q + 1) * SSM_GROUPS, b * LANES:(b + 1) * LANES] = x0
            a_ref[(2 * q + 1) * SSM_GROUPS:(2 * q + 2) * SSM_GROUPS, b * LANES:(b + 1) * LANES] = x1
    z = jax.nn.gelu(a_ref[...], approximate=True)
    gt = jnp.dot(wg_ref[...], z.astype(BF16), preferred_element_type=F32)
    zz = (z * jax.nn.sigmoid(gt)).astype(BF16)
    y = lax.dot_general(zz, wo_ref[...], (((0,), (0,)), ((), ())), preferred_element_type=F32)
    y = y.reshape(bsz, ROWS, D_MODEL)
    gate = 1.0 + mod_ref[5, :, :][:, None, :]
    o_ref[...] = _layer_norm(ALPHA * x_ref[...] + gate * y, g_ref[...], b_ref[...])


def _s5out(y4, x, mods, wg_t, wo, ln_g, ln_b, *, layer):
    bsz, seq, _ = x.shape
    width = SSM_GROUP * CHUNK
    return pl.pallas_call(
        _s5out_kernel,
        grid=(seq // ROWS,),
        in_specs=[
            pl.BlockSpec((2, SSM_GROUPS, bsz, width), lambda j: (j, 0, 0, 0)),
            pl.BlockSpec((bsz, ROWS, D_MODEL), lambda j: (0, j, 0)),
            _resident((None, 9, bsz, D_MODEL), lambda j: (layer, 0, 0, 0)),
            _resident((D_MODEL, D_MODEL), lambda j: (0, 0)),
            _resident((D_MODEL, D_MODEL), lambda j: (0, 0)),
            _resident((None, None, 1, D_MODEL), lambda j: (layer, 1, 0, 0)),
            _resident((None, None, 1, D_MODEL), lambda j: (layer, 1, 0, 0)),
        ],
        out_specs=pl.BlockSpec((bsz, ROWS, D_MODEL), lambda j: (0, j, 0)),
        out_shape=jax.ShapeDtypeStruct(x.shape, F32),
        scratch_shapes=[
            pltpu.VMEM((D_MODEL, bsz * ROWS), F32),
            pltpu.VMEM((PAIRS, 2 * SSM_GROUPS * bsz, LANES), F32),
        ],
        compiler_params=_params(("arbitrary",), 48),
        name="s5_out",
    )(y4, x, mods, wg_t, wo, ln_g, ln_b)


def kernel(x, c, ada_w, ada_b, ln_g, ln_b, ffn_w1, ffn_w3, ffn_w2, attn_w_in, attn_lam, attn_subln_g, attn_w_out, ssm_w_in, ssm_a_re, ssm_a_im, ssm_log_dt, ssm_b_re, ssm_b_im, ssm_c_re, ssm_c_im, ssm_d, ssm_w_gate, ssm_w_out):
    mods = _ada(c, ada_w, ada_b)
    lng = ln_g.reshape(DEPTH, 3, 1, D_MODEL)
    lnb = ln_b.reshape(DEPTH, 3, 1, D_MODEL)
    w1, w3, w2 = ffn_w1, ffn_w3, ffn_w2
    gp = (SSM_GROUPS, SSM_GROUP)

    for layer in range(DEPTH):
        i = layer // 2
        x = _ffn(x, mods, w1, w3, w2, lng, lnb, layer=layer, half=0, sub=0)
        if layer % 2 == 0:
            lam_init = 0.8 - 0.6 * math.exp(-0.3 * layer)
            qkv = _modproj(x, mods, attn_w_in[i].astype(BF16), layer=layer, sub=1)
            o = _attention(qkv, attn_lam[i], attn_subln_g[i], lam_init=lam_init)
            x = _projres(o, x, mods, attn_w_out[i].astype(BF16), lng, lnb, layer=layer, sub=1)
        else:
            rows_pg = lambda w: w.reshape(*gp, D_MODEL).transpose(1, 0, 2).reshape(D_MODEL, D_MODEL)
            w_in_t = rows_pg(ssm_w_in[i].astype(BF16).T)
            wg_t = rows_pg(rows_pg(ssm_w_gate[i].astype(BF16)).T)
            wo = rows_pg(ssm_w_out[i].astype(BF16))
            u4 = _s5in(x, mods, w_in_t, layer=layer)
            y4 = _s5scan(u4, ssm_a_re[i], ssm_a_im[i], ssm_log_dt[i], ssm_b_re[i], ssm_b_im[i],
                         ssm_c_re[i], ssm_c_im[i], ssm_d[i])
            x = _s5out(y4, x, mods, wg_t, wo, lng, lnb, layer=layer)
        x = _ffn(x, mods, w1, w3, w2, lng, lnb, layer=layer, half=1, sub=2)
    return x
```

```python
import functools
import math

import jax
import jax.numpy as jnp
from jax import lax
from jax.experimental import pallas as pl
from jax.experimental.pallas import tpu as pltpu

D_MODEL = 1024
DEPTH = 2
CHUNK = 64
ATTN_HEADS = 8
ATTN_HEAD_DIM = 64
SSM_GROUP = 16
SSM_GROUPS = D_MODEL // SSM_GROUP
SSM_STATE = 64
D_FF = 2816
ALPHA = (2 * DEPTH) ** 0.25
LN_EPS = 1e-5

LANES = 128
F_CHUNK = 256
MIB = 1024 * 1024

F32 = jnp.float32
BF16 = jnp.bfloat16


def _params(sem, vmem_mib):
    return pltpu.CompilerParams(dimension_semantics=sem, vmem_limit_bytes=vmem_mib * MIB)


def _resident(block_shape, index_map):
    return pl.BlockSpec(block_shape, index_map, pipeline_mode=pl.Buffered(1))


def _layer_norm(r, g, b):
    mu = jnp.mean(r, axis=-1, keepdims=True)
    d = r - mu
    var = jnp.mean(d * d, axis=-1, keepdims=True)
    return d * lax.rsqrt(var + LN_EPS) * g + b


def _mods(mod_ref, sub, bidx):
    shift = mod_ref[3 * sub + 0, pl.ds(bidx, 1), :]
    scale = mod_ref[3 * sub + 1, pl.ds(bidx, 1), :]
    gate = 1.0 + mod_ref[3 * sub + 2, pl.ds(bidx, 1), :]
    return shift, scale, gate


def _ada_kernel(c_ref, w_ref, b_ref, o_ref):
    c = c_ref[...]
    cond = (c * jax.nn.sigmoid(c)).astype(BF16)
    o_ref[...] = jnp.dot(cond, w_ref[...].astype(BF16), preferred_element_type=F32) + b_ref[...]


def _ada(c, ada_w, ada_b):
    bsz = c.shape[0]
    n_blk = ada_w.shape[2] // D_MODEL
    return pl.pallas_call(
        _ada_kernel,
        grid=(DEPTH, n_blk),
        in_specs=[
            pl.BlockSpec((bsz, D_MODEL), lambda l, n: (0, 0)),
            pl.BlockSpec((None, D_MODEL, D_MODEL), lambda l, n: (l, 0, n)),
            pl.BlockSpec((None, None, 1, D_MODEL), lambda l, n: (l, n, 0, 0)),
        ],
        out_specs=pl.BlockSpec((None, None, bsz, D_MODEL), lambda l, n: (l, n, 0, 0)),
        out_shape=jax.ShapeDtypeStruct((DEPTH, n_blk, bsz, D_MODEL), F32),
        compiler_params=_params(("arbitrary", "arbitrary"), 24),
        name="ada_mods",
    )(c, ada_w, ada_b.reshape(DEPTH, n_blk, 1, D_MODEL))


W_ROWS = 256


def _ffn_kernel(x_ref, mod_ref, w1_in, w3_in, w2_in, g_ref, b_ref, o_ref, *scratch, sub, ts, layer, half):
    shift, scale, gate = _mods(mod_ref, sub, pl.program_id(0))
    n_f = D_FF // F_CHUNK
    chunk = lambda f: slice(f * F_CHUNK, (f + 1) * F_CHUNK)
    w1_ref, w3_ref, w2_ref = scratch[:3] if scratch else (w1_in, w3_in, w2_in)

    def stream_weights():
        wide_ref, narrow_ref, sem = scratch[3:]
        chunks = [(src, dst, stage, kind, slice(r, r + W_ROWS))
                  for src, dst, stage, kind in ((w1_in, w1_ref, wide_ref, 0), (w3_in, w3_ref, wide_ref, 0),
                                                (w2_in, w2_ref, narrow_ref, 1))
                  for r in range(0, dst.shape[0], W_ROWS)]

        def copy(i):
            src, _, stage, kind, rows = chunks[i]
            return pltpu.make_async_copy(src.at[layer, half, rows, :], stage.at[i % 2], sem.at[kind, i % 2])

        copy(0).start()
        for i, (_, dst, stage, _, rows) in enumerate(chunks):
            if i + 1 < len(chunks):
                copy(i + 1).start()
            copy(i).wait()
            dst[rows, :] = stage[i % 2].astype(BF16)

    if scratch:
        pl.when(jnp.logical_and(pl.program_id(0) == 0, pl.program_id(1) == 0))(stream_weights)

    def modulated(rows):
        return (x_ref[rows, :] * (1.0 + scale) + shift).astype(BF16)

    def up(h, f):
        return (jnp.dot(h, w1_ref[:, chunk(f)], preferred_element_type=F32),
                jnp.dot(h, w3_ref[:, chunk(f)], preferred_element_type=F32))

    def down(ab, acc, f):
        a, b = ab
        u = (a * jax.nn.sigmoid(a) * b).astype(BF16)
        d = jnp.dot(u, w2_ref[chunk(f), :], preferred_element_type=F32)
        return d if acc is None else acc + d

    def epilogue(rows, acc):
        r = ALPHA * x_ref[rows, :] + gate * (0.5 * acc)
        o_ref[rows, :] = _layer_norm(r, g_ref[...], b_ref[...])

    n_sub = x_ref.shape[0] // ts
    tile = lambda t: slice(t * ts, (t + 1) * ts)
    h = modulated(tile(0))
    nxt = up(h, 0)
    for t in range(n_sub):
        acc = None
        for f in range(n_f):
            cur = nxt
            if f + 1 < n_f:
                nxt = up(h, f + 1)
            elif t + 1 < n_sub:
                nxt = up(h_next, 0)
            if f == n_f // 2 and t + 1 < n_sub:
                h_next = modulated(tile(t + 1))
            acc = down(cur, acc, f)
        epilogue(tile(t), acc)
        if t + 1 < n_sub:
            h = h_next


def _ffn(x, mods, w1, w3, w2, ln_g, ln_b, *, layer, half, sub, tm=1024, ts=256):
    bsz, seq, _ = x.shape
    if w1.dtype == BF16:
        wspec = lambda r, c: _resident((None, None, r, c), lambda b, i: (layer, half, 0, 0))
        w_specs = [wspec(D_MODEL, D_FF), wspec(D_MODEL, D_FF), wspec(D_FF, D_MODEL)]
        scratch = []
    else:
        w_specs = [pl.BlockSpec(memory_space=pl.ANY)] * 3
        scratch = [
            pltpu.VMEM((D_MODEL, D_FF), BF16), pltpu.VMEM((D_MODEL, D_FF), BF16),
            pltpu.VMEM((D_FF, D_MODEL), BF16),
            pltpu.VMEM((2, W_ROWS, D_FF), F32), pltpu.VMEM((2, W_ROWS, D_MODEL), F32),
            pltpu.SemaphoreType.DMA((2, 2)),
        ]
    return pl.pallas_call(
        functools.partial(_ffn_kernel, sub=sub, ts=ts, layer=layer, half=half),
        grid=(bsz, seq // tm),
        in_specs=[
            pl.BlockSpec((None, tm, D_MODEL), lambda b, i: (b, i, 0)),
            _resident((None, 9, bsz, D_MODEL), lambda b, i: (layer, 0, 0, 0)),
            *w_specs,
            _resident((None, None, 1, D_MODEL), lambda b, i: (layer, sub, 0, 0)),
            _resident((None, None, 1, D_MODEL), lambda b, i: (layer, sub, 0, 0)),
        ],
        out_specs=pl.BlockSpec((None, tm, D_MODEL), lambda b, i: (b, i, 0)),
        out_shape=jax.ShapeDtypeStruct(x.shape, F32),
        scratch_shapes=scratch,
        compiler_params=_params(("arbitrary", "arbitrary"), 52),
        name=f"ffn_l{layer}_h{half}",
    )(x, mods, w1, w3, w2, ln_g, ln_b)


def _modproj_kernel(x_ref, mod_ref, w_ref, o_ref, *, sub):
    shift, scale, _ = _mods(mod_ref, sub, pl.program_id(0))
    h = (x_ref[...] * (1.0 + scale) + shift).astype(BF16)
    for n in range(o_ref.shape[1] // D_MODEL):
        cols = slice(n * D_MODEL, (n + 1) * D_MODEL)
        o_ref[:, cols] = jnp.dot(h, w_ref[:, cols], preferred_element_type=F32).astype(o_ref.dtype)


def _modproj(x, mods, w, *, layer, sub, tm=1024):
    bsz, seq, _ = x.shape
    n_out = w.shape[1]
    return pl.pallas_call(
        functools.partial(_modproj_kernel, sub=sub),
        grid=(bsz, seq // tm),
        in_specs=[
            pl.BlockSpec((None, tm, D_MODEL), lambda b, i: (b, i, 0)),
            _resident((None, 9, bsz, D_MODEL), lambda b, i: (layer, 0, 0, 0)),
            _resident((D_MODEL, n_out), lambda b, i: (0, 0)),
        ],
        out_specs=pl.BlockSpec((None, tm, n_out), lambda b, i: (b, i, 0)),
        out_shape=jax.ShapeDtypeStruct((bsz, seq, n_out), BF16),
        compiler_params=_params(("arbitrary", "arbitrary"), 44),
        name=f"modproj_l{layer}",
    )(x, mods, w)


def _projres_kernel(a_ref, x_ref, mod_ref, w_ref, g_ref, b_ref, o_ref, *, sub, ts):
    _, _, gate = _mods(mod_ref, sub, pl.program_id(0))
    n_sub = x_ref.shape[0] // ts
    proj = lambda t: jnp.dot(a_ref[t * ts:(t + 1) * ts, :], w_ref[...], preferred_element_type=F32)
    nxt = proj(0)
    for t in range(n_sub):
        y, rows = nxt, slice(t * ts, (t + 1) * ts)
        if t + 1 < n_sub:
            nxt = proj(t + 1)
        o_ref[rows, :] = _layer_norm(ALPHA * x_ref[rows, :] + gate * y, g_ref[...], b_ref[...])


def _projres(a, x, mods, w, ln_g, ln_b, *, layer, sub, tm=1024, ts=256):
    bsz, seq, _ = x.shape
    return pl.pallas_call(
        functools.partial(_projres_kernel, sub=sub, ts=ts),
        grid=(bsz, seq // tm),
        in_specs=[
            pl.BlockSpec((None, tm, D_MODEL), lambda b, i: (b, i, 0)),
            pl.BlockSpec((None, tm, D_MODEL), lambda b, i: (b, i, 0)),
            _resident((None, 9, bsz, D_MODEL), lambda b, i: (layer, 0, 0, 0)),
            _resident((D_MODEL, D_MODEL), lambda b, i: (0, 0)),
            _resident((None, None, 1, D_MODEL), lambda b, i: (layer, sub, 0, 0)),
            _resident((None, None, 1, D_MODEL), lambda b, i: (layer, sub, 0, 0)),
        ],
        out_specs=pl.BlockSpec((None, tm, D_MODEL), lambda b, i: (b, i, 0)),
        out_shape=jax.ShapeDtypeStruct(x.shape, F32),
        compiler_params=_params(("arbitrary", "arbitrary"), 36),
        name=f"projres_l{layer}",
    )(a, x, mods, w, ln_g, ln_b)


def _attn_kernel(q_ref, k_ref, v_ref, lam_ref, sg_ref, w1_ref, w3_ref, w2_ref,
                 o_ref, w1b_ref, w3b_ref, w2b_ref, vx_ref, *, lam_init, tq, w2_span):
    seq = q_ref.shape[0]
    hd = ATTN_HEAD_DIM
    lam = lam_ref[...]
    lam_full = (jnp.exp(jnp.sum(lam[0:1] * lam[1:2], axis=-1, keepdims=True))
                - jnp.exp(jnp.sum(lam[2:3] * lam[3:4], axis=-1, keepdims=True)) + lam_init)
    lane = lax.broadcasted_iota(jnp.int32, (tq, 2 * hd), 1)
    rq = lax.broadcasted_iota(jnp.int32, (tq, tq), 0) // CHUNK
    ck = lax.broadcasted_iota(jnp.int32, (tq, tq), 1) // CHUNK
    allowed = ck <= rq
    nt = (((1,), (1,)), ((), ()))

    hw = 2 * hd
    vx_ref[:, :hw] = v_ref[...]
    vx_ref[:, hw:] = jnp.ones((seq, hw), BF16)

    def score(i, m):
        kend = (i + 1) * tq
        q = q_ref[i * tq:kend, :].astype(F32) * (hd ** -0.5)
        qm = jnp.where((lane < hd) if m == 0 else (lane >= hd), q, 0.0).astype(BF16)
        s = lax.dot_general(qm, k_ref[0:kend, :], nt, preferred_element_type=F32)
        diag = jnp.where(allowed, s[:, kend - tq:], -jnp.inf)
        return diag if kend == tq else jnp.concatenate([s[:, :kend - tq], diag], axis=1)

    def attend(i, s):
        kend = (i + 1) * tq
        p = jnp.exp(s - jnp.max(s, axis=-1, keepdims=True))
        r = jnp.dot(p.astype(BF16), vx_ref[0:kend, :], preferred_element_type=F32)
        return r[:, :hw] / r[:, hw:]

    def finish(i, o1, o2):
        o = o1 - lam_full * o2
        o = o * lax.rsqrt(jnp.mean(o * o, axis=-1, keepdims=True) + LN_EPS)
        o_ref[i * tq:(i + 1) * tq, :] = (o * sg_ref[...] * (1.0 - lam_init)).astype(o_ref.dtype)

    n = seq // tq
    units = [(i, m) for i in reversed(range(n)) for m in range(2)]
    pend = score(*units[0])
    outs = {}
    for u, (i, m) in enumerate(units):
        cur = pend
        if u + 1 < len(units):
            pend = score(*units[u + 1])
        outs[(i, m)] = attend(i, cur)
        if m == 1:
            finish(i, outs.pop((i, 0)), outs.pop((i, 1)))

    w1b_ref[...] = w1_ref[...].astype(BF16)
    w3b_ref[...] = w3_ref[...].astype(BF16)

    @pl.when((pl.program_id(0) * pl.num_programs(1) + pl.program_id(1)) % w2_span == 0)
    def _():
        w2b_ref[...] = w2_ref[...].astype(BF16)


def _attention(qkv, lam, subln_g, w1, w3, w2, *, lam_init, tq=256):
    bsz, seq, _ = qkv.shape
    hw = 2 * ATTN_HEAD_DIM
    steps = bsz * ATTN_HEADS
    up_rows = D_MODEL // steps
    down_blocks = 16
    down_rows = D_FF // down_blocks
    up_spec = pl.BlockSpec((DEPTH, 2, up_rows, D_FF), lambda b, h: (0, 0, b * ATTN_HEADS + h, 0))
    down_spec = pl.BlockSpec((DEPTH, 2, down_rows, D_MODEL),
                             lambda b, h: (0, 0, (b * ATTN_HEADS + h) // (steps // down_blocks), 0))
    return pl.pallas_call(
        functools.partial(_attn_kernel, lam_init=lam_init, tq=tq, w2_span=steps // down_blocks),
        grid=(bsz, ATTN_HEADS),
        in_specs=[
            pl.BlockSpec((None, seq, hw), lambda b, h: (b, 0, h)),
            pl.BlockSpec((None, seq, hw), lambda b, h: (b, 0, ATTN_HEADS + h)),
            pl.BlockSpec((None, seq, hw), lambda b, h: (b, 0, 2 * ATTN_HEADS + h)),
            _resident((4, ATTN_HEAD_DIM), lambda b, h: (0, 0)),
            _resident((1, hw), lambda b, h: (0, 0)),
            up_spec, up_spec, down_spec,
        ],
        out_specs=[pl.BlockSpec((None, seq, hw), lambda b, h: (b, 0, h)), up_spec, up_spec, down_spec],
        out_shape=[jax.ShapeDtypeStruct((bsz, seq, D_MODEL), BF16), jax.ShapeDtypeStruct(w1.shape, BF16),
                   jax.ShapeDtypeStruct(w3.shape, BF16), jax.ShapeDtypeStruct(w2.shape, BF16)],
        scratch_shapes=[pltpu.VMEM((seq, 2 * hw), BF16)],
        compiler_params=_params(("arbitrary", "arbitrary"), 48),
        name="diff_attention",
    )(qkv, qkv, qkv, lam, subln_g.reshape(1, hw), w1, w3, w2)


ROWS = 2 * CHUNK
PAIRS = SSM_GROUP // 2


def _lane_halves(e, o, lane):
    lo = jnp.where(lane < CHUNK, e, pltpu.roll(o, CHUNK, 1))
    hi = jnp.where(lane < CHUNK, pltpu.roll(e, CHUNK, 1), o)
    return lo, hi


def _s5in_kernel(x_ref, mod_ref, w_ref, o_ref, a_ref, s_ref):
    bsz = x_ref.shape[0]
    shift = mod_ref[3, :, :][:, None, :]
    scale = mod_ref[4, :, :][:, None, :]
    h = (x_ref[...] * (1.0 + scale) + shift).astype(BF16).reshape(bsz * ROWS, D_MODEL)
    a_ref[...] = lax.dot_general(w_ref[...], h, (((1,), (1,)), ((), ())), preferred_element_type=F32)
    lane = lax.broadcasted_iota(jnp.int32, (SSM_GROUPS, LANES), 1)
    half = SSM_GROUPS * bsz
    for q in range(PAIRS):
        for b in range(bsz):
            x0 = a_ref[(2 * q) * SSM_GROUPS:(2 * q + 1) * SSM_GROUPS, b * LANES:(b + 1) * LANES]
            x1 = a_ref[(2 * q + 1) * SSM_GROUPS:(2 * q + 2) * SSM_GROUPS, b * LANES:(b + 1) * LANES]
            c0, c1 = _lane_halves(x0, x1, lane)
            s_ref[q, pl.ds(b, SSM_GROUPS, stride=bsz), :] = c0
            s_ref[q, pl.ds(half + b, SSM_GROUPS, stride=bsz), :] = c1
    for c2 in range(2):
        for q in range(PAIRS):
            o_ref[c2, :, :, q * LANES:(q + 1) * LANES] = (
                s_ref[q, c2 * half:(c2 + 1) * half, :].reshape(SSM_GROUPS, bsz, LANES))


def _s5in(x, mods, w_t, *, layer):
    bsz, seq, _ = x.shape
    width = SSM_GROUP * CHUNK
    return pl.pallas_call(
        _s5in_kernel,
        grid=(seq // ROWS,),
        in_specs=[
            pl.BlockSpec((bsz, ROWS, D_MODEL), lambda j: (0, j, 0)),
            _resident((None, 9, bsz, D_MODEL), lambda j: (layer, 0, 0, 0)),
            _resident((D_MODEL, D_MODEL), lambda j: (0, 0)),
        ],
        out_specs=pl.BlockSpec((2, SSM_GROUPS, bsz, width), lambda j: (j, 0, 0, 0)),
        out_shape=jax.ShapeDtypeStruct((seq // CHUNK, SSM_GROUPS, bsz, width), F32),
        scratch_shapes=[
            pltpu.VMEM((D_MODEL, bsz * ROWS), F32),
            pltpu.VMEM((PAIRS, 2 * SSM_GROUPS * bsz, LANES), F32),
        ],
        compiler_params=_params(("arbitrary",), 40),
        name="s5_in",
    )(x, mods, w_t)


def _s5scan_kernel(u_ref, arr_ref, air_ref, arc_ref, aic_ref, ldt_ref, btr_ref, bti_ref,
                   ctr_ref, cti_ref, d_ref, y_ref, t_ref, ws_ref, wc_ref):
    n_chunks, bsz, width = u_ref.shape
    st = SSM_STATE
    dt = jnp.exp(ldt_ref[...])

    ar, ai = arr_ref[...], air_ref[...]
    mag = jnp.exp(ar * dt)
    abr, abi = mag * jnp.cos(ai * dt), mag * jnp.sin(ai * dt)
    den = ar * ar + ai * ai
    pr, pim = abr - 1.0, abi
    cfr, cfi = (pr * ar + pim * ai) / den, (pim * ar - pr * ai) / den
    btr, bti = btr_ref[...], bti_ref[...]
    bbr, bbi = cfr * btr - cfi * bti, cfr * bti + cfi * btr

    arc, aic = arc_ref[...], aic_ref[...]
    lane = lax.broadcasted_iota(jnp.int32, (1, LANES), 1)
    lag = (lane % CHUNK).astype(F32)
    first = lane < CHUNK
    ctr, cti = ctr_ref[...], cti_ref[...]

    def c_times_power(shift):
        e = lag + shift
        m = jnp.exp(arc * dt * e)
        er, ei = m * jnp.cos(aic * dt * e), m * jnp.sin(aic * dt * e)
        xr, xi = [], []
        for q in range(PAIRS):
            cr = jnp.where(first, ctr[:, 2 * q:2 * q + 1], ctr[:, 2 * q + 1:2 * q + 2])
            ci = jnp.where(first, cti[:, 2 * q:2 * q + 1], cti[:, 2 * q + 1:2 * q + 2])
            xr.append(cr * er - ci * ei)
            xi.append(cr * ei + ci * er)
        return jnp.concatenate(xr, axis=1), jnp.concatenate(xi, axis=1)

    xr, xi = c_times_power(0.0)
    hi = lax.Precision.HIGHEST
    kflat = (jnp.dot(bbr, xr, precision=hi, preferred_element_type=F32)
             - jnp.dot(bbi, xi, precision=hi, preferred_element_type=F32))
    prow = lax.broadcasted_iota(jnp.int32, kflat.shape, 0)
    plane = lax.broadcasted_iota(jnp.int32, kflat.shape, 1)
    kflat = kflat + jnp.where(plane == prow * CHUNK, d_ref[...], 0.0)

    srow = lax.broadcasted_iota(jnp.int32, (CHUNK, LANES), 0)
    keep = (lax.broadcasted_iota(jnp.int32, (CHUNK, LANES), 1) % CHUNK) >= srow
    for p in range(SSM_GROUP):
        rows = jnp.broadcast_to(kflat[p:p + 1, :], (CHUNK, width))
        for q in range(PAIRS):
            blk = pltpu.roll(rows[:, q * LANES:(q + 1) * LANES], 0, 1, stride=1, stride_axis=0)
            t_ref[p * CHUNK:(p + 1) * CHUNK, q * LANES:(q + 1) * LANES] = (
                jnp.where(keep, blk, 0.0).astype(BF16))

    e = (CHUNK - 1 - lax.broadcasted_iota(jnp.int32, (CHUNK, 1), 0)).astype(F32)
    m = jnp.exp(ar * dt * e)
    er, ei = m * jnp.cos(ai * dt * e), m * jnp.sin(ai * dt * e)
    zpad = jnp.zeros((CHUNK, LANES - st), F32)
    for p in range(SSM_GROUP):
        br, bi = bbr[p:p + 1, :], bbi[p:p + 1, :]
        ws_ref[p * CHUNK:(p + 1) * CHUNK, :] = jnp.concatenate(
            [er * br - ei * bi, zpad, er * bi + ei * br, zpad], axis=1).astype(BF16)

    pr_, pi_ = c_times_power(1.0)
    zrows = jnp.zeros((LANES - st, width), F32)
    wc_ref[...] = jnp.concatenate([pr_, zrows, -pi_, zrows], axis=0).astype(BF16)

    u = u_ref[...].reshape(n_chunks * bsz, width).astype(BF16)
    s = jnp.dot(u, ws_ref[...], preferred_element_type=F32)
    m64 = jnp.exp(ar * dt * CHUNK)
    zlane = jnp.zeros((1, LANES - st), F32)
    a64r = jnp.concatenate([m64 * jnp.cos(ai * dt * CHUNK), zlane], axis=1)
    a64i = jnp.concatenate([m64 * jnp.sin(ai * dt * CHUNK), zlane], axis=1)
    hr = jnp.zeros((bsz, LANES), F32)
    hi_ = jnp.zeros((bsz, LANES), F32)
    prev = []
    for c in range(n_chunks):
        prev.append(jnp.concatenate([hr, hi_], axis=1))
        sr, si = s[c * bsz:(c + 1) * bsz, :LANES], s[c * bsz:(c + 1) * bsz, LANES:]
        hr, hi_ = a64r * hr - a64i * hi_ + sr, a64r * hi_ + a64i * hr + si
    hprev = jnp.concatenate(prev, axis=0).astype(BF16)

    y = (jnp.dot(u, t_ref[...], preferred_element_type=F32)
         + jnp.dot(hprev, wc_ref[...], preferred_element_type=F32))
    y_ref[...] = y.reshape(n_chunks, bsz, width)


def _s5scan(u4, a_re, a_im, log_dt, b_re, b_im, c_re, c_im, d):
    n_chunks, groups, bsz, width = u4.shape
    st, pg = SSM_STATE, SSM_GROUP
    per_g = lambda *shape: pl.BlockSpec((None,) + shape, lambda g: (g,) + (0,) * len(shape))
    return pl.pallas_call(
        _s5scan_kernel,
        grid=(groups,),
        in_specs=[
            pl.BlockSpec((n_chunks, None, bsz, width), lambda g: (0, g, 0, 0)),
            per_g(1, st), per_g(1, st), per_g(st, 1), per_g(st, 1), per_g(1, 1),
            per_g(pg, st), per_g(pg, st), per_g(st, pg), per_g(st, pg), per_g(pg, 1),
        ],
        out_specs=pl.BlockSpec((n_chunks, None, bsz, width), lambda g: (0, g, 0, 0)),
        out_shape=jax.ShapeDtypeStruct(u4.shape, F32),
        scratch_shapes=[
            pltpu.VMEM((width, width), BF16),
            pltpu.VMEM((width, 2 * LANES), BF16),
            pltpu.VMEM((2 * LANES, width), BF16),
        ],
        compiler_params=_params(("arbitrary",), 32),
        name="s5_scan",
    )(u4,
      a_re.reshape(groups, 1, st), a_im.reshape(groups, 1, st),
      a_re.reshape(groups, st, 1), a_im.reshape(groups, st, 1),
      log_dt.reshape(groups, 1, 1),
      jnp.swapaxes(b_re, 1, 2), jnp.swapaxes(b_im, 1, 2),
      jnp.swapaxes(c_re, 1, 2), jnp.swapaxes(c_im, 1, 2),
      d.reshape(groups, pg, 1))


def _s5out_kernel(y_ref, x_ref, mod_ref, wg_ref, wo_ref, g_ref, b_ref, o_ref, a_ref, s_ref):
    bsz = x_ref.shape[0]
    half = SSM_GROUPS * bsz
    for c2 in range(2):
        for q in range(PAIRS):
            s_ref[q, c2 * half:(c2 + 1) * half, :] = (
                y_ref[c2, :, :, q * LANES:(q + 1) * LANES].reshape(half, LANES))
    lane = lax.broadcasted_iota(jnp.int32, (SSM_GROUPS, LANES), 1)
    for q in range(PAIRS):
        for b in range(bsz):
            c0 = s_ref[q, pl.ds(b, SSM_GROUPS, stride=bsz), :]
            c1 = s_ref[q, pl.ds(half + b, SSM_GROUPS, stride=bsz), :]
            x0, x1 = _lane_halves(c0, c1, lane)
            a_ref[(2 * q) * SSM_GROUPS:(2 * q + 1) * SSM_GROUPS, b * LANES:(b + 1) * LANES] = x0
            a_ref[(2 * q + 1) * SSM_GROUPS:(2 * q + 2) * SSM_GROUPS, b * LANES:(b + 1) * LANES] = x1
    z = jax.nn.gelu(a_ref[...], approximate=True)
    gt = jnp.dot(wg_ref[...], z.astype(BF16), preferred_element_type=F32)
    zz = (z * jax.nn.sigmoid(gt)).astype(BF16)
    y = lax.dot_general(zz, wo_ref[...], (((0,), (0,)), ((), ())), preferred_element_type=F32)
    y = y.reshape(bsz, ROWS, D_MODEL)
    gate = 1.0 + mod_ref[5, :, :][:, None, :]
    o_ref[...] = _layer_norm(ALPHA * x_ref[...] + gate * y, g_ref[...], b_ref[...])


def _s5out(y4, x, mods, wg_t, wo, ln_g, ln_b, *, layer):
    bsz, seq, _ = x.shape
    width = SSM_GROUP * CHUNK
    return pl.pallas_call(
        _s5out_kernel,
        grid=(seq // ROWS,),
        in_specs=[
            pl.BlockSpec((2, SSM_GROUPS, bsz, width), lambda j: (j, 0, 0, 0)),
            pl.BlockSpec((bsz, ROWS, D_MODEL), lambda j: (0, j, 0)),
            _resident((None, 9, bsz, D_MODEL), lambda j: (layer, 0, 0, 0)),
            _resident((D_MODEL, D_MODEL), lambda j: (0, 0)),
            _resident((D_MODEL, D_MODEL), lambda j: (0, 0)),
            _resident((None, None, 1, D_MODEL), lambda j: (layer, 1, 0, 0)),
            _resident((None, None, 1, D_MODEL), lambda j: (layer, 1, 0, 0)),
        ],
        out_specs=pl.BlockSpec((bsz, ROWS, D_MODEL), lambda j: (0, j, 0)),
        out_shape=jax.ShapeDtypeStruct(x.shape, F32),
        scratch_shapes=[
            pltpu.VMEM((D_MODEL, bsz * ROWS), F32),
            pltpu.VMEM((PAIRS, 2 * SSM_GROUPS * bsz, LANES), F32),
        ],
        compiler_params=_params(("arbitrary",), 48),
        name="s5_out",
    )(y4, x, mods, wg_t, wo, ln_g, ln_b)


def kernel(x, c, ada_w, ada_b, ln_g, ln_b, ffn_w1, ffn_w3, ffn_w2, attn_w_in, attn_lam, attn_subln_g, attn_w_out, ssm_w_in, ssm_a_re, ssm_a_im, ssm_log_dt, ssm_b_re, ssm_b_im, ssm_c_re, ssm_c_im, ssm_d, ssm_w_gate, ssm_w_out):
    mods = _ada(c, ada_w, ada_b)
    lng = ln_g.reshape(DEPTH, 3, 1, D_MODEL)
    lnb = ln_b.reshape(DEPTH, 3, 1, D_MODEL)
    w1, w3, w2 = ffn_w1, ffn_w3, ffn_w2
    gp = (SSM_GROUPS, SSM_GROUP)

    for layer in range(DEPTH):
        i = layer // 2
        x = _ffn(x, mods, w1, w3, w2, lng, lnb, layer=layer, half=0, sub=0)
        if layer % 2 == 0:
            lam_init = 0.8 - 0.6 * math.exp(-0.3 * layer)
            qkv = _modproj(x, mods, attn_w_in[i].astype(BF16), layer=layer, sub=1)
            o, w1, w3, w2 = _attention(qkv, attn_lam[i], attn_subln_g[i], ffn_w1, ffn_w3, ffn_w2,
                                       lam_init=lam_init)
            x = _projres(o, x, mods, attn_w_out[i].astype(BF16), lng, lnb, layer=layer, sub=1)
        else:
            rows_pg = lambda w: w.reshape(*gp, D_MODEL).transpose(1, 0, 2).reshape(D_MODEL, D_MODEL)
            w_in_t = rows_pg(ssm_w_in[i].astype(BF16).T)
            wg_t = rows_pg(rows_pg(ssm_w_gate[i].astype(BF16)).T)
            wo = rows_pg(ssm_w_out[i].astype(BF16))
            u4 = _s5in(x, mods, w_in_t, layer=layer)
            y4 = _s5scan(u4, ssm_a_re[i], ssm_a_im[i], ssm_log_dt[i], ssm_b_re[i], ssm_b_im[i],
                         ssm_c_re[i], ssm_c_im[i], ssm_d[i])
            x = _s5out(y4, x, mods, wg_t, wo, lng, lnb, layer=layer)
        x = _ffn(x, mods, w1, w3, w2, lng, lnb, layer=layer, half=1, sub=2)
    return x
```

```python
import functools
import math

import jax
import jax.numpy as jnp
from jax import lax
from jax.experimental import pallas as pl
from jax.experimental.pallas import tpu as pltpu

D_MODEL = 1024
DEPTH = 2
CHUNK = 64
ATTN_HEADS = 8
ATTN_HEAD_DIM = 64
SSM_GROUP = 16
SSM_GROUPS = D_MODEL // SSM_GROUP
SSM_STATE = 64
D_FF = 2816
ALPHA = (2 * DEPTH) ** 0.25
LN_EPS = 1e-5

LANES = 128
F_CHUNK = 256
MIB = 1024 * 1024

F32 = jnp.float32
BF16 = jnp.bfloat16


def _params(sem, vmem_mib):
    return pltpu.CompilerParams(dimension_semantics=sem, vmem_limit_bytes=vmem_mib * MIB)


def _resident(block_shape, index_map):
    return pl.BlockSpec(block_shape, index_map, pipeline_mode=pl.Buffered(1))


def _layer_norm(r, g, b):
    mu = jnp.mean(r, axis=-1, keepdims=True)
    d = r - mu
    var = jnp.mean(d * d, axis=-1, keepdims=True)
    return d * lax.rsqrt(var + LN_EPS) * g + b


def _mods(mod_ref, sub, bidx):
    shift = mod_ref[3 * sub + 0, pl.ds(bidx, 1), :]
    scale = mod_ref[3 * sub + 1, pl.ds(bidx, 1), :]
    gate = 1.0 + mod_ref[3 * sub + 2, pl.ds(bidx, 1), :]
    return shift, scale, gate


def _ada_kernel(c_ref, w_ref, b_ref, o_ref):
    c = c_ref[...]
    cond = (c * jax.nn.sigmoid(c)).astype(BF16)
    o_ref[...] = jnp.dot(cond, w_ref[...].astype(BF16), preferred_element_type=F32) + b_ref[...]


def _ada(c, ada_w, ada_b):
    bsz = c.shape[0]
    n_blk = ada_w.shape[2] // D_MODEL
    return pl.pallas_call(
        _ada_kernel,
        grid=(DEPTH, n_blk),
        in_specs=[
            pl.BlockSpec((bsz, D_MODEL), lambda l, n: (0, 0)),
            pl.BlockSpec((None, D_MODEL, D_MODEL), lambda l, n: (l, 0, n)),
            pl.BlockSpec((None, None, 1, D_MODEL), lambda l, n: (l, n, 0, 0)),
        ],
        out_specs=pl.BlockSpec((None, None, bsz, D_MODEL), lambda l, n: (l, n, 0, 0)),
        out_shape=jax.ShapeDtypeStruct((DEPTH, n_blk, bsz, D_MODEL), F32),
        compiler_params=_params(("arbitrary", "arbitrary"), 24),
        name="ada_mods",
    )(c, ada_w, ada_b.reshape(DEPTH, n_blk, 1, D_MODEL))


W_ROWS = 128


def _ffn_kernel(x_ref, mod_ref, w1_in, w3_in, w2_in, g_ref, b_ref, o_ref, *scratch, sub, ts, layer, half):
    shift, scale, gate = _mods(mod_ref, sub, pl.program_id(0))
    n_f = D_FF // F_CHUNK
    chunk = lambda f: slice(f * F_CHUNK, (f + 1) * F_CHUNK)
    w1_ref, w3_ref, w2_ref = scratch[:3] if scratch else (w1_in, w3_in, w2_in)

    def stream_weights():
        wide_ref, narrow_ref, sem = scratch[3:]
        chunks = [(src, dst, stage, kind, slice(r, r + W_ROWS))
                  for src, dst, stage, kind in ((w1_in, w1_ref, wide_ref, 0), (w3_in, w3_ref, wide_ref, 0),
                                                (w2_in, w2_ref, narrow_ref, 1))
                  for r in range(0, dst.shape[0], W_ROWS)]

        def copy(i):
            src, _, stage, kind, rows = chunks[i]
            return pltpu.make_async_copy(src.at[layer, half, rows, :], stage.at[i % 2], sem.at[kind, i % 2])

        copy(0).start()
        for i, (_, dst, stage, _, rows) in enumerate(chunks):
            if i + 1 < len(chunks):
                copy(i + 1).start()
            copy(i).wait()
            dst[rows, :] = stage[i % 2].astype(BF16)

    if scratch:
        pl.when(jnp.logical_and(pl.program_id(0) == 0, pl.program_id(1) == 0))(stream_weights)

    def modulated(rows):
        return (x_ref[rows, :] * (1.0 + scale) + shift).astype(BF16)

    def up(h, f):
        return (jnp.dot(h, w1_ref[:, chunk(f)], preferred_element_type=F32),
                jnp.dot(h, w3_ref[:, chunk(f)], preferred_element_type=F32))

    def down(ab, acc, f):
        a, b = ab
        u = (a * jax.nn.sigmoid(a) * b).astype(BF16)
        d = jnp.dot(u, w2_ref[chunk(f), :], preferred_element_type=F32)
        return d if acc is None else acc + d

    def epilogue(rows, acc):
        r = ALPHA * x_ref[rows, :] + gate * (0.5 * acc)
        o_ref[rows, :] = _layer_norm(r, g_ref[...], b_ref[...])

    n_sub = x_ref.shape[0] // ts
    tile = lambda t: slice(t * ts, (t + 1) * ts)
    h = modulated(tile(0))
    nxt = up(h, 0)
    for t in range(n_sub):
        acc = None
        for f in range(n_f):
            cur = nxt
            if f + 1 < n_f:
                nxt = up(h, f + 1)
            elif t + 1 < n_sub:
                nxt = up(h_next, 0)
            if f == n_f // 2 and t + 1 < n_sub:
                h_next = modulated(tile(t + 1))
            acc = down(cur, acc, f)
        epilogue(tile(t), acc)
        if t + 1 < n_sub:
            h = h_next


def _ffn(x, mods, w1, w3, w2, ln_g, ln_b, *, layer, half, sub, tm=2048, ts=256):
    bsz, seq, _ = x.shape
    if w1.dtype == BF16:
        wspec = lambda r, c: _resident((None, None, r, c), lambda b, i: (layer, half, 0, 0))
        w_specs = [wspec(D_MODEL, D_FF), wspec(D_MODEL, D_FF), wspec(D_FF, D_MODEL)]
        scratch = []
    else:
        w_specs = [pl.BlockSpec(memory_space=pl.ANY)] * 3
        scratch = [
            pltpu.VMEM((D_MODEL, D_FF), BF16), pltpu.VMEM((D_MODEL, D_FF), BF16),
            pltpu.VMEM((D_FF, D_MODEL), BF16),
            pltpu.VMEM((2, W_ROWS, D_FF), F32), pltpu.VMEM((2, W_ROWS, D_MODEL), F32),
            pltpu.SemaphoreType.DMA((2, 2)),
        ]
    return pl.pallas_call(
        functools.partial(_ffn_kernel, sub=sub, ts=ts, layer=layer, half=half),
        grid=(bsz, seq // tm),
        in_specs=[
            pl.BlockSpec((None, tm, D_MODEL), lambda b, i: (b, i, 0)),
            _resident((None, 9, bsz, D_MODEL), lambda b, i: (layer, 0, 0, 0)),
            *w_specs,
            _resident((None, None, 1, D_MODEL), lambda b, i: (layer, sub, 0, 0)),
            _resident((None, None, 1, D_MODEL), lambda b, i: (layer, sub, 0, 0)),
        ],
        out_specs=pl.BlockSpec((None, tm, D_MODEL), lambda b, i: (b, i, 0)),
        out_shape=jax.ShapeDtypeStruct(x.shape, F32),
        scratch_shapes=scratch,
        compiler_params=_params(("arbitrary", "arbitrary"), 58),
        name=f"ffn_l{layer}_h{half}",
    )(x, mods, w1, w3, w2, ln_g, ln_b)


def _modproj_kernel(x_ref, mod_ref, w_ref, o_ref, *, sub):
    shift, scale, _ = _mods(mod_ref, sub, pl.program_id(0))
    h = (x_ref[...] * (1.0 + scale) + shift).astype(BF16)
    for n in range(o_ref.shape[1] // D_MODEL):
        cols = slice(n * D_MODEL, (n + 1) * D_MODEL)
        o_ref[:, cols] = jnp.dot(h, w_ref[:, cols], preferred_element_type=F32).astype(o_ref.dtype)


def _modproj(x, mods, w, *, layer, sub, tm=1024):
    bsz, seq, _ = x.shape
    n_out = w.shape[1]
    return pl.pallas_call(
        functools.partial(_modproj_kernel, sub=sub),
        grid=(bsz, seq // tm),
        in_specs=[
            pl.BlockSpec((None, tm, D_MODEL), lambda b, i: (b, i, 0)),
            _resident((None, 9, bsz, D_MODEL), lambda b, i: (layer, 0, 0, 0)),
            _resident((D_MODEL, n_out), lambda b, i: (0, 0)),
        ],
        out_specs=pl.BlockSpec((None, tm, n_out), lambda b, i: (b, i, 0)),
        out_shape=jax.ShapeDtypeStruct((bsz, seq, n_out), BF16),
        compiler_params=_params(("arbitrary", "arbitrary"), 44),
        name=f"modproj_l{layer}",
    )(x, mods, w)


def _projres_kernel(a_ref, x_ref, mod_ref, w_ref, g_ref, b_ref, o_ref, *, sub, ts):
    _, _, gate = _mods(mod_ref, sub, pl.program_id(0))
    n_sub = x_ref.shape[0] // ts
    proj = lambda t: jnp.dot(a_ref[t * ts:(t + 1) * ts, :], w_ref[...], preferred_element_type=F32)
    nxt = proj(0)
    for t in range(n_sub):
        y, rows = nxt, slice(t * ts, (t + 1) * ts)
        if t + 1 < n_sub:
            nxt = proj(t + 1)
        o_ref[rows, :] = _layer_norm(ALPHA * x_ref[rows, :] + gate * y, g_ref[...], b_ref[...])


def _projres(a, x, mods, w, ln_g, ln_b, *, layer, sub, tm=2048, ts=256):
    bsz, seq, _ = x.shape
    return pl.pallas_call(
        functools.partial(_projres_kernel, sub=sub, ts=ts),
        grid=(bsz, seq // tm),
        in_specs=[
            pl.BlockSpec((None, tm, D_MODEL), lambda b, i: (b, i, 0)),
            pl.BlockSpec((None, tm, D_MODEL), lambda b, i: (b, i, 0)),
            _resident((None, 9, bsz, D_MODEL), lambda b, i: (layer, 0, 0, 0)),
            _resident((D_MODEL, D_MODEL), lambda b, i: (0, 0)),
            _resident((None, None, 1, D_MODEL), lambda b, i: (layer, sub, 0, 0)),
            _resident((None, None, 1, D_MODEL), lambda b, i: (layer, sub, 0, 0)),
        ],
        out_specs=pl.BlockSpec((None, tm, D_MODEL), lambda b, i: (b, i, 0)),
        out_shape=jax.ShapeDtypeStruct(x.shape, F32),
        compiler_params=_params(("arbitrary", "arbitrary"), 52),
        name=f"projres_l{layer}",
    )(a, x, mods, w, ln_g, ln_b)


def _attn_kernel(q_ref, k_ref, v_ref, lam_ref, sg_ref, w1_ref, w3_ref, w2_ref,
                 o_ref, w1b_ref, w3b_ref, w2b_ref, vx_ref, *, lam_init, tq, w2_span):
    seq = q_ref.shape[0]
    hd = ATTN_HEAD_DIM
    lam = lam_ref[...]
    lam_full = (jnp.exp(jnp.sum(lam[0:1] * lam[1:2], axis=-1, keepdims=True))
                - jnp.exp(jnp.sum(lam[2:3] * lam[3:4], axis=-1, keepdims=True)) + lam_init)
    lane = lax.broadcasted_iota(jnp.int32, (tq, 2 * hd), 1)
    rq = lax.broadcasted_iota(jnp.int32, (tq, tq), 0) // CHUNK
    ck = lax.broadcasted_iota(jnp.int32, (tq, tq), 1) // CHUNK
    allowed = ck <= rq
    nt = (((1,), (1,)), ((), ()))

    hw = 2 * hd
    vx_ref[:, :hw] = v_ref[...]
    vx_ref[:, hw:] = jnp.ones((seq, hw), BF16)

    def score(i, m):
        kend = (i + 1) * tq
        q = q_ref[i * tq:kend, :].astype(F32) * (hd ** -0.5)
        qm = jnp.where((lane < hd) if m == 0 else (lane >= hd), q, 0.0).astype(BF16)
        s = lax.dot_general(qm, k_ref[0:kend, :], nt, preferred_element_type=F32)
        diag = jnp.where(allowed, s[:, kend - tq:], -jnp.inf)
        return diag if kend == tq else jnp.concatenate([s[:, :kend - tq], diag], axis=1)

    def attend(i, s):
        kend = (i + 1) * tq
        p = jnp.exp(s - jnp.max(s, axis=-1, keepdims=True))
        r = jnp.dot(p.astype(BF16), vx_ref[0:kend, :], preferred_element_type=F32)
        return r[:, :hw] / r[:, hw:]

    def finish(i, o1, o2):
        o = o1 - lam_full * o2
        o = o * lax.rsqrt(jnp.mean(o * o, axis=-1, keepdims=True) + LN_EPS)
        o_ref[i * tq:(i + 1) * tq, :] = (o * sg_ref[...] * (1.0 - lam_init)).astype(o_ref.dtype)

    n = seq // tq
    units = [(i, m) for i in reversed(range(n)) for m in range(2)]
    pend = score(*units[0])
    outs = {}
    for u, (i, m) in enumerate(units):
        cur = pend
        if u + 1 < len(units):
            pend = score(*units[u + 1])
        outs[(i, m)] = attend(i, cur)
        if m == 1:
            finish(i, outs.pop((i, 0)), outs.pop((i, 1)))

    w1b_ref[...] = w1_ref[...].astype(BF16)
    w3b_ref[...] = w3_ref[...].astype(BF16)

    @pl.when((pl.program_id(0) * pl.num_programs(1) + pl.program_id(1)) % w2_span == 0)
    def _():
        w2b_ref[...] = w2_ref[...].astype(BF16)


def _attention(qkv, lam, subln_g, w1, w3, w2, *, lam_init, tq=256):
    bsz, seq, _ = qkv.shape
    hw = 2 * ATTN_HEAD_DIM
    steps = bsz * ATTN_HEADS
    up_rows = D_MODEL // steps
    down_blocks = 16
    down_rows = D_FF // down_blocks
    up_spec = pl.BlockSpec((DEPTH, 2, up_rows, D_FF), lambda b, h: (0, 0, b * ATTN_HEADS + h, 0))
    down_spec = pl.BlockSpec((DEPTH, 2, down_rows, D_MODEL),
                             lambda b, h: (0, 0, (b * ATTN_HEADS + h) // (steps // down_blocks), 0))
    return pl.pallas_call(
        functools.partial(_attn_kernel, lam_init=lam_init, tq=tq, w2_span=steps // down_blocks),
        grid=(bsz, ATTN_HEADS),
        in_specs=[
            pl.BlockSpec((None, seq, hw), lambda b, h: (b, 0, h)),
            pl.BlockSpec((None, seq, hw), lambda b, h: (b, 0, ATTN_HEADS + h)),
            pl.BlockSpec((None, seq, hw), lambda b, h: (b, 0, 2 * ATTN_HEADS + h)),
            _resident((4, ATTN_HEAD_DIM), lambda b, h: (0, 0)),
            _resident((1, hw), lambda b, h: (0, 0)),
            up_spec, up_spec, down_spec,
        ],
        out_specs=[pl.BlockSpec((None, seq, hw), lambda b, h: (b, 0, h)), up_spec, up_spec, down_spec],
        out_shape=[jax.ShapeDtypeStruct((bsz, seq, D_MODEL), BF16), jax.ShapeDtypeStruct(w1.shape, BF16),
                   jax.ShapeDtypeStruct(w3.shape, BF16), jax.ShapeDtypeStruct(w2.shape, BF16)],
        scratch_shapes=[pltpu.VMEM((seq, 2 * hw), BF16)],
        compiler_params=_params(("arbitrary", "arbitrary"), 48),
        name="diff_attention",
    )(qkv, qkv, qkv, lam, subln_g.reshape(1, hw), w1, w3, w2)


ROWS = 2 * CHUNK
PAIRS = SSM_GROUP // 2


def _lane_halves(e, o, lane):
    lo = jnp.where(lane < CHUNK, e, pltpu.roll(o, CHUNK, 1))
    hi = jnp.where(lane < CHUNK, pltpu.roll(e, CHUNK, 1), o)
    return lo, hi


def _s5in_kernel(x_ref, mod_ref, w_ref, o_ref, a_ref, s_ref):
    bsz = x_ref.shape[0]
    shift = mod_ref[3, :, :][:, None, :]
    scale = mod_ref[4, :, :][:, None, :]
    h = (x_ref[...] * (1.0 + scale) + shift).astype(BF16).reshape(bsz * ROWS, D_MODEL)
    a_ref[...] = lax.dot_general(w_ref[...], h, (((1,), (1,)), ((), ())), preferred_element_type=F32)
    lane = lax.broadcasted_iota(jnp.int32, (SSM_GROUPS, LANES), 1)
    half = SSM_GROUPS * bsz
    for q in range(PAIRS):
        for b in range(bsz):
            x0 = a_ref[(2 * q) * SSM_GROUPS:(2 * q + 1) * SSM_GROUPS, b * LANES:(b + 1) * LANES]
            x1 = a_ref[(2 * q + 1) * SSM_GROUPS:(2 * q + 2) * SSM_GROUPS, b * LANES:(b + 1) * LANES]
            c0, c1 = _lane_halves(x0, x1, lane)
            s_ref[q, pl.ds(b, SSM_GROUPS, stride=bsz), :] = c0
            s_ref[q, pl.ds(half + b, SSM_GROUPS, stride=bsz), :] = c1
    for c2 in range(2):
        for q in range(PAIRS):
            o_ref[c2, :, :, q * LANES:(q + 1) * LANES] = (
                s_ref[q, c2 * half:(c2 + 1) * half, :].reshape(SSM_GROUPS, bsz, LANES))


def _s5in(x, mods, w_t, *, layer):
    bsz, seq, _ = x.shape
    width = SSM_GROUP * CHUNK
    return pl.pallas_call(
        _s5in_kernel,
        grid=(seq // ROWS,),
        in_specs=[
            pl.BlockSpec((bsz, ROWS, D_MODEL), lambda j: (0, j, 0)),
            _resident((None, 9, bsz, D_MODEL), lambda j: (layer, 0, 0, 0)),
            _resident((D_MODEL, D_MODEL), lambda j: (0, 0)),
        ],
        out_specs=pl.BlockSpec((2, SSM_GROUPS, bsz, width), lambda j: (j, 0, 0, 0)),
        out_shape=jax.ShapeDtypeStruct((seq // CHUNK, SSM_GROUPS, bsz, width), F32),
        scratch_shapes=[
            pltpu.VMEM((D_MODEL, bsz * ROWS), F32),
            pltpu.VMEM((PAIRS, 2 * SSM_GROUPS * bsz, LANES), F32),
        ],
        compiler_params=_params(("arbitrary",), 40),
        name="s5_in",
    )(x, mods, w_t)


def _s5scan_kernel(u_ref, arr_ref, air_ref, arc_ref, aic_ref, ldt_ref, btr_ref, bti_ref,
                   ctr_ref, cti_ref, d_ref, y_ref, t_ref, ws_ref, wc_ref):
    n_chunks, bsz, width = u_ref.shape
    st = SSM_STATE
    dt = jnp.exp(ldt_ref[...])

    ar, ai = arr_ref[...], air_ref[...]
    mag = jnp.exp(ar * dt)
    abr, abi = mag * jnp.cos(ai * dt), mag * jnp.sin(ai * dt)
    den = ar * ar + ai * ai
    pr, pim = abr - 1.0, abi
    cfr, cfi = (pr * ar + pim * ai) / den, (pim * ar - pr * ai) / den
    btr, bti = btr_ref[...], bti_ref[...]
    bbr, bbi = cfr * btr - cfi * bti, cfr * bti + cfi * btr

    arc, aic = arc_ref[...], aic_ref[...]
    lane = lax.broadcasted_iota(jnp.int32, (1, LANES), 1)
    lag = (lane % CHUNK).astype(F32)
    first = lane < CHUNK
    ctr, cti = ctr_ref[...], cti_ref[...]

    def c_times_power(shift):
        e = lag + shift
        m = jnp.exp(arc * dt * e)
        er, ei = m * jnp.cos(aic * dt * e), m * jnp.sin(aic * dt * e)
        xr, xi = [], []
        for q in range(PAIRS):
            cr = jnp.where(first, ctr[:, 2 * q:2 * q + 1], ctr[:, 2 * q + 1:2 * q + 2])
            ci = jnp.where(first, cti[:, 2 * q:2 * q + 1], cti[:, 2 * q + 1:2 * q + 2])
            xr.append(cr * er - ci * ei)
            xi.append(cr * ei + ci * er)
        return jnp.concatenate(xr, axis=1), jnp.concatenate(xi, axis=1)

    xr, xi = c_times_power(0.0)
    hi = lax.Precision.HIGHEST
    kflat = (jnp.dot(bbr, xr, precision=hi, preferred_element_type=F32)
             - jnp.dot(bbi, xi, precision=hi, preferred_element_type=F32))
    prow = lax.broadcasted_iota(jnp.int32, kflat.shape, 0)
    plane = lax.broadcasted_iota(jnp.int32, kflat.shape, 1)
    kflat = kflat + jnp.where(plane == prow * CHUNK, d_ref[...], 0.0)

    srow = lax.broadcasted_iota(jnp.int32, (CHUNK, LANES), 0)
    keep = (lax.broadcasted_iota(jnp.int32, (CHUNK, LANES), 1) % CHUNK) >= srow
    for p in range(SSM_GROUP):
        rows = jnp.broadcast_to(kflat[p:p + 1, :], (CHUNK, width))
        for q in range(PAIRS):
            blk = pltpu.roll(rows[:, q * LANES:(q + 1) * LANES], 0, 1, stride=1, stride_axis=0)
            t_ref[p * CHUNK:(p + 1) * CHUNK, q * LANES:(q + 1) * LANES] = (
                jnp.where(keep, blk, 0.0).astype(BF16))

    e = (CHUNK - 1 - lax.broadcasted_iota(jnp.int32, (CHUNK, 1), 0)).astype(F32)
    m = jnp.exp(ar * dt * e)
    er, ei = m * jnp.cos(ai * dt * e), m * jnp.sin(ai * dt * e)
    zpad = jnp.zeros((CHUNK, LANES - st), F32)
    for p in range(SSM_GROUP):
        br, bi = bbr[p:p + 1, :], bbi[p:p + 1, :]
        ws_ref[p * CHUNK:(p + 1) * CHUNK, :] = jnp.concatenate(
            [er * br - ei * bi, zpad, er * bi + ei * br, zpad], axis=1).astype(BF16)

    pr_, pi_ = c_times_power(1.0)
    zrows = jnp.zeros((LANES - st, width), F32)
    wc_ref[...] = jnp.concatenate([pr_, zrows, -pi_, zrows], axis=0).astype(BF16)

    u = u_ref[...].reshape(n_chunks * bsz, width).astype(BF16)
    s = jnp.dot(u, ws_ref[...], preferred_element_type=F32)
    m64 = jnp.exp(ar * dt * CHUNK)
    zlane = jnp.zeros((1, LANES - st), F32)
    a64r = jnp.concatenate([m64 * jnp.cos(ai * dt * CHUNK), zlane], axis=1)
    a64i = jnp.concatenate([m64 * jnp.sin(ai * dt * CHUNK), zlane], axis=1)
    hr = jnp.zeros((bsz, LANES), F32)
    hi_ = jnp.zeros((bsz, LANES), F32)
    prev = []
    for c in range(n_chunks):
        prev.append(jnp.concatenate([hr, hi_], axis=1))
        sr, si = s[c * bsz:(c + 1) * bsz, :LANES], s[c * bsz:(c + 1) * bsz, LANES:]
        hr, hi_ = a64r * hr - a64i * hi_ + sr, a64r * hi_ + a64i * hr + si
    hprev = jnp.concatenate(prev, axis=0).astype(BF16)

    y = (jnp.dot(u, t_ref[...], preferred_element_type=F32)
         + jnp.dot(hprev, wc_ref[...], preferred_element_type=F32))
    y_ref[...] = y.reshape(n_chunks, bsz, width)


def _s5scan(u4, a_re, a_im, log_dt, b_re, b_im, c_re, c_im, d):
    n_chunks, groups, bsz, width = u4.shape
    st, pg = SSM_STATE, SSM_GROUP
    per_g = lambda *shape: pl.BlockSpec((None,) + shape, lambda g: (g,) + (0,) * len(shape))
    return pl.pallas_call(
        _s5scan_kernel,
        grid=(groups,),
        in_specs=[
            pl.BlockSpec((n_chunks, None, bsz, width), lambda g: (0, g, 0, 0)),
            per_g(1, st), per_g(1, st), per_g(st, 1), per_g(st, 1), per_g(1, 1),
            per_g(pg, st), per_g(pg, st), per_g(st, pg), per_g(st, pg), per_g(pg, 1),
        ],
        out_specs=pl.BlockSpec((n_chunks, None, bsz, width), lambda g: (0, g, 0, 0)),
        out_shape=jax.ShapeDtypeStruct(u4.shape, F32),
        scratch_shapes=[
            pltpu.VMEM((width, width), BF16),
            pltpu.VMEM((width, 2 * LANES), BF16),
            pltpu.VMEM((2 * LANES, width), BF16),
        ],
        compiler_params=_params(("arbitrary",), 32),
        name="s5_scan",
    )(u4,
      a_re.reshape(groups, 1, st), a_im.reshape(groups, 1, st),
      a_re.reshape(groups, st, 1), a_im.reshape(groups, st, 1),
      log_dt.reshape(groups, 1, 1),
      jnp.swapaxes(b_re, 1, 2), jnp.swapaxes(b_im, 1, 2),
      jnp.swapaxes(c_re, 1, 2), jnp.swapaxes(c_im, 1, 2),
      d.reshape(groups, pg, 1))


def _s5out_kernel(y_ref, x_ref, mod_ref, wg_ref, wo_ref, g_ref, b_ref, o_ref, a_ref, s_ref):
    bsz = x_ref.shape[0]
    half = SSM_GROUPS * bsz
    for c2 in range(2):
        for q in range(PAIRS):
            s_ref[q, c2 * half:(c2 + 1) * half, :] = (
                y_ref[c2, :, :, q * LANES:(q + 1) * LANES].reshape(half, LANES))
    lane = lax.broadcasted_iota(jnp.int32, (SSM_GROUPS, LANES), 1)
    n_grp = 4
    per = bsz // n_grp

    def gather(k):
        for q in range(PAIRS):
            for b in range(k * per, (k + 1) * per):
                c0 = s_ref[q, pl.ds(b, SSM_GROUPS, stride=bsz), :]
                c1 = s_ref[q, pl.ds(half + b, SSM_GROUPS, stride=bsz), :]
                x0, x1 = _lane_halves(c0, c1, lane)
                a_ref[(2 * q) * SSM_GROUPS:(2 * q + 1) * SSM_GROUPS, b * LANES:(b + 1) * LANES] = x0
                a_ref[(2 * q + 1) * SSM_GROUPS:(2 * q + 2) * SSM_GROUPS, b * LANES:(b + 1) * LANES] = x1
        return jax.nn.gelu(a_ref[:, k * per * LANES:(k + 1) * per * LANES], approximate=True)

    def gate_mm(z):
        return jnp.dot(wg_ref[...], z.astype(BF16), preferred_element_type=F32)

    def out_mm(z, gt):
        zz = (z * jax.nn.sigmoid(gt)).astype(BF16)
        return lax.dot_general(zz, wo_ref[...], (((0,), (0,)), ((), ())), preferred_element_type=F32)

    def finish(k, y):
        rows = slice(k * per, (k + 1) * per)
        y = y.reshape(per, ROWS, D_MODEL)
        gate = 1.0 + mod_ref[5, rows, :][:, None, :]
        o_ref[rows] = _layer_norm(ALPHA * x_ref[rows] + gate * y, g_ref[...], b_ref[...])

    zs, gs, ys = {}, {}, {}
    for step in range(n_grp + 3):
        if step < n_grp:
            zs[step] = gather(step)
        if 0 <= step - 1 < n_grp:
            gs[step - 1] = gate_mm(zs[step - 1])
        if 0 <= step - 2 < n_grp:
            ys[step - 2] = out_mm(zs.pop(step - 2), gs.pop(step - 2))
        if 0 <= step - 3 < n_grp:
            finish(step - 3, ys.pop(step - 3))


def _s5out(y4, x, mods, wg_t, wo, ln_g, ln_b, *, layer):
    bsz, seq, _ = x.shape
    width = SSM_GROUP * CHUNK
    return pl.pallas_call(
        _s5out_kernel,
        grid=(seq // ROWS,),
        in_specs=[
            pl.BlockSpec((2, SSM_GROUPS, bsz, width), lambda j: (j, 0, 0, 0)),
            pl.BlockSpec((bsz, ROWS, D_MODEL), lambda j: (0, j, 0)),
            _resident((None, 9, bsz, D_MODEL), lambda j: (layer, 0, 0, 0)),
            _resident((D_MODEL, D_MODEL), lambda j: (0, 0)),
            _resident((D_MODEL, D_MODEL), lambda j: (0, 0)),
            _resident((None, None, 1, D_MODEL), lambda j: (layer, 1, 0, 0)),
            _resident((None, None, 1, D_MODEL), lambda j: (layer, 1, 0, 0)),
        ],
        out_specs=pl.BlockSpec((bsz, ROWS, D_MODEL), lambda j: (0, j, 0)),
        out_shape=jax.ShapeDtypeStruct(x.shape, F32),
        scratch_shapes=[
            pltpu.VMEM((D_MODEL, bsz * ROWS), F32),
            pltpu.VMEM((PAIRS, 2 * SSM_GROUPS * bsz, LANES), F32),
        ],
        compiler_params=_params(("arbitrary",), 48),
        name="s5_out",
    )(y4, x, mods, wg_t, wo, ln_g, ln_b)


def kernel(x, c, ada_w, ada_b, ln_g, ln_b, ffn_w1, ffn_w3, ffn_w2, attn_w_in, attn_lam, attn_subln_g, attn_w_out, ssm_w_in, ssm_a_re, ssm_a_im, ssm_log_dt, ssm_b_re, ssm_b_im, ssm_c_re, ssm_c_im, ssm_d, ssm_w_gate, ssm_w_out):
    mods = _ada(c, ada_w, ada_b)
    lng = ln_g.reshape(DEPTH, 3, 1, D_MODEL)
    lnb = ln_b.reshape(DEPTH, 3, 1, D_MODEL)
    w1, w3, w2 = ffn_w1, ffn_w3, ffn_w2
    gp = (SSM_GROUPS, SSM_GROUP)

    for layer in range(DEPTH):
        i = layer // 2
        x = _ffn(x, mods, w1, w3, w2, lng, lnb, layer=layer, half=0, sub=0)
        if layer % 2 == 0:
            lam_init = 0.8 - 0.6 * math.exp(-0.3 * layer)
            qkv = _modproj(x, mods, attn_w_in[i].astype(BF16), layer=layer, sub=1)
            o, w1, w3, w2 = _attention(qkv, attn_lam[i], attn_subln_g[i], ffn_w1, ffn_w3, ffn_w2,
                                       lam_init=lam_init)
            x = _projres(o, x, mods, attn_w_out[i].astype(BF16), lng, lnb, layer=layer, sub=1)
        else:
            rows_pg = lambda w: w.reshape(*gp, D_MODEL).transpose(1, 0, 2).reshape(D_MODEL, D_MODEL)
            w_in_t = rows_pg(ssm_w_in[i].astype(BF16).T)
            wg_t = rows_pg(rows_pg(ssm_w_gate[i].astype(BF16)).T)
            wo = rows_pg(ssm_w_out[i].astype(BF16))
            u4 = _s5in(x, mods, w_in_t, layer=layer)
            y4 = _s5scan(u4, ssm_a_re[i], ssm_a_im[i], ssm_log_dt[i], ssm_b_re[i], ssm_b_im[i],
                         ssm_c_re[i], ssm_c_im[i], ssm_d[i])
            x = _s5out(y4, x, mods, wg_t, wo, lng, lnb, layer=layer)
        x = _ffn(x, mods, w1, w3, w2, lng, lnb, layer=layer, half=1, sub=2)
    return x
```

```python
import functools
import math

import jax
import jax.numpy as jnp
from jax import lax
from jax.experimental import pallas as pl
from jax.experimental.pallas import tpu as pltpu

D_MODEL = 1024
DEPTH = 2
CHUNK = 64
ATTN_HEADS = 8
ATTN_HEAD_DIM = 64
SSM_GROUP = 16
SSM_GROUPS = D_MODEL // SSM_GROUP
SSM_STATE = 64
D_FF = 2816
ALPHA = (2 * DEPTH) ** 0.25
LN_EPS = 1e-5

LANES = 128
F_CHUNK = 256
MIB = 1024 * 1024

F32 = jnp.float32
BF16 = jnp.bfloat16


def _params(sem, vmem_mib):
    return pltpu.CompilerParams(dimension_semantics=sem, vmem_limit_bytes=vmem_mib * MIB)


def _resident(block_shape, index_map):
    return pl.BlockSpec(block_shape, index_map, pipeline_mode=pl.Buffered(1))


def _layer_norm(r, g, b):
    mu = jnp.mean(r, axis=-1, keepdims=True)
    d = r - mu
    var = jnp.mean(d * d, axis=-1, keepdims=True)
    return d * lax.rsqrt(var + LN_EPS) * g + b


def _mods(mod_ref, sub, bidx):
    shift = mod_ref[3 * sub + 0, pl.ds(bidx, 1), :]
    scale = mod_ref[3 * sub + 1, pl.ds(bidx, 1), :]
    gate = 1.0 + mod_ref[3 * sub + 2, pl.ds(bidx, 1), :]
    return shift, scale, gate


def _ada_kernel(c_ref, w_ref, b_ref, o_ref):
    c = c_ref[...]
    cond = (c * jax.nn.sigmoid(c)).astype(BF16)
    o_ref[...] = jnp.dot(cond, w_ref[...].astype(BF16), preferred_element_type=F32) + b_ref[...]


def _ada(c, ada_w, ada_b):
    bsz = c.shape[0]
    n_blk = ada_w.shape[2] // D_MODEL
    return pl.pallas_call(
        _ada_kernel,
        grid=(DEPTH, n_blk),
        in_specs=[
            pl.BlockSpec((bsz, D_MODEL), lambda l, n: (0, 0)),
            pl.BlockSpec((None, D_MODEL, D_MODEL), lambda l, n: (l, 0, n)),
            pl.BlockSpec((None, None, 1, D_MODEL), lambda l, n: (l, n, 0, 0)),
        ],
        out_specs=pl.BlockSpec((None, None, bsz, D_MODEL), lambda l, n: (l, n, 0, 0)),
        out_shape=jax.ShapeDtypeStruct((DEPTH, n_blk, bsz, D_MODEL), F32),
        compiler_params=_params(("arbitrary", "arbitrary"), 24),
        name="ada_mods",
    )(c, ada_w, ada_b.reshape(DEPTH, n_blk, 1, D_MODEL))


W_ROWS = 256


def _ffn_kernel(x_ref, mod_ref, w1_in, w3_in, w2_in, g_ref, b_ref, o_ref, *scratch, sub, ts, layer, half):
    shift, scale, gate = _mods(mod_ref, sub, pl.program_id(0))
    n_f = D_FF // F_CHUNK
    chunk = lambda f: slice(f * F_CHUNK, (f + 1) * F_CHUNK)
    w1_ref, w3_ref, w2_ref = scratch[:3] if scratch else (w1_in, w3_in, w2_in)

    def stream_weights():
        wide_ref, narrow_ref, sem = scratch[3:]
        chunks = [(src, dst, stage, kind, slice(r, r + W_ROWS))
                  for src, dst, stage, kind in ((w1_in, w1_ref, wide_ref, 0), (w3_in, w3_ref, wide_ref, 0),
                                                (w2_in, w2_ref, narrow_ref, 1))
                  for r in range(0, dst.shape[0], W_ROWS)]

        def copy(i):
            src, _, stage, kind, rows = chunks[i]
            return pltpu.make_async_copy(src.at[layer, half, rows, :], stage.at[i % 2], sem.at[kind, i % 2])

        copy(0).start()
        for i, (_, dst, stage, _, rows) in enumerate(chunks):
            if i + 1 < len(chunks):
                copy(i + 1).start()
            copy(i).wait()
            dst[rows, :] = stage[i % 2].astype(BF16)

    if scratch:
        pl.when(jnp.logical_and(pl.program_id(0) == 0, pl.program_id(1) == 0))(stream_weights)

    def modulated(rows):
        return (x_ref[rows, :] * (1.0 + scale) + shift).astype(BF16)

    def up(h, f):
        return (jnp.dot(h, w1_ref[:, chunk(f)], preferred_element_type=F32),
                jnp.dot(h, w3_ref[:, chunk(f)], preferred_element_type=F32))

    def down(ab, acc, f):
        a, b = ab
        u = (a * jax.nn.sigmoid(a) * b).astype(BF16)
        d = jnp.dot(u, w2_ref[chunk(f), :], preferred_element_type=F32)
        return d if acc is None else acc + d

    def epilogue(rows, acc):
        r = ALPHA * x_ref[rows, :] + gate * (0.5 * acc)
        o_ref[rows, :] = _layer_norm(r, g_ref[...], b_ref[...])

    n_sub = x_ref.shape[0] // ts
    tile = lambda t: slice(t * ts, (t + 1) * ts)
    h = modulated(tile(0))
    nxt = up(h, 0)
    for t in range(n_sub):
        acc = None
        for f in range(n_f):
            cur = nxt
            if f + 1 < n_f:
                nxt = up(h, f + 1)
            elif t + 1 < n_sub:
                nxt = up(h_next, 0)
            if f == n_f // 2 and t + 1 < n_sub:
                h_next = modulated(tile(t + 1))
            acc = down(cur, acc, f)
        epilogue(tile(t), acc)
        if t + 1 < n_sub:
            h = h_next


def _ffn(x, mods, w1, w3, w2, ln_g, ln_b, *, layer, half, sub, tm=1024, ts=256):
    bsz, seq, _ = x.shape
    if w1.dtype == BF16:
        wspec = lambda r, c: _resident((None, None, r, c), lambda b, i: (layer, half, 0, 0))
        w_specs = [wspec(D_MODEL, D_FF), wspec(D_MODEL, D_FF), wspec(D_FF, D_MODEL)]
        scratch = []
    else:
        w_specs = [pl.BlockSpec(memory_space=pl.ANY)] * 3
        scratch = [
            pltpu.VMEM((D_MODEL, D_FF), BF16), pltpu.VMEM((D_MODEL, D_FF), BF16),
            pltpu.VMEM((D_FF, D_MODEL), BF16),
            pltpu.VMEM((2, W_ROWS, D_FF), F32), pltpu.VMEM((2, W_ROWS, D_MODEL), F32),
            pltpu.SemaphoreType.DMA((2, 2)),
        ]
    return pl.pallas_call(
        functools.partial(_ffn_kernel, sub=sub, ts=ts, layer=layer, half=half),
        grid=(bsz, seq // tm),
        in_specs=[
            pl.BlockSpec((None, tm, D_MODEL), lambda b, i: (b, i, 0)),
            _resident((None, 9, bsz, D_MODEL), lambda b, i: (layer, 0, 0, 0)),
            *w_specs,
            _resident((None, None, 1, D_MODEL), lambda b, i: (layer, sub, 0, 0)),
            _resident((None, None, 1, D_MODEL), lambda b, i: (layer, sub, 0, 0)),
        ],
        out_specs=pl.BlockSpec((None, tm, D_MODEL), lambda b, i: (b, i, 0)),
        out_shape=jax.ShapeDtypeStruct(x.shape, F32),
        scratch_shapes=scratch,
        compiler_params=_params(("arbitrary", "arbitrary"), 52),
        name=f"ffn_l{layer}_h{half}",
    )(x, mods, w1, w3, w2, ln_g, ln_b)


def _modproj_kernel(x_ref, mod_ref, w_ref, w1_ref, w3_ref, w2_ref, o_ref, w1b_ref, w3b_ref, w2b_ref,
                    *, sub, w2_span):
    shift, scale, _ = _mods(mod_ref, sub, pl.program_id(0))
    h = (x_ref[...] * (1.0 + scale) + shift).astype(BF16)
    for n in range(o_ref.shape[1] // D_MODEL):
        cols = slice(n * D_MODEL, (n + 1) * D_MODEL)
        o_ref[:, cols] = jnp.dot(h, w_ref[:, cols], preferred_element_type=F32).astype(o_ref.dtype)
    w1b_ref[...] = w1_ref[...].astype(BF16)
    w3b_ref[...] = w3_ref[...].astype(BF16)

    @pl.when((pl.program_id(0) * pl.num_programs(1) + pl.program_id(1)) % w2_span == 0)
    def _():
        w2b_ref[...] = w2_ref[...].astype(BF16)


def _modproj(x, mods, w, w1, w3, w2, *, layer, sub, tm=512):
    bsz, seq, _ = x.shape
    n_out = w.shape[1]
    per_seq = seq // tm
    steps = bsz * per_seq
    up_rows = D_MODEL // steps
    down_blocks = 16
    down_rows = D_FF // down_blocks
    up_spec = pl.BlockSpec((DEPTH, 2, up_rows, D_FF), lambda b, i: (0, 0, b * per_seq + i, 0))
    down_spec = pl.BlockSpec((DEPTH, 2, down_rows, D_MODEL),
                             lambda b, i: (0, 0, (b * per_seq + i) // (steps // down_blocks), 0))
    return pl.pallas_call(
        functools.partial(_modproj_kernel, sub=sub, w2_span=steps // down_blocks),
        grid=(bsz, per_seq),
        in_specs=[
            pl.BlockSpec((None, tm, D_MODEL), lambda b, i: (b, i, 0)),
            _resident((None, 9, bsz, D_MODEL), lambda b, i: (layer, 0, 0, 0)),
            _resident((D_MODEL, n_out), lambda b, i: (0, 0)),
            up_spec, up_spec, down_spec,
        ],
        out_specs=[pl.BlockSpec((None, tm, n_out), lambda b, i: (b, i, 0)), up_spec, up_spec, down_spec],
        out_shape=[jax.ShapeDtypeStruct((bsz, seq, n_out), BF16), jax.ShapeDtypeStruct(w1.shape, BF16),
                   jax.ShapeDtypeStruct(w3.shape, BF16), jax.ShapeDtypeStruct(w2.shape, BF16)],
        compiler_params=_params(("arbitrary", "arbitrary"), 44),
        name=f"modproj_l{layer}",
    )(x, mods, w, w1, w3, w2)


def _projres_kernel(a_ref, x_ref, mod_ref, w_ref, g_ref, b_ref, o_ref, *, sub, ts):
    _, _, gate = _mods(mod_ref, sub, pl.program_id(0))
    n_sub = x_ref.shape[0] // ts
    proj = lambda t: jnp.dot(a_ref[t * ts:(t + 1) * ts, :], w_ref[...], preferred_element_type=F32)
    nxt = proj(0)
    for t in range(n_sub):
        y, rows = nxt, slice(t * ts, (t + 1) * ts)
        if t + 1 < n_sub:
            nxt = proj(t + 1)
        o_ref[rows, :] = _layer_norm(ALPHA * x_ref[rows, :] + gate * y, g_ref[...], b_ref[...])


def _projres(a, x, mods, w, ln_g, ln_b, *, layer, sub, tm=1024, ts=256):
    bsz, seq, _ = x.shape
    return pl.pallas_call(
        functools.partial(_projres_kernel, sub=sub, ts=ts),
        grid=(bsz, seq // tm),
        in_specs=[
            pl.BlockSpec((None, tm, D_MODEL), lambda b, i: (b, i, 0)),
            pl.BlockSpec((None, tm, D_MODEL), lambda b, i: (b, i, 0)),
            _resident((None, 9, bsz, D_MODEL), lambda b, i: (layer, 0, 0, 0)),
            _resident((D_MODEL, D_MODEL), lambda b, i: (0, 0)),
            _resident((None, None, 1, D_MODEL), lambda b, i: (layer, sub, 0, 0)),
            _resident((None, None, 1, D_MODEL), lambda b, i: (layer, sub, 0, 0)),
        ],
        out_specs=pl.BlockSpec((None, tm, D_MODEL), lambda b, i: (b, i, 0)),
        out_shape=jax.ShapeDtypeStruct(x.shape, F32),
        compiler_params=_params(("arbitrary", "arbitrary"), 36),
        name=f"projres_l{layer}",
    )(a, x, mods, w, ln_g, ln_b)


def _attn_kernel(q_ref, k_ref, v_ref, lam_ref, sg_ref, o_ref, vx_ref, *, lam_init, tq):
    seq = q_ref.shape[0]
    hd = ATTN_HEAD_DIM
    lam = lam_ref[...]
    lam_full = (jnp.exp(jnp.sum(lam[0:1] * lam[1:2], axis=-1, keepdims=True))
                - jnp.exp(jnp.sum(lam[2:3] * lam[3:4], axis=-1, keepdims=True)) + lam_init)
    lane = lax.broadcasted_iota(jnp.int32, (tq, 2 * hd), 1)
    rq = lax.broadcasted_iota(jnp.int32, (tq, tq), 0) // CHUNK
    ck = lax.broadcasted_iota(jnp.int32, (tq, tq), 1) // CHUNK
    allowed = ck <= rq
    nt = (((1,), (1,)), ((), ()))

    hw = 2 * hd
    vx_ref[:, :hw] = v_ref[...]
    vx_ref[:, hw:] = jnp.ones((seq, hw), BF16)

    def score(i, m):
        kend = (i + 1) * tq
        q = q_ref[i * tq:kend, :].astype(F32) * (hd ** -0.5)
        qm = jnp.where((lane < hd) if m == 0 else (lane >= hd), q, 0.0).astype(BF16)
        s = lax.dot_general(qm, k_ref[0:kend, :], nt, preferred_element_type=F32)
        diag = jnp.where(allowed, s[:, kend - tq:], -jnp.inf)
        return diag if kend == tq else jnp.concatenate([s[:, :kend - tq], diag], axis=1)

    def attend(i, s):
        kend = (i + 1) * tq
        p = jnp.exp(s - jnp.max(s, axis=-1, keepdims=True))
        r = jnp.dot(p.astype(BF16), vx_ref[0:kend, :], preferred_element_type=F32)
        return r[:, :hw] / r[:, hw:]

    def finish(i, o1, o2):
        o = o1 - lam_full * o2
        o = o * lax.rsqrt(jnp.mean(o * o, axis=-1, keepdims=True) + LN_EPS)
        o_ref[i * tq:(i + 1) * tq, :] = (o * sg_ref[...] * (1.0 - lam_init)).astype(o_ref.dtype)

    n = seq // tq
    units = [(i, m) for i in reversed(range(n)) for m in range(2)]
    pend = score(*units[0])
    outs = {}
    for u, (i, m) in enumerate(units):
        cur = pend
        if u + 1 < len(units):
            pend = score(*units[u + 1])
        outs[(i, m)] = attend(i, cur)
        if m == 1:
            finish(i, outs.pop((i, 0)), outs.pop((i, 1)))


def _attention(qkv, lam, subln_g, *, lam_init, tq=256):
    bsz, seq, _ = qkv.shape
    hw = 2 * ATTN_HEAD_DIM
    return pl.pallas_call(
        functools.partial(_attn_kernel, lam_init=lam_init, tq=tq),
        grid=(bsz, ATTN_HEADS),
        in_specs=[
            pl.BlockSpec((None, seq, hw), lambda b, h: (b, 0, h)),
            pl.BlockSpec((None, seq, hw), lambda b, h: (b, 0, ATTN_HEADS + h)),
            pl.BlockSpec((None, seq, hw), lambda b, h: (b, 0, 2 * ATTN_HEADS + h)),
            _resident((4, ATTN_HEAD_DIM), lambda b, h: (0, 0)),
            _resident((1, hw), lambda b, h: (0, 0)),
        ],
        out_specs=pl.BlockSpec((None, seq, hw), lambda b, h: (b, 0, h)),
        out_shape=jax.ShapeDtypeStruct((bsz, seq, D_MODEL), BF16),
        scratch_shapes=[pltpu.VMEM((seq, 2 * hw), BF16)],
        compiler_params=_params(("arbitrary", "arbitrary"), 40),
        name="diff_attention",
    )(qkv, qkv, qkv, lam, subln_g.reshape(1, hw))


ROWS = 2 * CHUNK
PAIRS = SSM_GROUP // 2


def _lane_halves(e, o, lane):
    lo = jnp.where(lane < CHUNK, e, pltpu.roll(o, CHUNK, 1))
    hi = jnp.where(lane < CHUNK, pltpu.roll(e, CHUNK, 1), o)
    return lo, hi


def _s5in_kernel(x_ref, mod_ref, w_ref, o_ref, a_ref, s_ref):
    bsz = x_ref.shape[0]
    shift = mod_ref[3, :, :][:, None, :]
    scale = mod_ref[4, :, :][:, None, :]
    h = (x_ref[...] * (1.0 + scale) + shift).astype(BF16).reshape(bsz * ROWS, D_MODEL)
    a_ref[...] = lax.dot_general(w_ref[...], h, (((1,), (1,)), ((), ())), preferred_element_type=F32)
    lane = lax.broadcasted_iota(jnp.int32, (SSM_GROUPS, LANES), 1)
    half = SSM_GROUPS * bsz
    for q in range(PAIRS):
        for b in range(bsz):
            x0 = a_ref[(2 * q) * SSM_GROUPS:(2 * q + 1) * SSM_GROUPS, b * LANES:(b + 1) * LANES]
            x1 = a_ref[(2 * q + 1) * SSM_GROUPS:(2 * q + 2) * SSM_GROUPS, b * LANES:(b + 1) * LANES]
            c0, c1 = _lane_halves(x0, x1, lane)
            s_ref[q, pl.ds(b, SSM_GROUPS, stride=bsz), :] = c0
            s_ref[q, pl.ds(half + b, SSM_GROUPS, stride=bsz), :] = c1
    for c2 in range(2):
        for q in range(PAIRS):
            o_ref[c2, :, :, q * LANES:(q + 1) * LANES] = (
                s_ref[q, c2 * half:(c2 + 1) * half, :].reshape(SSM_GROUPS, bsz, LANES))


def _s5in(x, mods, w_t, *, layer):
    bsz, seq, _ = x.shape
    width = SSM_GROUP * CHUNK
    return pl.pallas_call(
        _s5in_kernel,
        grid=(seq // ROWS,),
        in_specs=[
            pl.BlockSpec((bsz, ROWS, D_MODEL), lambda j: (0, j, 0)),
            _resident((None, 9, bsz, D_MODEL), lambda j: (layer, 0, 0, 0)),
            _resident((D_MODEL, D_MODEL), lambda j: (0, 0)),
        ],
        out_specs=pl.BlockSpec((2, SSM_GROUPS, bsz, width), lambda j: (j, 0, 0, 0)),
        out_shape=jax.ShapeDtypeStruct((seq // CHUNK, SSM_GROUPS, bsz, width), F32),
        scratch_shapes=[
            pltpu.VMEM((D_MODEL, bsz * ROWS), F32),
            pltpu.VMEM((PAIRS, 2 * SSM_GROUPS * bsz, LANES), F32),
        ],
        compiler_params=_params(("arbitrary",), 40),
        name="s5_in",
    )(x, mods, w_t)


def _s5scan_kernel(u_ref, arr_ref, air_ref, arc_ref, aic_ref, ldt_ref, btr_ref, bti_ref,
                   ctr_ref, cti_ref, d_ref, y_ref, t_ref, ws_ref, wc_ref):
    n_chunks, bsz, width = u_ref.shape
    st = SSM_STATE
    dt = jnp.exp(ldt_ref[...])

    ar, ai = arr_ref[...], air_ref[...]
    mag = jnp.exp(ar * dt)
    abr, abi = mag * jnp.cos(ai * dt), mag * jnp.sin(ai * dt)
    den = ar * ar + ai * ai
    pr, pim = abr - 1.0, abi
    cfr, cfi = (pr * ar + pim * ai) / den, (pim * ar - pr * ai) / den
    btr, bti = btr_ref[...], bti_ref[...]
    bbr, bbi = cfr * btr - cfi * bti, cfr * bti + cfi * btr

    arc, aic = arc_ref[...], aic_ref[...]
    lane = lax.broadcasted_iota(jnp.int32, (1, LANES), 1)
    lag = (lane % CHUNK).astype(F32)
    first = lane < CHUNK
    ctr, cti = ctr_ref[...], cti_ref[...]

    def c_times_power(shift):
        e = lag + shift
        m = jnp.exp(arc * dt * e)
        er, ei = m * jnp.cos(aic * dt * e), m * jnp.sin(aic * dt * e)
        xr, xi = [], []
        for q in range(PAIRS):
            cr = jnp.where(first, ctr[:, 2 * q:2 * q + 1], ctr[:, 2 * q + 1:2 * q + 2])
            ci = jnp.where(first, cti[:, 2 * q:2 * q + 1], cti[:, 2 * q + 1:2 * q + 2])
            xr.append(cr * er - ci * ei)
            xi.append(cr * ei + ci * er)
        return jnp.concatenate(xr, axis=1), jnp.concatenate(xi, axis=1)

    xr, xi = c_times_power(0.0)
    hi = lax.Precision.HIGHEST
    kflat = (jnp.dot(bbr, xr, precision=hi, preferred_element_type=F32)
             - jnp.dot(bbi, xi, precision=hi, preferred_element_type=F32))
    prow = lax.broadcasted_iota(jnp.int32, kflat.shape, 0)
    plane = lax.broadcasted_iota(jnp.int32, kflat.shape, 1)
    kflat = kflat + jnp.where(plane == prow * CHUNK, d_ref[...], 0.0)

    srow = lax.broadcasted_iota(jnp.int32, (CHUNK, LANES), 0)
    keep = (lax.broadcasted_iota(jnp.int32, (CHUNK, LANES), 1) % CHUNK) >= srow
    for p in range(SSM_GROUP):
        rows = jnp.broadcast_to(kflat[p:p + 1, :], (CHUNK, width))
        for q in range(PAIRS):
            blk = pltpu.roll(rows[:, q * LANES:(q + 1) * LANES], 0, 1, stride=1, stride_axis=0)
            t_ref[p * CHUNK:(p + 1) * CHUNK, q * LANES:(q + 1) * LANES] = (
                jnp.where(keep, blk, 0.0).astype(BF16))

    e = (CHUNK - 1 - lax.broadcasted_iota(jnp.int32, (CHUNK, 1), 0)).astype(F32)
    m = jnp.exp(ar * dt * e)
    er, ei = m * jnp.cos(ai * dt * e), m * jnp.sin(ai * dt * e)
    zpad = jnp.zeros((CHUNK, LANES - st), F32)
    for p in range(SSM_GROUP):
        br, bi = bbr[p:p + 1, :], bbi[p:p + 1, :]
        ws_ref[p * CHUNK:(p + 1) * CHUNK, :] = jnp.concatenate(
            [er * br - ei * bi, zpad, er * bi + ei * br, zpad], axis=1).astype(BF16)

    pr_, pi_ = c_times_power(1.0)
    zrows = jnp.zeros((LANES - st, width), F32)
    wc_ref[...] = jnp.concatenate([pr_, zrows, -pi_, zrows], axis=0).astype(BF16)

    u = u_ref[...].reshape(n_chunks * bsz, width).astype(BF16)
    s = jnp.dot(u, ws_ref[...], preferred_element_type=F32)
    m64 = jnp.exp(ar * dt * CHUNK)
    zlane = jnp.zeros((1, LANES - st), F32)
    a64r = jnp.concatenate([m64 * jnp.cos(ai * dt * CHUNK), zlane], axis=1)
    a64i = jnp.concatenate([m64 * jnp.sin(ai * dt * CHUNK), zlane], axis=1)
    hr = jnp.zeros((bsz, LANES), F32)
    hi_ = jnp.zeros((bsz, LANES), F32)
    prev = []
    for c in range(n_chunks):
        prev.append(jnp.concatenate([hr, hi_], axis=1))
        sr, si = s[c * bsz:(c + 1) * bsz, :LANES], s[c * bsz:(c + 1) * bsz, LANES:]
        hr, hi_ = a64r * hr - a64i * hi_ + sr, a64r * hi_ + a64i * hr + si
    hprev = jnp.concatenate(prev, axis=0).astype(BF16)

    y = (jnp.dot(u, t_ref[...], preferred_element_type=F32)
         + jnp.dot(hprev, wc_ref[...], preferred_element_type=F32))
    y_ref[...] = y.reshape(n_chunks, bsz, width)


def _s5scan(u4, a_re, a_im, log_dt, b_re, b_im, c_re, c_im, d):
    n_chunks, groups, bsz, width = u4.shape
    st, pg = SSM_STATE, SSM_GROUP
    per_g = lambda *shape: pl.BlockSpec((None,) + shape, lambda g: (g,) + (0,) * len(shape))
    return pl.pallas_call(
        _s5scan_kernel,
        grid=(groups,),
        in_specs=[
            pl.BlockSpec((n_chunks, None, bsz, width), lambda g: (0, g, 0, 0)),
            per_g(1, st), per_g(1, st), per_g(st, 1), per_g(st, 1), per_g(1, 1),
            per_g(pg, st), per_g(pg, st), per_g(st, pg), per_g(st, pg), per_g(pg, 1),
        ],
        out_specs=pl.BlockSpec((n_chunks, None, bsz, width), lambda g: (0, g, 0, 0)),
        out_shape=jax.ShapeDtypeStruct(u4.shape, F32),
        scratch_shapes=[
            pltpu.VMEM((width, width), BF16),
            pltpu.VMEM((width, 2 * LANES), BF16),
            pltpu.VMEM((2 * LANES, width), BF16),
        ],
        compiler_params=_params(("arbitrary",), 32),
        name="s5_scan",
    )(u4,
      a_re.reshape(groups, 1, st), a_im.reshape(groups, 1, st),
      a_re.reshape(groups, st, 1), a_im.reshape(groups, st, 1),
      log_dt.reshape(groups, 1, 1),
      jnp.swapaxes(b_re, 1, 2), jnp.swapaxes(b_im, 1, 2),
      jnp.swapaxes(c_re, 1, 2), jnp.swapaxes(c_im, 1, 2),
      d.reshape(groups, pg, 1))


def _s5out_kernel(y_ref, x_ref, mod_ref, wg_ref, wo_ref, g_ref, b_ref, o_ref, a_ref, s_ref):
    bsz = x_ref.shape[0]
    half = SSM_GROUPS * bsz
    for c2 in range(2):
        for q in range(PAIRS):
            s_ref[q, c2 * half:(c2 + 1) * half, :] = (
                y_ref[c2, :, :, q * LANES:(q + 1) * LANES].reshape(half, LANES))
    lane = lax.broadcasted_iota(jnp.int32, (SSM_GROUPS, LANES), 1)
    for q in range(PAIRS):
        for b in range(bsz):
            c0 = s_ref[q, pl.ds(b, SSM_GROUPS, stride=bsz), :]
            c1 = s_ref[q, pl.ds(half + b, SSM_GROUPS, stride=bsz), :]
            x0, x1 = _lane_halves(c0, c1, lane)
            a_ref[(2 * q) * SSM_GROUPS:(2 * q + 1) * SSM_GROUPS, b * LANES:(b + 1) * LANES] = x0
            a_ref[(2 * q + 1) * SSM_GROUPS:(2 * q + 2) * SSM_GROUPS, b * LANES:(b + 1) * LANES] = x1
    z = jax.nn.gelu(a_ref[...], approximate=True)
    gt = jnp.dot(wg_ref[...], z.astype(BF16), preferred_element_type=F32)
    zz = (z * jax.nn.sigmoid(gt)).astype(BF16)
    y = lax.dot_general(zz, wo_ref[...], (((0,), (0,)), ((), ())), preferred_element_type=F32)
    y = y.reshape(bsz, ROWS, D_MODEL)
    gate = 1.0 + mod_ref[5, :, :][:, None, :]
    o_ref[...] = _layer_norm(ALPHA * x_ref[...] + gate * y, g_ref[...], b_ref[...])


def _s5out(y4, x, mods, wg_t, wo, ln_g, ln_b, *, layer):
    bsz, seq, _ = x.shape
    width = SSM_GROUP * CHUNK
    return pl.pallas_call(
        _s5out_kernel,
        grid=(seq // ROWS,),
        in_specs=[
            pl.BlockSpec((2, SSM_GROUPS, bsz, width), lambda j: (j, 0, 0, 0)),
            pl.BlockSpec((bsz, ROWS, D_MODEL), lambda j: (0, j, 0)),
            _resident((None, 9, bsz, D_MODEL), lambda j: (layer, 0, 0, 0)),
            _resident((D_MODEL, D_MODEL), lambda j: (0, 0)),
            _resident((D_MODEL, D_MODEL), lambda j: (0, 0)),
            _resident((None, None, 1, D_MODEL), lambda j: (layer, 1, 0, 0)),
            _resident((None, None, 1, D_MODEL), lambda j: (layer, 1, 0, 0)),
        ],
        out_specs=pl.BlockSpec((bsz, ROWS, D_MODEL), lambda j: (0, j, 0)),
        out_shape=jax.ShapeDtypeStruct(x.shape, F32),
        scratch_shapes=[
            pltpu.VMEM((D_MODEL, bsz * ROWS), F32),
            pltpu.VMEM((PAIRS, 2 * SSM_GROUPS * bsz, LANES), F32),
        ],
        compiler_params=_params(("arbitrary",), 48),
        name="s5_out",
    )(y4, x, mods, wg_t, wo, ln_g, ln_b)


def kernel(x, c, ada_w, ada_b, ln_g, ln_b, ffn_w1, ffn_w3, ffn_w2, attn_w_in, attn_lam, attn_subln_g, attn_w_out, ssm_w_in, ssm_a_re, ssm_a_im, ssm_log_dt, ssm_b_re, ssm_b_im, ssm_c_re, ssm_c_im, ssm_d, ssm_w_gate, ssm_w_out):
    mods = _ada(c, ada_w, ada_b)
    lng = ln_g.reshape(DEPTH, 3, 1, D_MODEL)
    lnb = ln_b.reshape(DEPTH, 3, 1, D_MODEL)
    w1, w3, w2 = ffn_w1, ffn_w3, ffn_w2
    gp = (SSM_GROUPS, SSM_GROUP)

    for layer in range(DEPTH):
        i = layer // 2
        x = _ffn(x, mods, w1, w3, w2, lng, lnb, layer=layer, half=0, sub=0)
        if layer % 2 == 0:
            lam_init = 0.8 - 0.6 * math.exp(-0.3 * layer)
            qkv, w1, w3, w2 = _modproj(x, mods, attn_w_in[i].astype(BF16), ffn_w1, ffn_w3, ffn_w2,
                                       layer=layer, sub=1)
            o = _attention(qkv, attn_lam[i], attn_subln_g[i], lam_init=lam_init)
            x = _projres(o, x, mods, attn_w_out[i].astype(BF16), lng, lnb, layer=layer, sub=1)
        else:
            rows_pg = lambda w: w.reshape(*gp, D_MODEL).transpose(1, 0, 2).reshape(D_MODEL, D_MODEL)
            w_in_t = rows_pg(ssm_w_in[i].astype(BF16).T)
            wg_t = rows_pg(rows_pg(ssm_w_gate[i].astype(BF16)).T)
            wo = rows_pg(ssm_w_out[i].astype(BF16))
            u4 = _s5in(x, mods, w_in_t, layer=layer)
            y4 = _s5scan(u4, ssm_a_re[i], ssm_a_im[i], ssm_log_dt[i], ssm_b_re[i], ssm_b_im[i],
                         ssm_c_re[i], ssm_c_im[i], ssm_d[i])
            x = _s5out(y4, x, mods, wg_t, wo, lng, lnb, layer=layer)
        x = _ffn(x, mods, w1, w3, w2, lng, lnb, layer=layer, half=1, sub=2)
    return x
```

```python
import functools
import math

import jax
import jax.numpy as jnp
from jax import lax
from jax.experimental import pallas as pl
from jax.experimental.pallas import tpu as pltpu

D_MODEL = 1024
DEPTH = 2
CHUNK = 64
ATTN_HEADS = 8
ATTN_HEAD_DIM = 64
SSM_GROUP = 16
SSM_GROUPS = D_MODEL // SSM_GROUP
SSM_STATE = 64
D_FF = 2816
ALPHA = (2 * DEPTH) ** 0.25
LN_EPS = 1e-5

LANES = 128
F_CHUNK = 256
MIB = 1024 * 1024

F32 = jnp.float32
BF16 = jnp.bfloat16


def _params(sem, vmem_mib):
    return pltpu.CompilerParams(dimension_semantics=sem, vmem_limit_bytes=vmem_mib * MIB)


def _resident(block_shape, index_map):
    return pl.BlockSpec(block_shape, index_map, pipeline_mode=pl.Buffered(1))


def _layer_norm(r, g, b):
    mu = jnp.mean(r, axis=-1, keepdims=True)
    d = r - mu
    var = jnp.mean(d * d, axis=-1, keepdims=True)
    return d * lax.rsqrt(var + LN_EPS) * g + b


def _mods(mod_ref, sub, bidx):
    shift = mod_ref[3 * sub + 0, pl.ds(bidx, 1), :]
    scale = mod_ref[3 * sub + 1, pl.ds(bidx, 1), :]
    gate = 1.0 + mod_ref[3 * sub + 2, pl.ds(bidx, 1), :]
    return shift, scale, gate


def _ada_kernel(c_ref, w_ref, b_ref, o_ref):
    c = c_ref[...]
    cond = (c * jax.nn.sigmoid(c)).astype(BF16)
    o_ref[...] = jnp.dot(cond, w_ref[...].astype(BF16), preferred_element_type=F32) + b_ref[...]


def _ada(c, ada_w, ada_b):
    bsz = c.shape[0]
    n_blk = ada_w.shape[2] // D_MODEL
    return pl.pallas_call(
        _ada_kernel,
        grid=(DEPTH, n_blk),
        in_specs=[
            pl.BlockSpec((bsz, D_MODEL), lambda l, n: (0, 0)),
            pl.BlockSpec((None, D_MODEL, D_MODEL), lambda l, n: (l, 0, n)),
            pl.BlockSpec((None, None, 1, D_MODEL), lambda l, n: (l, n, 0, 0)),
        ],
        out_specs=pl.BlockSpec((None, None, bsz, D_MODEL), lambda l, n: (l, n, 0, 0)),
        out_shape=jax.ShapeDtypeStruct((DEPTH, n_blk, bsz, D_MODEL), F32),
        compiler_params=_params(("arbitrary", "arbitrary"), 24),
        name="ada_mods",
    )(c, ada_w, ada_b.reshape(DEPTH, n_blk, 1, D_MODEL))


W_ROWS = 256


def _ffn_kernel(*refs, sub, ts, layer, half, proj_sub):
    if proj_sub is None:
        x_ref, mod_ref, w1_in, w3_in, w2_in, g_ref, b_ref, o_ref, *scratch = refs
        xin_ref = x_ref
    else:
        (x_ref, a_ref, wo_ref, gp_ref, bp_ref, mod_ref, w1_in, w3_in, w2_in, g_ref, b_ref, o_ref,
         xin_ref, *scratch) = refs
    shift, scale, gate = _mods(mod_ref, sub, pl.program_id(0))
    n_f = D_FF // F_CHUNK
    chunk = lambda f: slice(f * F_CHUNK, (f + 1) * F_CHUNK)
    w1_ref, w3_ref, w2_ref = scratch[:3] if scratch else (w1_in, w3_in, w2_in)

    def stream_weights():
        wide_ref, narrow_ref, sem = scratch[3:]
        chunks = [(src, dst, stage, kind, slice(r, r + W_ROWS))
                  for src, dst, stage, kind in ((w1_in, w1_ref, wide_ref, 0), (w3_in, w3_ref, wide_ref, 0),
                                                (w2_in, w2_ref, narrow_ref, 1))
                  for r in range(0, dst.shape[0], W_ROWS)]

        def copy(i):
            src, _, stage, kind, rows = chunks[i]
            return pltpu.make_async_copy(src.at[layer, half, rows, :], stage.at[i % 2], sem.at[kind, i % 2])

        copy(0).start()
        for i, (_, dst, stage, _, rows) in enumerate(chunks):
            if i + 1 < len(chunks):
                copy(i + 1).start()
            copy(i).wait()
            dst[rows, :] = stage[i % 2].astype(BF16)

    if scratch:
        pl.when(jnp.logical_and(pl.program_id(0) == 0, pl.program_id(1) == 0))(stream_weights)

    def modulated(rows):
        if proj_sub is not None:
            _, _, gate_p = _mods(mod_ref, proj_sub, pl.program_id(0))
            y = jnp.dot(a_ref[rows, :], wo_ref[...], preferred_element_type=F32)
            xin_ref[rows, :] = _layer_norm(ALPHA * x_ref[rows, :] + gate_p * y, gp_ref[...], bp_ref[...])
        return (xin_ref[rows, :] * (1.0 + scale) + shift).astype(BF16)

    def up(h, f):
        return (jnp.dot(h, w1_ref[:, chunk(f)], preferred_element_type=F32),
                jnp.dot(h, w3_ref[:, chunk(f)], preferred_element_type=F32))

    def down(ab, acc, f):
        a, b = ab
        u = (a * jax.nn.sigmoid(a) * b).astype(BF16)
        d = jnp.dot(u, w2_ref[chunk(f), :], preferred_element_type=F32)
        return d if acc is None else acc + d

    def epilogue(rows, acc):
        r = ALPHA * xin_ref[rows, :] + gate * (0.5 * acc)
        o_ref[rows, :] = _layer_norm(r, g_ref[...], b_ref[...])

    n_sub = x_ref.shape[0] // ts
    tile = lambda t: slice(t * ts, (t + 1) * ts)
    h = modulated(tile(0))
    nxt = up(h, 0)
    for t in range(n_sub):
        acc = None
        for f in range(n_f):
            cur = nxt
            if f + 1 < n_f:
                nxt = up(h, f + 1)
            elif t + 1 < n_sub:
                nxt = up(h_next, 0)
            if f == n_f // 2 and t + 1 < n_sub:
                h_next = modulated(tile(t + 1))
            acc = down(cur, acc, f)
        epilogue(tile(t), acc)
        if t + 1 < n_sub:
            h = h_next


def _ffn(x, mods, w1, w3, w2, ln_g, ln_b, *, layer, half, sub, proj=None, tm=1024, ts=256):
    bsz, seq, _ = x.shape
    row_block = pl.BlockSpec((None, tm, D_MODEL), lambda b, i: (b, i, 0))
    ln_spec = lambda s: _resident((None, None, 1, D_MODEL), lambda b, i: (layer, s, 0, 0))
    if w1.dtype == BF16:
        wspec = lambda r, c: _resident((None, None, r, c), lambda b, i: (layer, half, 0, 0))
        w_specs = [wspec(D_MODEL, D_FF), wspec(D_MODEL, D_FF), wspec(D_FF, D_MODEL)]
        scratch = []
    else:
        w_specs = [pl.BlockSpec(memory_space=pl.ANY)] * 3
        scratch = [
            pltpu.VMEM((D_MODEL, D_FF), BF16), pltpu.VMEM((D_MODEL, D_FF), BF16),
            pltpu.VMEM((D_FF, D_MODEL), BF16),
            pltpu.VMEM((2, W_ROWS, D_FF), F32), pltpu.VMEM((2, W_ROWS, D_MODEL), F32),
            pltpu.SemaphoreType.DMA((2, 2)),
        ]
    operands, in_specs = [x], [row_block]
    if proj is not None:
        a, wo, proj_sub = proj
        operands += [a, wo, ln_g, ln_b]
        in_specs += [row_block, _resident((D_MODEL, D_MODEL), lambda b, i: (0, 0)),
                     ln_spec(proj_sub), ln_spec(proj_sub)]
        scratch = [pltpu.VMEM((tm, D_MODEL), F32)] + scratch
    operands += [mods, w1, w3, w2, ln_g, ln_b]
    in_specs += [_resident((None, 9, bsz, D_MODEL), lambda b, i: (layer, 0, 0, 0)), *w_specs,
                 ln_spec(sub), ln_spec(sub)]
    return pl.pallas_call(
        functools.partial(_ffn_kernel, sub=sub, ts=ts, layer=layer, half=half,
                          proj_sub=None if proj is None else proj[2]),
        grid=(bsz, seq // tm),
        in_specs=in_specs,
        out_specs=row_block,
        out_shape=jax.ShapeDtypeStruct(x.shape, F32),
        scratch_shapes=scratch,
        compiler_params=_params(("arbitrary", "arbitrary"), 52),
        name=f"ffn_l{layer}_h{half}",
    )(*operands)


def _modproj_kernel(x_ref, mod_ref, w_ref, o_ref, *, sub):
    shift, scale, _ = _mods(mod_ref, sub, pl.program_id(0))
    h = (x_ref[...] * (1.0 + scale) + shift).astype(BF16)
    for n in range(o_ref.shape[1] // D_MODEL):
        cols = slice(n * D_MODEL, (n + 1) * D_MODEL)
        o_ref[:, cols] = jnp.dot(h, w_ref[:, cols], preferred_element_type=F32).astype(o_ref.dtype)


def _modproj(x, mods, w, *, layer, sub, tm=1024):
    bsz, seq, _ = x.shape
    n_out = w.shape[1]
    return pl.pallas_call(
        functools.partial(_modproj_kernel, sub=sub),
        grid=(bsz, seq // tm),
        in_specs=[
            pl.BlockSpec((None, tm, D_MODEL), lambda b, i: (b, i, 0)),
            _resident((None, 9, bsz, D_MODEL), lambda b, i: (layer, 0, 0, 0)),
            _resident((D_MODEL, n_out), lambda b, i: (0, 0)),
        ],
        out_specs=pl.BlockSpec((None, tm, n_out), lambda b, i: (b, i, 0)),
        out_shape=jax.ShapeDtypeStruct((bsz, seq, n_out), BF16),
        compiler_params=_params(("arbitrary", "arbitrary"), 44),
        name=f"modproj_l{layer}",
    )(x, mods, w)


def _projres_kernel(a_ref, x_ref, mod_ref, w_ref, g_ref, b_ref, o_ref, *, sub, ts):
    _, _, gate = _mods(mod_ref, sub, pl.program_id(0))
    n_sub = x_ref.shape[0] // ts
    proj = lambda t: jnp.dot(a_ref[t * ts:(t + 1) * ts, :], w_ref[...], preferred_element_type=F32)
    nxt = proj(0)
    for t in range(n_sub):
        y, rows = nxt, slice(t * ts, (t + 1) * ts)
        if t + 1 < n_sub:
            nxt = proj(t + 1)
        o_ref[rows, :] = _layer_norm(ALPHA * x_ref[rows, :] + gate * y, g_ref[...], b_ref[...])


def _projres(a, x, mods, w, ln_g, ln_b, *, layer, sub, tm=1024, ts=256):
    bsz, seq, _ = x.shape
    return pl.pallas_call(
        functools.partial(_projres_kernel, sub=sub, ts=ts),
        grid=(bsz, seq // tm),
        in_specs=[
            pl.BlockSpec((None, tm, D_MODEL), lambda b, i: (b, i, 0)),
            pl.BlockSpec((None, tm, D_MODEL), lambda b, i: (b, i, 0)),
            _resident((None, 9, bsz, D_MODEL), lambda b, i: (layer, 0, 0, 0)),
            _resident((D_MODEL, D_MODEL), lambda b, i: (0, 0)),
            _resident((None, None, 1, D_MODEL), lambda b, i: (layer, sub, 0, 0)),
            _resident((None, None, 1, D_MODEL), lambda b, i: (layer, sub, 0, 0)),
        ],
        out_specs=pl.BlockSpec((None, tm, D_MODEL), lambda b, i: (b, i, 0)),
        out_shape=jax.ShapeDtypeStruct(x.shape, F32),
        compiler_params=_params(("arbitrary", "arbitrary"), 36),
        name=f"projres_l{layer}",
    )(a, x, mods, w, ln_g, ln_b)


def _attn_kernel(q_ref, k_ref, v_ref, lam_ref, sg_ref, w1_ref, w3_ref, w2_ref,
                 o_ref, w1b_ref, w3b_ref, w2b_ref, vx_ref, *, lam_init, tq, w2_span):
    seq = q_ref.shape[0]
    hd = ATTN_HEAD_DIM
    lam = lam_ref[...]
    lam_full = (jnp.exp(jnp.sum(lam[0:1] * lam[1:2], axis=-1, keepdims=True))
                - jnp.exp(jnp.sum(lam[2:3] * lam[3:4], axis=-1, keepdims=True)) + lam_init)
    lane = lax.broadcasted_iota(jnp.int32, (tq, 2 * hd), 1)
    rq = lax.broadcasted_iota(jnp.int32, (tq, tq), 0) // CHUNK
    ck = lax.broadcasted_iota(jnp.int32, (tq, tq), 1) // CHUNK
    allowed = ck <= rq
    nt = (((1,), (1,)), ((), ()))

    hw = 2 * hd
    vx_ref[:, :hw] = v_ref[...]
    vx_ref[:, hw:] = jnp.ones((seq, hw), BF16)

    def score(i, m):
        kend = (i + 1) * tq
        q = q_ref[i * tq:kend, :].astype(F32) * (hd ** -0.5)
        qm = jnp.where((lane < hd) if m == 0 else (lane >= hd), q, 0.0).astype(BF16)
        s = lax.dot_general(qm, k_ref[0:kend, :], nt, preferred_element_type=F32)
        diag = jnp.where(allowed, s[:, kend - tq:], -jnp.inf)
        return diag if kend == tq else jnp.concatenate([s[:, :kend - tq], diag], axis=1)

    def attend(i, s):
        kend = (i + 1) * tq
        p = jnp.exp(s - jnp.max(s, axis=-1, keepdims=True))
        r = jnp.dot(p.astype(BF16), vx_ref[0:kend, :], preferred_element_type=F32)
        return r[:, :hw] / r[:, hw:]

    def finish(i, o1, o2):
        o = o1 - lam_full * o2
        o = o * lax.rsqrt(jnp.mean(o * o, axis=-1, keepdims=True) + LN_EPS)
        o_ref[i * tq:(i + 1) * tq, :] = (o * sg_ref[...] * (1.0 - lam_init)).astype(o_ref.dtype)

    n = seq // tq
    units = [(i, m) for i in reversed(range(n)) for m in range(2)]
    pend = score(*units[0])
    outs = {}
    for u, (i, m) in enumerate(units):
        cur = pend
        if u + 1 < len(units):
            pend = score(*units[u + 1])
        outs[(i, m)] = attend(i, cur)
        if m == 1:
            finish(i, outs.pop((i, 0)), outs.pop((i, 1)))

    w1b_ref[...] = w1_ref[...].astype(BF16)
    w3b_ref[...] = w3_ref[...].astype(BF16)

    @pl.when((pl.program_id(0) * pl.num_programs(1) + pl.program_id(1)) % w2_span == 0)
    def _():
        w2b_ref[...] = w2_ref[...].astype(BF16)


def _attention(qkv, lam, subln_g, w1, w3, w2, *, lam_init, tq=256):
    bsz, seq, _ = qkv.shape
    hw = 2 * ATTN_HEAD_DIM
    steps = bsz * ATTN_HEADS
    up_rows = D_MODEL // steps
    down_blocks = 16
    down_rows = D_FF // down_blocks
    up_spec = pl.BlockSpec((DEPTH, 2, up_rows, D_FF), lambda b, h: (0, 0, b * ATTN_HEADS + h, 0))
    down_spec = pl.BlockSpec((DEPTH, 2, down_rows, D_MODEL),
                             lambda b, h: (0, 0, (b * ATTN_HEADS + h) // (steps // down_blocks), 0))
    return pl.pallas_call(
        functools.partial(_attn_kernel, lam_init=lam_init, tq=tq, w2_span=steps // down_blocks),
        grid=(bsz, ATTN_HEADS),
        in_specs=[
            pl.BlockSpec((None, seq, hw), lambda b, h: (b, 0, h)),
            pl.BlockSpec((None, seq, hw), lambda b, h: (b, 0, ATTN_HEADS + h)),
            pl.BlockSpec((None, seq, hw), lambda b, h: (b, 0, 2 * ATTN_HEADS + h)),
            _resident((4, ATTN_HEAD_DIM), lambda b, h: (0, 0)),
            _resident((1, hw), lambda b, h: (0, 0)),
            up_spec, up_spec, down_spec,
        ],
        out_specs=[pl.BlockSpec((None, seq, hw), lambda b, h: (b, 0, h)), up_spec, up_spec, down_spec],
        out_shape=[jax.ShapeDtypeStruct((bsz, seq, D_MODEL), BF16), jax.ShapeDtypeStruct(w1.shape, BF16),
                   jax.ShapeDtypeStruct(w3.shape, BF16), jax.ShapeDtypeStruct(w2.shape, BF16)],
        scratch_shapes=[pltpu.VMEM((seq, 2 * hw), BF16)],
        compiler_params=_params(("arbitrary", "arbitrary"), 48),
        name="diff_attention",
    )(qkv, qkv, qkv, lam, subln_g.reshape(1, hw), w1, w3, w2)


ROWS = 2 * CHUNK
PAIRS = SSM_GROUP // 2


def _lane_halves(e, o, lane):
    lo = jnp.where(lane < CHUNK, e, pltpu.roll(o, CHUNK, 1))
    hi = jnp.where(lane < CHUNK, pltpu.roll(e, CHUNK, 1), o)
    return lo, hi


def _s5in_kernel(x_ref, mod_ref, w_ref, o_ref, a_ref, s_ref):
    bsz = x_ref.shape[0]
    shift = mod_ref[3, :, :][:, None, :]
    scale = mod_ref[4, :, :][:, None, :]
    h = (x_ref[...] * (1.0 + scale) + shift).astype(BF16).reshape(bsz * ROWS, D_MODEL)
    a_ref[...] = lax.dot_general(w_ref[...], h, (((1,), (1,)), ((), ())), preferred_element_type=F32)
    lane = lax.broadcasted_iota(jnp.int32, (SSM_GROUPS, LANES), 1)
    half = SSM_GROUPS * bsz
    for q in range(PAIRS):
        for b in range(bsz):
            x0 = a_ref[(2 * q) * SSM_GROUPS:(2 * q + 1) * SSM_GROUPS, b * LANES:(b + 1) * LANES]
            x1 = a_ref[(2 * q + 1) * SSM_GROUPS:(2 * q + 2) * SSM_GROUPS, b * LANES:(b + 1) * LANES]
            c0, c1 = _lane_halves(x0, x1, lane)
            s_ref[q, pl.ds(b, SSM_GROUPS, stride=bsz), :] = c0
            s_ref[q, pl.ds(half + b, SSM_GROUPS, stride=bsz), :] = c1
    for c2 in range(2):
        for q in range(PAIRS):
            o_ref[c2, :, :, q * LANES:(q + 1) * LANES] = (
                s_ref[q, c2 * half:(c2 + 1) * half, :].reshape(SSM_GROUPS, bsz, LANES))


def _s5in(x, mods, w_t, *, layer):
    bsz, seq, _ = x.shape
    width = SSM_GROUP * CHUNK
    return pl.pallas_call(
        _s5in_kernel,
        grid=(seq // ROWS,),
        in_specs=[
            pl.BlockSpec((bsz, ROWS, D_MODEL), lambda j: (0, j, 0)),
            _resident((None, 9, bsz, D_MODEL), lambda j: (layer, 0, 0, 0)),
            _resident((D_MODEL, D_MODEL), lambda j: (0, 0)),
        ],
        out_specs=pl.BlockSpec((2, SSM_GROUPS, bsz, width), lambda j: (j, 0, 0, 0)),
        out_shape=jax.ShapeDtypeStruct((seq // CHUNK, SSM_GROUPS, bsz, width), F32),
        scratch_shapes=[
            pltpu.VMEM((D_MODEL, bsz * ROWS), F32),
            pltpu.VMEM((PAIRS, 2 * SSM_GROUPS * bsz, LANES), F32),
        ],
        compiler_params=_params(("arbitrary",), 40),
        name="s5_in",
    )(x, mods, w_t)


def _s5scan_kernel(u_ref, arr_ref, air_ref, arc_ref, aic_ref, ldt_ref, btr_ref, bti_ref,
                   ctr_ref, cti_ref, d_ref, y_ref, t_ref, ws_ref, wc_ref):
    n_chunks, bsz, width = u_ref.shape
    st = SSM_STATE
    dt = jnp.exp(ldt_ref[...])

    ar, ai = arr_ref[...], air_ref[...]
    mag = jnp.exp(ar * dt)
    abr, abi = mag * jnp.cos(ai * dt), mag * jnp.sin(ai * dt)
    den = ar * ar + ai * ai
    pr, pim = abr - 1.0, abi
    cfr, cfi = (pr * ar + pim * ai) / den, (pim * ar - pr * ai) / den
    btr, bti = btr_ref[...], bti_ref[...]
    bbr, bbi = cfr * btr - cfi * bti, cfr * bti + cfi * btr

    arc, aic = arc_ref[...], aic_ref[...]
    lane = lax.broadcasted_iota(jnp.int32, (1, LANES), 1)
    lag = (lane % CHUNK).astype(F32)
    first = lane < CHUNK
    ctr, cti = ctr_ref[...], cti_ref[...]

    def c_times_power(shift):
        e = lag + shift
        m = jnp.exp(arc * dt * e)
        er, ei = m * jnp.cos(aic * dt * e), m * jnp.sin(aic * dt * e)
        xr, xi = [], []
        for q in range(PAIRS):
            cr = jnp.where(first, ctr[:, 2 * q:2 * q + 1], ctr[:, 2 * q + 1:2 * q + 2])
            ci = jnp.where(first, cti[:, 2 * q:2 * q + 1], cti[:, 2 * q + 1:2 * q + 2])
            xr.append(cr * er - ci * ei)
            xi.append(cr * ei + ci * er)
        return jnp.concatenate(xr, axis=1), jnp.concatenate(xi, axis=1)

    xr, xi = c_times_power(0.0)
    hi = lax.Precision.HIGHEST
    kflat = (jnp.dot(bbr, xr, precision=hi, preferred_element_type=F32)
             - jnp.dot(bbi, xi, precision=hi, preferred_element_type=F32))
    prow = lax.broadcasted_iota(jnp.int32, kflat.shape, 0)
    plane = lax.broadcasted_iota(jnp.int32, kflat.shape, 1)
    kflat = kflat + jnp.where(plane == prow * CHUNK, d_ref[...], 0.0)

    srow = lax.broadcasted_iota(jnp.int32, (CHUNK, LANES), 0)
    keep = (lax.broadcasted_iota(jnp.int32, (CHUNK, LANES), 1) % CHUNK) >= srow
    for p in range(SSM_GROUP):
        rows = jnp.broadcast_to(kflat[p:p + 1, :], (CHUNK, width))
        for q in range(PAIRS):
            blk = pltpu.roll(rows[:, q * LANES:(q + 1) * LANES], 0, 1, stride=1, stride_axis=0)
            t_ref[p * CHUNK:(p + 1) * CHUNK, q * LANES:(q + 1) * LANES] = (
                jnp.where(keep, blk, 0.0).astype(BF16))

    e = (CHUNK - 1 - lax.broadcasted_iota(jnp.int32, (CHUNK, 1), 0)).astype(F32)
    m = jnp.exp(ar * dt * e)
    er, ei = m * jnp.cos(ai * dt * e), m * jnp.sin(ai * dt * e)
    zpad = jnp.zeros((CHUNK, LANES - st), F32)
    for p in range(SSM_GROUP):
        br, bi = bbr[p:p + 1, :], bbi[p:p + 1, :]
        ws_ref[p * CHUNK:(p + 1) * CHUNK, :] = jnp.concatenate(
            [er * br - ei * bi, zpad, er * bi + ei * br, zpad], axis=1).astype(BF16)

    pr_, pi_ = c_times_power(1.0)
    zrows = jnp.zeros((LANES - st, width), F32)
    wc_ref[...] = jnp.concatenate([pr_, zrows, -pi_, zrows], axis=0).astype(BF16)

    u = u_ref[...].reshape(n_chunks * bsz, width).astype(BF16)
    s = jnp.dot(u, ws_ref[...], preferred_element_type=F32)
    m64 = jnp.exp(ar * dt * CHUNK)
    zlane = jnp.zeros((1, LANES - st), F32)
    a64r = jnp.concatenate([m64 * jnp.cos(ai * dt * CHUNK), zlane], axis=1)
    a64i = jnp.concatenate([m64 * jnp.sin(ai * dt * CHUNK), zlane], axis=1)
    hr = jnp.zeros((bsz, LANES), F32)
    hi_ = jnp.zeros((bsz, LANES), F32)
    prev = []
    for c in range(n_chunks):
        prev.append(jnp.concatenate([hr, hi_], axis=1))
        sr, si = s[c * bsz:(c + 1) * bsz, :LANES], s[c * bsz:(c + 1) * bsz, LANES:]
        hr, hi_ = a64r * hr - a64i * hi_ + sr, a64r * hi_ + a64i * hr + si
    hprev = jnp.concatenate(prev, axis=0).astype(BF16)

    y = (jnp.dot(u, t_ref[...], preferred_element_type=F32)
         + jnp.dot(hprev, wc_ref[...], preferred_element_type=F32))
    y_ref[...] = y.reshape(n_chunks, bsz, width)


def _s5scan(u4, a_re, a_im, log_dt, b_re, b_im, c_re, c_im, d):
    n_chunks, groups, bsz, width = u4.shape
    st, pg = SSM_STATE, SSM_GROUP
    per_g = lambda *shape: pl.BlockSpec((None,) + shape, lambda g: (g,) + (0,) * len(shape))
    return pl.pallas_call(
        _s5scan_kernel,
        grid=(groups,),
        in_specs=[
            pl.BlockSpec((n_chunks, None, bsz, width), lambda g: (0, g, 0, 0)),
            per_g(1, st), per_g(1, st), per_g(st, 1), per_g(st, 1), per_g(1, 1),
            per_g(pg, st), per_g(pg, st), per_g(st, pg), per_g(st, pg), per_g(pg, 1),
        ],
        out_specs=pl.BlockSpec((n_chunks, None, bsz, width), lambda g: (0, g, 0, 0)),
        out_shape=jax.ShapeDtypeStruct(u4.shape, F32),
        scratch_shapes=[
            pltpu.VMEM((width, width), BF16),
            pltpu.VMEM((width, 2 * LANES), BF16),
            pltpu.VMEM((2 * LANES, width), BF16),
        ],
        compiler_params=_params(("arbitrary",), 32),
        name="s5_scan",
    )(u4,
      a_re.reshape(groups, 1, st), a_im.reshape(groups, 1, st),
      a_re.reshape(groups, st, 1), a_im.reshape(groups, st, 1),
      log_dt.reshape(groups, 1, 1),
      jnp.swapaxes(b_re, 1, 2), jnp.swapaxes(b_im, 1, 2),
      jnp.swapaxes(c_re, 1, 2), jnp.swapaxes(c_im, 1, 2),
      d.reshape(groups, pg, 1))


def _s5out_kernel(y_ref, x_ref, mod_ref, wg_ref, wo_ref, g_ref, b_ref, o_ref, a_ref, s_ref):
    bsz = x_ref.shape[0]
    half = SSM_GROUPS * bsz
    for c2 in range(2):
        for q in range(PAIRS):
            s_ref[q, c2 * half:(c2 + 1) * half, :] = (
                y_ref[c2, :, :, q * LANES:(q + 1) * LANES].reshape(half, LANES))
    lane = lax.broadcasted_iota(jnp.int32, (SSM_GROUPS, LANES), 1)
    for q in range(PAIRS):
        for b in range(bsz):
            c0 = s_ref[q, pl.ds(b, SSM_GROUPS, stride=bsz), :]
            c1 = s_ref[q, pl.ds(half + b, SSM_GROUPS, stride=bsz), :]
            x0, x1 = _lane_halves(c0, c1, lane)
            a_ref[(2 * q) * SSM_GROUPS:(2 * q + 1) * SSM_GROUPS, b * LANES:(b + 1) * LANES] = x0
            a_ref[(2 * q + 1) * SSM_GROUPS:(2 * q + 2) * SSM_GROUPS, b * LANES:(b + 1) * LANES] = x1
    z = jax.nn.gelu(a_ref[...], approximate=True)
    gt = jnp.dot(wg_ref[...], z.astype(BF16), preferred_element_type=F32)
    zz = (z * jax.nn.sigmoid(gt)).astype(BF16)
    y = lax.dot_general(zz, wo_ref[...], (((0,), (0,)), ((), ())), preferred_element_type=F32)
    y = y.reshape(bsz, ROWS, D_MODEL)
    gate = 1.0 + mod_ref[5, :, :][:, None, :]
    o_ref[...] = _layer_norm(ALPHA * x_ref[...] + gate * y, g_ref[...], b_ref[...])


def _s5out(y4, x, mods, wg_t, wo, ln_g, ln_b, *, layer):
    bsz, seq, _ = x.shape
    width = SSM_GROUP * CHUNK
    return pl.pallas_call(
        _s5out_kernel,
        grid=(seq // ROWS,),
        in_specs=[
            pl.BlockSpec((2, SSM_GROUPS, bsz, width), lambda j: (j, 0, 0, 0)),
            pl.BlockSpec((bsz, ROWS, D_MODEL), lambda j: (0, j, 0)),
            _resident((None, 9, bsz, D_MODEL), lambda j: (layer, 0, 0, 0)),
            _resident((D_MODEL, D_MODEL), lambda j: (0, 0)),
            _resident((D_MODEL, D_MODEL), lambda j: (0, 0)),
            _resident((None, None, 1, D_MODEL), lambda j: (layer, 1, 0, 0)),
            _resident((None, None, 1, D_MODEL), lambda j: (layer, 1, 0, 0)),
        ],
        out_specs=pl.BlockSpec((bsz, ROWS, D_MODEL), lambda j: (0, j, 0)),
        out_shape=jax.ShapeDtypeStruct(x.shape, F32),
        scratch_shapes=[
            pltpu.VMEM((D_MODEL, bsz * ROWS), F32),
            pltpu.VMEM((PAIRS, 2 * SSM_GROUPS * bsz, LANES), F32),
        ],
        compiler_params=_params(("arbitrary",), 48),
        name="s5_out",
    )(y4, x, mods, wg_t, wo, ln_g, ln_b)


def kernel(x, c, ada_w, ada_b, ln_g, ln_b, ffn_w1, ffn_w3, ffn_w2, attn_w_in, attn_lam, attn_subln_g, attn_w_out, ssm_w_in, ssm_a_re, ssm_a_im, ssm_log_dt, ssm_b_re, ssm_b_im, ssm_c_re, ssm_c_im, ssm_d, ssm_w_gate, ssm_w_out):
    mods = _ada(c, ada_w, ada_b)
    lng = ln_g.reshape(DEPTH, 3, 1, D_MODEL)
    lnb = ln_b.reshape(DEPTH, 3, 1, D_MODEL)
    w1, w3, w2 = ffn_w1, ffn_w3, ffn_w2
    gp = (SSM_GROUPS, SSM_GROUP)

    for layer in range(DEPTH):
        i = layer // 2
        proj = None
        x = _ffn(x, mods, w1, w3, w2, lng, lnb, layer=layer, half=0, sub=0)
        if layer % 2 == 0:
            lam_init = 0.8 - 0.6 * math.exp(-0.3 * layer)
            qkv = _modproj(x, mods, attn_w_in[i].astype(BF16), layer=layer, sub=1)
            o, w1, w3, w2 = _attention(qkv, attn_lam[i], attn_subln_g[i], ffn_w1, ffn_w3, ffn_w2,
                                       lam_init=lam_init)
            proj = (o, attn_w_out[i].astype(BF16), 1)
        else:
            rows_pg = lambda w: w.reshape(*gp, D_MODEL).transpose(1, 0, 2).reshape(D_MODEL, D_MODEL)
            w_in_t = rows_pg(ssm_w_in[i].astype(BF16).T)
            wg_t = rows_pg(rows_pg(ssm_w_gate[i].astype(BF16)).T)
            wo = rows_pg(ssm_w_out[i].astype(BF16))
            u4 = _s5in(x, mods, w_in_t, layer=layer)
            y4 = _s5scan(u4, ssm_a_re[i], ssm_a_im[i], ssm_log_dt[i], ssm_b_re[i], ssm_b_im[i],
                         ssm_c_re[i], ssm_c_im[i], ssm_d[i])
            x = _s5out(y4, x, mods, wg_t, wo, lng, lnb, layer=layer)
        x = _ffn(x, mods, w1, w3, w2, lng, lnb, layer=layer, half=1, sub=2, proj=proj)
    return x
```

```python
import functools
import math

import jax
import jax.numpy as jnp
from jax import lax
from jax.experimental import pallas as pl
from jax.experimental.pallas import tpu as pltpu

D_MODEL = 1024
DEPTH = 2
CHUNK = 64
ATTN_HEADS = 8
ATTN_HEAD_DIM = 64
SSM_GROUP = 16
SSM_GROUPS = D_MODEL // SSM_GROUP
SSM_STATE = 64
D_FF = 2816
ALPHA = (2 * DEPTH) ** 0.25
LN_EPS = 1e-5

LANES = 128
F_CHUNK = 256
MIB = 1024 * 1024

F32 = jnp.float32
BF16 = jnp.bfloat16


def _params(sem, vmem_mib):
    return pltpu.CompilerParams(dimension_semantics=sem, vmem_limit_bytes=vmem_mib * MIB)


def _resident(block_shape, index_map):
    return pl.BlockSpec(block_shape, index_map, pipeline_mode=pl.Buffered(1))


def _layer_norm(r, g, b):
    mu = jnp.mean(r, axis=-1, keepdims=True)
    d = r - mu
    var = jnp.mean(d * d, axis=-1, keepdims=True)
    return d * lax.rsqrt(var + LN_EPS) * g + b


def _mods(mod_ref, sub, bidx):
    shift = mod_ref[3 * sub + 0, pl.ds(bidx, 1), :]
    scale = mod_ref[3 * sub + 1, pl.ds(bidx, 1), :]
    gate = 1.0 + mod_ref[3 * sub + 2, pl.ds(bidx, 1), :]
    return shift, scale, gate


def _ada_kernel(c_ref, w_ref, b_ref, o_ref):
    c = c_ref[...]
    cond = (c * jax.nn.sigmoid(c)).astype(BF16)
    o_ref[...] = jnp.dot(cond, w_ref[...].astype(BF16), preferred_element_type=F32) + b_ref[...]


def _ada(c, ada_w, ada_b):
    bsz = c.shape[0]
    n_blk = ada_w.shape[2] // D_MODEL
    return pl.pallas_call(
        _ada_kernel,
        grid=(DEPTH, n_blk),
        in_specs=[
            pl.BlockSpec((bsz, D_MODEL), lambda l, n: (0, 0)),
            pl.BlockSpec((None, D_MODEL, D_MODEL), lambda l, n: (l, 0, n)),
            pl.BlockSpec((None, None, 1, D_MODEL), lambda l, n: (l, n, 0, 0)),
        ],
        out_specs=pl.BlockSpec((None, None, bsz, D_MODEL), lambda l, n: (l, n, 0, 0)),
        out_shape=jax.ShapeDtypeStruct((DEPTH, n_blk, bsz, D_MODEL), F32),
        compiler_params=_params(("arbitrary", "arbitrary"), 24),
        name="ada_mods",
    )(c, ada_w, ada_b.reshape(DEPTH, n_blk, 1, D_MODEL))


W_ROWS = 256


def _ffn_kernel(*refs, sub, ts, layer, half, proj_sub):
    if proj_sub is None:
        x_ref, mod_ref, w1_in, w3_in, w2_in, g_ref, b_ref, o_ref, *scratch = refs
        xin_ref = x_ref
    else:
        (x_ref, a_ref, wo_ref, gp_ref, bp_ref, mod_ref, w1_in, w3_in, w2_in, g_ref, b_ref, o_ref,
         xin_ref, *scratch) = refs
    shift, scale, gate = _mods(mod_ref, sub, pl.program_id(0))
    n_f = D_FF // F_CHUNK
    chunk = lambda f: slice(f * F_CHUNK, (f + 1) * F_CHUNK)
    w1_ref, w3_ref, w2_ref = scratch[:3] if scratch else (w1_in, w3_in, w2_in)

    def stream_weights():
        wide_ref, narrow_ref, sem = scratch[3:]
        chunks = [(src, dst, stage, kind, slice(r, r + W_ROWS))
                  for src, dst, stage, kind in ((w1_in, w1_ref, wide_ref, 0), (w3_in, w3_ref, wide_ref, 0),
                                                (w2_in, w2_ref, narrow_ref, 1))
                  for r in range(0, dst.shape[0], W_ROWS)]

        def copy(i):
            src, _, stage, kind, rows = chunks[i]
            return pltpu.make_async_copy(src.at[layer, half, rows, :], stage.at[i % 2], sem.at[kind, i % 2])

        copy(0).start()
        for i, (_, dst, stage, _, rows) in enumerate(chunks):
            if i + 1 < len(chunks):
                copy(i + 1).start()
            copy(i).wait()
            dst[rows, :] = stage[i % 2].astype(BF16)

    if scratch:
        pl.when(jnp.logical_and(pl.program_id(0) == 0, pl.program_id(1) == 0))(stream_weights)

    def modulated(rows):
        if proj_sub is not None:
            _, _, gate_p = _mods(mod_ref, proj_sub, pl.program_id(0))
            y = jnp.dot(a_ref[rows, :], wo_ref[...], preferred_element_type=F32)
            xin_ref[rows, :] = _layer_norm(ALPHA * x_ref[rows, :] + gate_p * y, gp_ref[...], bp_ref[...])
        return (xin_ref[rows, :] * (1.0 + scale) + shift).astype(BF16)

    def up(h, f):
        return (jnp.dot(h, w1_ref[:, chunk(f)], preferred_element_type=F32),
                jnp.dot(h, w3_ref[:, chunk(f)], preferred_element_type=F32))

    def down(ab, acc, f):
        a, b = ab
        u = (a * jax.nn.sigmoid(a) * b).astype(BF16)
        d = jnp.dot(u, w2_ref[chunk(f), :], preferred_element_type=F32)
        return d if acc is None else acc + d

    def epilogue(rows, acc):
        r = ALPHA * xin_ref[rows, :] + gate * (0.5 * acc)
        o_ref[rows, :] = _layer_norm(r, g_ref[...], b_ref[...])

    n_sub = x_ref.shape[0] // ts
    tile = lambda t: slice(t * ts, (t + 1) * ts)
    h = modulated(tile(0))
    nxt = up(h, 0)
    for t in range(n_sub):
        acc = None
        for f in range(n_f):
            cur = nxt
            if f + 1 < n_f:
                nxt = up(h, f + 1)
            elif t + 1 < n_sub:
                nxt = up(h_next, 0)
            if f == n_f // 2 and t + 1 < n_sub:
                h_next = modulated(tile(t + 1))
            acc = down(cur, acc, f)
        epilogue(tile(t), acc)
        if t + 1 < n_sub:
            h = h_next


def _ffn(x, mods, w1, w3, w2, ln_g, ln_b, *, layer, half, sub, proj=None, tm=1024, ts=256):
    bsz, seq, _ = x.shape
    row_block = pl.BlockSpec((None, tm, D_MODEL), lambda b, i: (b, i, 0))
    ln_spec = lambda s: _resident((None, None, 1, D_MODEL), lambda b, i: (layer, s, 0, 0))
    if w1.dtype == BF16:
        wspec = lambda r, c: _resident((None, None, r, c), lambda b, i: (layer, half, 0, 0))
        w_specs = [wspec(D_MODEL, D_FF), wspec(D_MODEL, D_FF), wspec(D_FF, D_MODEL)]
        scratch = []
    else:
        w_specs = [pl.BlockSpec(memory_space=pl.ANY)] * 3
        scratch = [
            pltpu.VMEM((D_MODEL, D_FF), BF16), pltpu.VMEM((D_MODEL, D_FF), BF16),
            pltpu.VMEM((D_FF, D_MODEL), BF16),
            pltpu.VMEM((2, W_ROWS, D_FF), F32), pltpu.VMEM((2, W_ROWS, D_MODEL), F32),
            pltpu.SemaphoreType.DMA((2, 2)),
        ]
    operands, in_specs = [x], [row_block]
    if proj is not None:
        a, wo, proj_sub = proj
        operands += [a, wo, ln_g, ln_b]
        in_specs += [row_block, _resident((D_MODEL, D_MODEL), lambda b, i: (0, 0)),
                     ln_spec(proj_sub), ln_spec(proj_sub)]
        scratch = [pltpu.VMEM((tm, D_MODEL), F32)] + scratch
    operands += [mods, w1, w3, w2, ln_g, ln_b]
    in_specs += [_resident((None, 9, bsz, D_MODEL), lambda b, i: (layer, 0, 0, 0)), *w_specs,
                 ln_spec(sub), ln_spec(sub)]
    return pl.pallas_call(
        functools.partial(_ffn_kernel, sub=sub, ts=ts, layer=layer, half=half,
                          proj_sub=None if proj is None else proj[2]),
        grid=(bsz, seq // tm),
        in_specs=in_specs,
        out_specs=row_block,
        out_shape=jax.ShapeDtypeStruct(x.shape, F32),
        scratch_shapes=scratch,
        compiler_params=_params(("arbitrary", "arbitrary"), 52),
        name=f"ffn_l{layer}_h{half}",
    )(*operands)


def _modproj_kernel(x_ref, mod_ref, w_ref, o_ref, *, sub):
    shift, scale, _ = _mods(mod_ref, sub, pl.program_id(0))
    h = (x_ref[...] * (1.0 + scale) + shift).astype(BF16)
    for n in range(o_ref.shape[1] // D_MODEL):
        cols = slice(n * D_MODEL, (n + 1) * D_MODEL)
        o_ref[:, cols] = jnp.dot(h, w_ref[:, cols], preferred_element_type=F32).astype(o_ref.dtype)


def _modproj(x, mods, w, *, layer, sub, tm=1024):
    bsz, seq, _ = x.shape
    n_out = w.shape[1]
    return pl.pallas_call(
        functools.partial(_modproj_kernel, sub=sub),
        grid=(bsz, seq // tm),
        in_specs=[
            pl.BlockSpec((None, tm, D_MODEL), lambda b, i: (b, i, 0)),
            _resident((None, 9, bsz, D_MODEL), lambda b, i: (layer, 0, 0, 0)),
            _resident((D_MODEL, n_out), lambda b, i: (0, 0)),
        ],
        out_specs=pl.BlockSpec((None, tm, n_out), lambda b, i: (b, i, 0)),
        out_shape=jax.ShapeDtypeStruct((bsz, seq, n_out), BF16),
        compiler_params=_params(("arbitrary", "arbitrary"), 44),
        name=f"modproj_l{layer}",
    )(x, mods, w)


def _attn_kernel(q_ref, k_ref, v_ref, lam_ref, sg_ref, w1_ref, w3_ref, w2_ref,
                 o_ref, w1b_ref, w3b_ref, w2b_ref, vx_ref, *, lam_init, tq, w2_span):
    seq = q_ref.shape[0]
    hd = ATTN_HEAD_DIM
    lam = lam_ref[...]
    lam_full = (jnp.exp(jnp.sum(lam[0:1] * lam[1:2], axis=-1, keepdims=True))
                - jnp.exp(jnp.sum(lam[2:3] * lam[3:4], axis=-1, keepdims=True)) + lam_init)
    lane = lax.broadcasted_iota(jnp.int32, (tq, 2 * hd), 1)
    rq = lax.broadcasted_iota(jnp.int32, (tq, tq), 0) // CHUNK
    ck = lax.broadcasted_iota(jnp.int32, (tq, tq), 1) // CHUNK
    allowed = ck <= rq
    nt = (((1,), (1,)), ((), ()))

    hw = 2 * hd
    vx_ref[:, :hw] = v_ref[...]
    vx_ref[:, hw:] = jnp.ones((seq, hw), BF16)

    def score(i, m):
        kend = (i + 1) * tq
        q = q_ref[i * tq:kend, :].astype(F32) * (hd ** -0.5)
        qm = jnp.where((lane < hd) if m == 0 else (lane >= hd), q, 0.0).astype(BF16)
        s = lax.dot_general(qm, k_ref[0:kend, :], nt, preferred_element_type=F32)
        diag = jnp.where(allowed, s[:, kend - tq:], -jnp.inf)
        return diag if kend == tq else jnp.concatenate([s[:, :kend - tq], diag], axis=1)

    def attend(i, s):
        kend = (i + 1) * tq
        p = jnp.exp(s - jnp.max(s, axis=-1, keepdims=True))
        r = jnp.dot(p.astype(BF16), vx_ref[0:kend, :], preferred_element_type=F32)
        return r[:, :hw] / r[:, hw:]

    def finish(i, o1, o2):
        o = o1 - lam_full * o2
        o = o * lax.rsqrt(jnp.mean(o * o, axis=-1, keepdims=True) + LN_EPS)
        o_ref[i * tq:(i + 1) * tq, :] = (o * sg_ref[...] * (1.0 - lam_init)).astype(o_ref.dtype)

    n = seq // tq
    units = [(i, m) for i in reversed(range(n)) for m in range(2)]
    pend = score(*units[0])
    outs = {}
    for u, (i, m) in enumerate(units):
        cur = pend
        if u + 1 < len(units):
            pend = score(*units[u + 1])
        outs[(i, m)] = attend(i, cur)
        if m == 1:
            finish(i, outs.pop((i, 0)), outs.pop((i, 1)))

    w1b_ref[...] = w1_ref[...].astype(BF16)
    w3b_ref[...] = w3_ref[...].astype(BF16)

    @pl.when((pl.program_id(0) * pl.num_programs(1) + pl.program_id(1)) % w2_span == 0)
    def _():
        w2b_ref[...] = w2_ref[...].astype(BF16)


def _attention(qkv, lam, subln_g, w1, w3, w2, *, lam_init, tq=256):
    bsz, seq, _ = qkv.shape
    hw = 2 * ATTN_HEAD_DIM
    steps = bsz * ATTN_HEADS
    up_rows = D_MODEL // steps
    down_blocks = 16
    down_rows = D_FF // down_blocks
    up_spec = pl.BlockSpec((DEPTH, 2, up_rows, D_FF), lambda b, h: (0, 0, b * ATTN_HEADS + h, 0))
    down_spec = pl.BlockSpec((DEPTH, 2, down_rows, D_MODEL),
                             lambda b, h: (0, 0, (b * ATTN_HEADS + h) // (steps // down_blocks), 0))
    return pl.pallas_call(
        functools.partial(_attn_kernel, lam_init=lam_init, tq=tq, w2_span=steps // down_blocks),
        grid=(bsz, ATTN_HEADS),
        in_specs=[
            pl.BlockSpec((None, seq, hw), lambda b, h: (b, 0, h)),
            pl.BlockSpec((None, seq, hw), lambda b, h: (b, 0, ATTN_HEADS + h)),
            pl.BlockSpec((None, seq, hw), lambda b, h: (b, 0, 2 * ATTN_HEADS + h)),
            _resident((4, ATTN_HEAD_DIM), lambda b, h: (0, 0)),
            _resident((1, hw), lambda b, h: (0, 0)),
            up_spec, up_spec, down_spec,
        ],
        out_specs=[pl.BlockSpec((None, seq, hw), lambda b, h: (b, 0, h)), up_spec, up_spec, down_spec],
        out_shape=[jax.ShapeDtypeStruct((bsz, seq, D_MODEL), BF16), jax.ShapeDtypeStruct(w1.shape, BF16),
                   jax.ShapeDtypeStruct(w3.shape, BF16), jax.ShapeDtypeStruct(w2.shape, BF16)],
        scratch_shapes=[pltpu.VMEM((seq, 2 * hw), BF16)],
        compiler_params=_params(("arbitrary", "arbitrary"), 48),
        name="diff_attention",
    )(qkv, qkv, qkv, lam, subln_g.reshape(1, hw), w1, w3, w2)


ROWS = 2 * CHUNK
PAIRS = SSM_GROUP // 2


def _lane_halves(e, o, lane):
    lo = jnp.where(lane < CHUNK, e, pltpu.roll(o, CHUNK, 1))
    hi = jnp.where(lane < CHUNK, pltpu.roll(e, CHUNK, 1), o)
    return lo, hi


def _s5in_kernel(x_ref, mod_ref, w_ref, o_ref, a_ref, s_ref):
    bsz = x_ref.shape[0]
    shift = mod_ref[3, :, :][:, None, :]
    scale = mod_ref[4, :, :][:, None, :]
    h = (x_ref[...] * (1.0 + scale) + shift).astype(BF16).reshape(bsz * ROWS, D_MODEL)
    a_ref[...] = lax.dot_general(w_ref[...], h, (((1,), (1,)), ((), ())), preferred_element_type=F32)
    lane = lax.broadcasted_iota(jnp.int32, (SSM_GROUPS, LANES), 1)
    half = SSM_GROUPS * bsz
    for q in range(PAIRS):
        for b in range(bsz):
            x0 = a_ref[(2 * q) * SSM_GROUPS:(2 * q + 1) * SSM_GROUPS, b * LANES:(b + 1) * LANES]
            x1 = a_ref[(2 * q + 1) * SSM_GROUPS:(2 * q + 2) * SSM_GROUPS, b * LANES:(b + 1) * LANES]
            c0, c1 = _lane_halves(x0, x1, lane)
            s_ref[q, pl.ds(b, SSM_GROUPS, stride=bsz), :] = c0
            s_ref[q, pl.ds(half + b, SSM_GROUPS, stride=bsz), :] = c1
    for c2 in range(2):
        for q in range(PAIRS):
            o_ref[c2, :, :, q * LANES:(q + 1) * LANES] = (
                s_ref[q, c2 * half:(c2 + 1) * half, :].reshape(SSM_GROUPS, bsz, LANES))


def _s5in(x, mods, w_t, *, layer):
    bsz, seq, _ = x.shape
    width = SSM_GROUP * CHUNK
    return pl.pallas_call(
        _s5in_kernel,
        grid=(seq // ROWS,),
        in_specs=[
            pl.BlockSpec((bsz, ROWS, D_MODEL), lambda j: (0, j, 0)),
            _resident((None, 9, bsz, D_MODEL), lambda j: (layer, 0, 0, 0)),
            _resident((D_MODEL, D_MODEL), lambda j: (0, 0)),
        ],
        out_specs=pl.BlockSpec((2, SSM_GROUPS, bsz, width), lambda j: (j, 0, 0, 0)),
        out_shape=jax.ShapeDtypeStruct((seq // CHUNK, SSM_GROUPS, bsz, width), F32),
        scratch_shapes=[
            pltpu.VMEM((D_MODEL, bsz * ROWS), F32),
            pltpu.VMEM((PAIRS, 2 * SSM_GROUPS * bsz, LANES), F32),
        ],
        compiler_params=_params(("arbitrary",), 40),
        name="s5_in",
    )(x, mods, w_t)


def _s5scan_kernel(u_ref, arr_ref, air_ref, arc_ref, aic_ref, ldt_ref, btr_ref, bti_ref,
                   ctr_ref, cti_ref, d_ref, y_ref, t_ref, ws_ref, wc_ref):
    n_chunks, bsz, width = u_ref.shape
    st = SSM_STATE
    dt = jnp.exp(ldt_ref[...])

    ar, ai = arr_ref[...], air_ref[...]
    mag = jnp.exp(ar * dt)
    abr, abi = mag * jnp.cos(ai * dt), mag * jnp.sin(ai * dt)
    den = ar * ar + ai * ai
    pr, pim = abr - 1.0, abi
    cfr, cfi = (pr * ar + pim * ai) / den, (pim * ar - pr * ai) / den
    btr, bti = btr_ref[...], bti_ref[...]
    bbr, bbi = cfr * btr - cfi * bti, cfr * bti + cfi * btr

    arc, aic = arc_ref[...], aic_ref[...]
    lane = lax.broadcasted_iota(jnp.int32, (1, LANES), 1)
    lag = (lane % CHUNK).astype(F32)
    first = lane < CHUNK
    ctr, cti = ctr_ref[...], cti_ref[...]

    def c_times_power(shift):
        e = lag + shift
        m = jnp.exp(arc * dt * e)
        er, ei = m * jnp.cos(aic * dt * e), m * jnp.sin(aic * dt * e)
        xr, xi = [], []
        for q in range(PAIRS):
            cr = jnp.where(first, ctr[:, 2 * q:2 * q + 1], ctr[:, 2 * q + 1:2 * q + 2])
            ci = jnp.where(first, cti[:, 2 * q:2 * q + 1], cti[:, 2 * q + 1:2 * q + 2])
            xr.append(cr * er - ci * ei)
            xi.append(cr * ei + ci * er)
        return jnp.concatenate(xr, axis=1), jnp.concatenate(xi, axis=1)

    xr, xi = c_times_power(0.0)
    hi = lax.Precision.HIGHEST
    kflat = (jnp.dot(bbr, xr, precision=hi, preferred_element_type=F32)
             - jnp.dot(bbi, xi, precision=hi, preferred_element_type=F32))
    prow = lax.broadcasted_iota(jnp.int32, kflat.shape, 0)
    plane = lax.broadcasted_iota(jnp.int32, kflat.shape, 1)
    kflat = kflat + jnp.where(plane == prow * CHUNK, d_ref[...], 0.0)

    srow = lax.broadcasted_iota(jnp.int32, (CHUNK, LANES), 0)
    keep = (lax.broadcasted_iota(jnp.int32, (CHUNK, LANES), 1) % CHUNK) >= srow
    for p in range(SSM_GROUP):
        rows = jnp.broadcast_to(kflat[p:p + 1, :], (CHUNK, width))
        for q in range(PAIRS):
            blk = pltpu.roll(rows[:, q * LANES:(q + 1) * LANES], 0, 1, stride=1, stride_axis=0)
            t_ref[p * CHUNK:(p + 1) * CHUNK, q * LANES:(q + 1) * LANES] = (
                jnp.where(keep, blk, 0.0).astype(BF16))

    e = (CHUNK - 1 - lax.broadcasted_iota(jnp.int32, (CHUNK, 1), 0)).astype(F32)
    m = jnp.exp(ar * dt * e)
    er, ei = m * jnp.cos(ai * dt * e), m * jnp.sin(ai * dt * e)
    zpad = jnp.zeros((CHUNK, LANES - st), F32)
    for p in range(SSM_GROUP):
        br, bi = bbr[p:p + 1, :], bbi[p:p + 1, :]
        ws_ref[p * CHUNK:(p + 1) * CHUNK, :] = jnp.concatenate(
            [er * br - ei * bi, zpad, er * bi + ei * br, zpad], axis=1).astype(BF16)

    pr_, pi_ = c_times_power(1.0)
    zrows = jnp.zeros((LANES - st, width), F32)
    wc_ref[...] = jnp.concatenate([pr_, zrows, -pi_, zrows], axis=0).astype(BF16)

    u = u_ref[...].reshape(n_chunks * bsz, width).astype(BF16)
    s = jnp.dot(u, ws_ref[...], preferred_element_type=F32)
    m64 = jnp.exp(ar * dt * CHUNK)
    zlane = jnp.zeros((1, LANES - st), F32)
    a64r = jnp.concatenate([m64 * jnp.cos(ai * dt * CHUNK), zlane], axis=1)
    a64i = jnp.concatenate([m64 * jnp.sin(ai * dt * CHUNK), zlane], axis=1)
    hr = jnp.zeros((bsz, LANES), F32)
    hi_ = jnp.zeros((bsz, LANES), F32)
    prev = []
    for c in range(n_chunks):
        prev.append(jnp.concatenate([hr, hi_], axis=1))
        sr, si = s[c * bsz:(c + 1) * bsz, :LANES], s[c * bsz:(c + 1) * bsz, LANES:]
        hr, hi_ = a64r * hr - a64i * hi_ + sr, a64r * hi_ + a64i * hr + si
    hprev = jnp.concatenate(prev, axis=0).astype(BF16)

    y = (jnp.dot(u, t_ref[...], preferred_element_type=F32)
         + jnp.dot(hprev, wc_ref[...], preferred_element_type=F32))
    y_ref[...] = y.reshape(n_chunks, bsz, width)


def _s5scan(u4, a_re, a_im, log_dt, b_re, b_im, c_re, c_im, d):
    n_chunks, groups, bsz, width = u4.shape
    st, pg = SSM_STATE, SSM_GROUP
    per_g = lambda *shape: pl.BlockSpec((None,) + shape, lambda g: (g,) + (0,) * len(shape))
    return pl.pallas_call(
        _s5scan_kernel,
        grid=(groups,),
        in_specs=[
            pl.BlockSpec((n_chunks, None, bsz, width), lambda g: (0, g, 0, 0)),
            per_g(1, st), per_g(1, st), per_g(st, 1), per_g(st, 1), per_g(1, 1),
            per_g(pg, st), per_g(pg, st), per_g(st, pg), per_g(st, pg), per_g(pg, 1),
        ],
        out_specs=pl.BlockSpec((n_chunks, None, bsz, width), lambda g: (0, g, 0, 0)),
        out_shape=jax.ShapeDtypeStruct(u4.shape, F32),
        scratch_shapes=[
            pltpu.VMEM((width, width), BF16),
            pltpu.VMEM((width, 2 * LANES), BF16),
            pltpu.VMEM((2 * LANES, width), BF16),
        ],
        compiler_params=_params(("arbitrary",), 32),
        name="s5_scan",
    )(u4,
      a_re.reshape(groups, 1, st), a_im.reshape(groups, 1, st),
      a_re.reshape(groups, st, 1), a_im.reshape(groups, st, 1),
      log_dt.reshape(groups, 1, 1),
      jnp.swapaxes(b_re, 1, 2), jnp.swapaxes(b_im, 1, 2),
      jnp.swapaxes(c_re, 1, 2), jnp.swapaxes(c_im, 1, 2),
      d.reshape(groups, pg, 1))


def _s5out_kernel(y_ref, x_ref, mod_ref, wg_ref, wo_ref, g_ref, b_ref, o_ref, a_ref, s_ref):
    bsz = x_ref.shape[0]
    half = SSM_GROUPS * bsz
    for c2 in range(2):
        for q in range(PAIRS):
            s_ref[q, c2 * half:(c2 + 1) * half, :] = (
                y_ref[c2, :, :, q * LANES:(q + 1) * LANES].reshape(half, LANES))
    lane = lax.broadcasted_iota(jnp.int32, (SSM_GROUPS, LANES), 1)
    n_grp = 4
    per = bsz // n_grp

    def gather(k):
        for q in range(PAIRS):
            for b in range(k * per, (k + 1) * per):
                c0 = s_ref[q, pl.ds(b, SSM_GROUPS, stride=bsz), :]
                c1 = s_ref[q, pl.ds(half + b, SSM_GROUPS, stride=bsz), :]
                x0, x1 = _lane_halves(c0, c1, lane)
                a_ref[(2 * q) * SSM_GROUPS:(2 * q + 1) * SSM_GROUPS, b * LANES:(b + 1) * LANES] = x0
                a_ref[(2 * q + 1) * SSM_GROUPS:(2 * q + 2) * SSM_GROUPS, b * LANES:(b + 1) * LANES] = x1
        return jax.nn.gelu(a_ref[:, k * per * LANES:(k + 1) * per * LANES], approximate=True)

    def gate_mm(z):
        return jnp.dot(wg_ref[...], z.astype(BF16), preferred_element_type=F32)

    def out_mm(z, gt):
        zz = (z * jax.nn.sigmoid(gt)).astype(BF16)
        return lax.dot_general(zz, wo_ref[...], (((0,), (0,)), ((), ())), preferred_element_type=F32)

    def finish(k, y):
        rows = slice(k * per, (k + 1) * per)
        y = y.reshape(per, ROWS, D_MODEL)
        gate = 1.0 + mod_ref[5, rows, :][:, None, :]
        o_ref[rows] = _layer_norm(ALPHA * x_ref[rows] + gate * y, g_ref[...], b_ref[...])

    zs, gs, ys = {}, {}, {}
    for step in range(n_grp + 3):
        if step < n_grp:
            zs[step] = gather(step)
        if 0 <= step - 1 < n_grp:
            gs[step - 1] = gate_mm(zs[step - 1])
        if 0 <= step - 2 < n_grp:
            ys[step - 2] = out_mm(zs.pop(step - 2), gs.pop(step - 2))
        if 0 <= step - 3 < n_grp:
            finish(step - 3, ys.pop(step - 3))


def _s5out(y4, x, mods, wg_t, wo, ln_g, ln_b, *, layer):
    bsz, seq, _ = x.shape
    width = SSM_GROUP * CHUNK
    return pl.pallas_call(
        _s5out_kernel,
        grid=(seq // ROWS,),
        in_specs=[
            pl.BlockSpec((2, SSM_GROUPS, bsz, width), lambda j: (j, 0, 0, 0)),
            pl.BlockSpec((bsz, ROWS, D_MODEL), lambda j: (0, j, 0)),
            _resident((None, 9, bsz, D_MODEL), lambda j: (layer, 0, 0, 0)),
            _resident((D_MODEL, D_MODEL), lambda j: (0, 0)),
            _resident((D_MODEL, D_MODEL), lambda j: (0, 0)),
            _resident((None, None, 1, D_MODEL), lambda j: (layer, 1, 0, 0)),
            _resident((None, None, 1, D_MODEL), lambda j: (layer, 1, 0, 0)),
        ],
        out_specs=pl.BlockSpec((bsz, ROWS, D_MODEL), lambda j: (0, j, 0)),
        out_shape=jax.ShapeDtypeStruct(x.shape, F32),
        scratch_shapes=[
            pltpu.VMEM((D_MODEL, bsz * ROWS), F32),
            pltpu.VMEM((PAIRS, 2 * SSM_GROUPS * bsz, LANES), F32),
        ],
        compiler_params=_params(("arbitrary",), 48),
        name="s5_out",
    )(y4, x, mods, wg_t, wo, ln_g, ln_b)


def kernel(x, c, ada_w, ada_b, ln_g, ln_b, ffn_w1, ffn_w3, ffn_w2, attn_w_in, attn_lam, attn_subln_g, attn_w_out, ssm_w_in, ssm_a_re, ssm_a_im, ssm_log_dt, ssm_b_re, ssm_b_im, ssm_c_re, ssm_c_im, ssm_d, ssm_w_gate, ssm_w_out):
    mods = _ada(c, ada_w, ada_b)
    lng = ln_g.reshape(DEPTH, 3, 1, D_MODEL)
    lnb = ln_b.reshape(DEPTH, 3, 1, D_MODEL)
    w1, w3, w2 = ffn_w1, ffn_w3, ffn_w2
    gp = (SSM_GROUPS, SSM_GROUP)

    for layer in range(DEPTH):
        i = layer // 2
        proj = None
        x = _ffn(x, mods, w1, w3, w2, lng, lnb, layer=layer, half=0, sub=0)
        if layer % 2 == 0:
            lam_init = 0.8 - 0.6 * math.exp(-0.3 * layer)
            qkv = _modproj(x, mods, attn_w_in[i].astype(BF16), layer=layer, sub=1)
            o, w1, w3, w2 = _attention(qkv, attn_lam[i], attn_subln_g[i], ffn_w1, ffn_w3, ffn_w2,
                                       lam_init=lam_init)
            proj = (o, attn_w_out[i].astype(BF16), 1)
        else:
            rows_pg = lambda w: w.reshape(*gp, D_MODEL).transpose(1, 0, 2).reshape(D_MODEL, D_MODEL)
            w_in_t = rows_pg(ssm_w_in[i].astype(BF16).T)
            wg_t = rows_pg(rows_pg(ssm_w_gate[i].astype(BF16)).T)
            wo = rows_pg(ssm_w_out[i].astype(BF16))
            u4 = _s5in(x, mods, w_in_t, layer=layer)
            y4 = _s5scan(u4, ssm_a_re[i], ssm_a_im[i], ssm_log_dt[i], ssm_b_re[i], ssm_b_im[i],
                         ssm_c_re[i], ssm_c_im[i], ssm_d[i])
            x = _s5out(y4, x, mods, wg_t, wo, lng, lnb, layer=layer)
        x = _ffn(x, mods, w1, w3, w2, lng, lnb, layer=layer, half=1, sub=2, proj=proj)
    return x
```

```python
import functools
import math

import jax
import jax.numpy as jnp
from jax import lax
from jax.experimental import pallas as pl
from jax.experimental.pallas import tpu as pltpu

D_MODEL = 1024
DEPTH = 2
CHUNK = 64
ATTN_HEADS = 8
ATTN_HEAD_DIM = 64
SSM_GROUP = 16
SSM_GROUPS = D_MODEL // SSM_GROUP
SSM_STATE = 64
D_FF = 2816
ALPHA = (2 * DEPTH) ** 0.25
LN_EPS = 1e-5

LANES = 128
F_CHUNK = 256
MIB = 1024 * 1024

F32 = jnp.float32
BF16 = jnp.bfloat16


def _params(sem, vmem_mib):
    return pltpu.CompilerParams(dimension_semantics=sem, vmem_limit_bytes=vmem_mib * MIB)


def _resident(block_shape, index_map):
    return pl.BlockSpec(block_shape, index_map, pipeline_mode=pl.Buffered(1))


def _layer_norm(r, g, b):
    mu = jnp.mean(r, axis=-1, keepdims=True)
    d = r - mu
    var = jnp.mean(d * d, axis=-1, keepdims=True)
    return d * lax.rsqrt(var + LN_EPS) * g + b


def _mods(mod_ref, sub, bidx):
    shift = mod_ref[3 * sub + 0, pl.ds(bidx, 1), :]
    scale = mod_ref[3 * sub + 1, pl.ds(bidx, 1), :]
    gate = 1.0 + mod_ref[3 * sub + 2, pl.ds(bidx, 1), :]
    return shift, scale, gate


def _ada_kernel(c_ref, w_ref, b_ref, o_ref):
    c = c_ref[...]
    cond = (c * jax.nn.sigmoid(c)).astype(BF16)
    o_ref[...] = jnp.dot(cond, w_ref[...].astype(BF16), preferred_element_type=F32) + b_ref[...]


def _ada(c, ada_w, ada_b):
    bsz = c.shape[0]
    n_blk = ada_w.shape[2] // D_MODEL
    return pl.pallas_call(
        _ada_kernel,
        grid=(DEPTH, n_blk),
        in_specs=[
            pl.BlockSpec((bsz, D_MODEL), lambda l, n: (0, 0)),
            pl.BlockSpec((None, D_MODEL, D_MODEL), lambda l, n: (l, 0, n)),
            pl.BlockSpec((None, None, 1, D_MODEL), lambda l, n: (l, n, 0, 0)),
        ],
        out_specs=pl.BlockSpec((None, None, bsz, D_MODEL), lambda l, n: (l, n, 0, 0)),
        out_shape=jax.ShapeDtypeStruct((DEPTH, n_blk, bsz, D_MODEL), F32),
        compiler_params=_params(("arbitrary", "arbitrary"), 24),
        name="ada_mods",
    )(c, ada_w, ada_b.reshape(DEPTH, n_blk, 1, D_MODEL))


W_ROWS = 256


def _ffn_kernel(*refs, sub, ts, layer, half, proj_sub):
    if proj_sub is None:
        x_ref, mod_ref, w1_in, w3_in, w2_in, g_ref, b_ref, o_ref, *scratch = refs
        xin_ref = x_ref
    else:
        (x_ref, a_ref, wo_ref, gp_ref, bp_ref, mod_ref, w1_in, w3_in, w2_in, g_ref, b_ref, o_ref,
         xin_ref, *scratch) = refs
    shift, scale, gate = _mods(mod_ref, sub, pl.program_id(0))
    n_f = D_FF // F_CHUNK
    chunk = lambda f: slice(f * F_CHUNK, (f + 1) * F_CHUNK)
    w1_ref, w3_ref, w2_ref = scratch[:3] if scratch else (w1_in, w3_in, w2_in)

    def stream_weights():
        wide_ref, narrow_ref, sem = scratch[3:]
        chunks = [(src, dst, stage, kind, slice(r, r + W_ROWS))
                  for src, dst, stage, kind in ((w1_in, w1_ref, wide_ref, 0), (w3_in, w3_ref, wide_ref, 0),
                                                (w2_in, w2_ref, narrow_ref, 1))
                  for r in range(0, dst.shape[0], W_ROWS)]

        def copy(i):
            src, _, stage, kind, rows = chunks[i]
            return pltpu.make_async_copy(src.at[layer, half, rows, :], stage.at[i % 2], sem.at[kind, i % 2])

        copy(0).start()
        for i, (_, dst, stage, _, rows) in enumerate(chunks):
            if i + 1 < len(chunks):
                copy(i + 1).start()
            copy(i).wait()
            dst[rows, :] = stage[i % 2].astype(BF16)

    if scratch:
        pl.when(jnp.logical_and(pl.program_id(0) == 0, pl.program_id(1) == 0))(stream_weights)

    def modulated(rows):
        if proj_sub is not None:
            _, _, gate_p = _mods(mod_ref, proj_sub, pl.program_id(0))
            y = jnp.dot(a_ref[rows, :], wo_ref[...], preferred_element_type=F32)
            xin_ref[rows, :] = _layer_norm(ALPHA * x_ref[rows, :] + gate_p * y, gp_ref[...], bp_ref[...])
        return (xin_ref[rows, :] * (1.0 + scale) + shift).astype(BF16)

    def up(h, f):
        return (jnp.dot(h, w1_ref[:, chunk(f)], preferred_element_type=F32),
                jnp.dot(h, w3_ref[:, chunk(f)], preferred_element_type=F32))

    def down(ab, acc, f):
        a, b = ab
        u = (a * jax.nn.sigmoid(a) * b).astype(BF16)
        d = jnp.dot(u, w2_ref[chunk(f), :], preferred_element_type=F32)
        return d if acc is None else acc + d

    def epilogue(rows, acc):
        r = ALPHA * xin_ref[rows, :] + gate * (0.5 * acc)
        o_ref[rows, :] = _layer_norm(r, g_ref[...], b_ref[...])

    n_sub = x_ref.shape[0] // ts
    tile = lambda t: slice(t * ts, (t + 1) * ts)
    h = modulated(tile(0))
    nxt = up(h, 0)
    for t in range(n_sub):
        acc = None
        for f in range(n_f):
            cur = nxt
            if f + 1 < n_f:
                nxt = up(h, f + 1)
            elif t + 1 < n_sub:
                nxt = up(h_next, 0)
            if f == n_f // 2 and t + 1 < n_sub:
                h_next = modulated(tile(t + 1))
            acc = down(cur, acc, f)
        epilogue(tile(t), acc)
        if t + 1 < n_sub:
            h = h_next


def _ffn(x, mods, w1, w3, w2, ln_g, ln_b, *, layer, half, sub, proj=None, tm=1024, ts=256):
    bsz, seq, _ = x.shape
    row_block = pl.BlockSpec((None, tm, D_MODEL), lambda b, i: (b, i, 0))
    ln_spec = lambda s: _resident((None, None, 1, D_MODEL), lambda b, i: (layer, s, 0, 0))
    if w1.dtype == BF16:
        wspec = lambda r, c: _resident((None, None, r, c), lambda b, i: (layer, half, 0, 0))
        w_specs = [wspec(D_MODEL, D_FF), wspec(D_MODEL, D_FF), wspec(D_FF, D_MODEL)]
        scratch = []
    else:
        w_specs = [pl.BlockSpec(memory_space=pl.ANY)] * 3
        scratch = [
            pltpu.VMEM((D_MODEL, D_FF), BF16), pltpu.VMEM((D_MODEL, D_FF), BF16),
            pltpu.VMEM((D_FF, D_MODEL), BF16),
            pltpu.VMEM((2, W_ROWS, D_FF), F32), pltpu.VMEM((2, W_ROWS, D_MODEL), F32),
            pltpu.SemaphoreType.DMA((2, 2)),
        ]
    operands, in_specs = [x], [row_block]
    if proj is not None:
        a, wo, proj_sub = proj
        operands += [a, wo, ln_g, ln_b]
        in_specs += [row_block, _resident((D_MODEL, D_MODEL), lambda b, i: (0, 0)),
                     ln_spec(proj_sub), ln_spec(proj_sub)]
        scratch = [pltpu.VMEM((tm, D_MODEL), F32)] + scratch
    operands += [mods, w1, w3, w2, ln_g, ln_b]
    in_specs += [_resident((None, 9, bsz, D_MODEL), lambda b, i: (layer, 0, 0, 0)), *w_specs,
                 ln_spec(sub), ln_spec(sub)]
    return pl.pallas_call(
        functools.partial(_ffn_kernel, sub=sub, ts=ts, layer=layer, half=half,
                          proj_sub=None if proj is None else proj[2]),
        grid=(bsz, seq // tm),
        in_specs=in_specs,
        out_specs=row_block,
        out_shape=jax.ShapeDtypeStruct(x.shape, F32),
        scratch_shapes=scratch,
        compiler_params=_params(("arbitrary", "arbitrary"), 52),
        name=f"ffn_l{layer}_h{half}",
    )(*operands)


def _modproj_kernel(x_ref, mod_ref, w_ref, o_ref, *, sub):
    shift, scale, _ = _mods(mod_ref, sub, pl.program_id(0))
    h = (x_ref[...] * (1.0 + scale) + shift).astype(BF16)
    for n in range(o_ref.shape[1] // D_MODEL):
        cols = slice(n * D_MODEL, (n + 1) * D_MODEL)
        o_ref[:, cols] = jnp.dot(h, w_ref[:, cols], preferred_element_type=F32).astype(o_ref.dtype)


def _modproj(x, mods, w, *, layer, sub, tm=1024):
    bsz, seq, _ = x.shape
    n_out = w.shape[1]
    return pl.pallas_call(
        functools.partial(_modproj_kernel, sub=sub),
        grid=(bsz, seq // tm),
        in_specs=[
            pl.BlockSpec((None, tm, D_MODEL), lambda b, i: (b, i, 0)),
            _resident((None, 9, bsz, D_MODEL), lambda b, i: (layer, 0, 0, 0)),
            _resident((D_MODEL, n_out), lambda b, i: (0, 0)),
        ],
        out_specs=pl.BlockSpec((None, tm, n_out), lambda b, i: (b, i, 0)),
        out_shape=jax.ShapeDtypeStruct((bsz, seq, n_out), BF16),
        compiler_params=_params(("arbitrary", "arbitrary"), 44),
        name=f"modproj_l{layer}",
    )(x, mods, w)


def _attn_kernel(q_ref, k_ref, v_ref, lam_ref, sg_ref, w1_ref, w3_ref, w2_ref,
                 o_ref, w1b_ref, w3b_ref, w2b_ref, vx_ref, *, lam_init, tq, w2_span):
    seq = q_ref.shape[0]
    hd = ATTN_HEAD_DIM
    lam = lam_ref[...]
    lam_full = (jnp.exp(jnp.sum(lam[0:1] * lam[1:2], axis=-1, keepdims=True))
                - jnp.exp(jnp.sum(lam[2:3] * lam[3:4], axis=-1, keepdims=True)) + lam_init)
    lane = lax.broadcasted_iota(jnp.int32, (tq, 2 * hd), 1)
    rq = lax.broadcasted_iota(jnp.int32, (tq, tq), 0) // CHUNK
    ck = lax.broadcasted_iota(jnp.int32, (tq, tq), 1) // CHUNK
    allowed = ck <= rq
    nt = (((1,), (1,)), ((), ()))

    hw = 2 * hd
    vx_ref[:, :hw] = v_ref[...]
    vx_ref[:, hw:] = jnp.ones((seq, hw), BF16)

    def score(i, m):
        kend = (i + 1) * tq
        q = q_ref[i * tq:kend, :].astype(F32) * (hd ** -0.5)
        qm = jnp.where((lane < hd) if m == 0 else (lane >= hd), q, 0.0).astype(BF16)
        s = lax.dot_general(qm, k_ref[0:kend, :], nt, preferred_element_type=F32)
        diag = jnp.where(allowed, s[:, kend - tq:], -jnp.inf)
        return diag if kend == tq else jnp.concatenate([s[:, :kend - tq], diag], axis=1)

    def attend(i, s):
        kend = (i + 1) * tq
        p = jnp.exp(s - jnp.max(s, axis=-1, keepdims=True))
        r = jnp.dot(p.astype(BF16), vx_ref[0:kend, :], preferred_element_type=F32)
        return r[:, :hw] / r[:, hw:]

    def finish(i, o1, o2):
        o = o1 - lam_full * o2
        o = o * lax.rsqrt(jnp.mean(o * o, axis=-1, keepdims=True) + LN_EPS)
        o_ref[i * tq:(i + 1) * tq, :] = (o * sg_ref[...] * (1.0 - lam_init)).astype(o_ref.dtype)

    n = seq // tq
    units = [(i, m) for i in reversed(range(n)) for m in range(2)]
    pend = score(*units[0])
    outs = {}
    for u, (i, m) in enumerate(units):
        cur = pend
        if u + 1 < len(units):
            pend = score(*units[u + 1])
        outs[(i, m)] = attend(i, cur)
        if m == 1:
            finish(i, outs.pop((i, 0)), outs.pop((i, 1)))

    w1b_ref[...] = w1_ref[...].astype(BF16)
    w3b_ref[...] = w3_ref[...].astype(BF16)

    @pl.when((pl.program_id(0) * pl.num_programs(1) + pl.program_id(1)) % w2_span == 0)
    def _():
        w2b_ref[...] = w2_ref[...].astype(BF16)


def _attention(qkv, lam, subln_g, w1, w3, w2, *, lam_init, tq=256):
    bsz, seq, _ = qkv.shape
    hw = 2 * ATTN_HEAD_DIM
    steps = bsz * ATTN_HEADS
    up_rows = D_MODEL // steps
    down_blocks = 16
    down_rows = D_FF // down_blocks
    up_spec = pl.BlockSpec((DEPTH, 2, up_rows, D_FF), lambda b, h: (0, 0, b * ATTN_HEADS + h, 0))
    down_spec = pl.BlockSpec((DEPTH, 2, down_rows, D_MODEL),
                             lambda b, h: (0, 0, (b * ATTN_HEADS + h) // (steps // down_blocks), 0))
    return pl.pallas_call(
        functools.partial(_attn_kernel, lam_init=lam_init, tq=tq, w2_span=steps // down_blocks),
        grid=(bsz, ATTN_HEADS),
        in_specs=[
            pl.BlockSpec((None, seq, hw), lambda b, h: (b, 0, h)),
            pl.BlockSpec((None, seq, hw), lambda b, h: (b, 0, ATTN_HEADS + h)),
            pl.BlockSpec((None, seq, hw), lambda b, h: (b, 0, 2 * ATTN_HEADS + h)),
            _resident((4, ATTN_HEAD_DIM), lambda b, h: (0, 0)),
            _resident((1, hw), lambda b, h: (0, 0)),
            up_spec, up_spec, down_spec,
        ],
        out_specs=[pl.BlockSpec((None, seq, hw), lambda b, h: (b, 0, h)), up_spec, up_spec, down_spec],
        out_shape=[jax.ShapeDtypeStruct((bsz, seq, D_MODEL), BF16), jax.ShapeDtypeStruct(w1.shape, BF16),
                   jax.ShapeDtypeStruct(w3.shape, BF16), jax.ShapeDtypeStruct(w2.shape, BF16)],
        scratch_shapes=[pltpu.VMEM((seq, 2 * hw), BF16)],
        compiler_params=_params(("arbitrary", "arbitrary"), 48),
        name="diff_attention",
    )(qkv, qkv, qkv, lam, subln_g.reshape(1, hw), w1, w3, w2)


ROWS = 2 * CHUNK
PAIRS = SSM_GROUP // 2


def _lane_halves(e, o, lane):
    lo = jnp.where(lane < CHUNK, e, pltpu.roll(o, CHUNK, 1))
    hi = jnp.where(lane < CHUNK, pltpu.roll(e, CHUNK, 1), o)
    return lo, hi


def _s5in_kernel(x_ref, mod_ref, w_ref, o_ref, a_ref, s_ref):
    bsz = x_ref.shape[0]
    lane = lax.broadcasted_iota(jnp.int32, (SSM_GROUPS, LANES), 1)
    half = SSM_GROUPS * bsz
    n_grp = 2
    per = bsz // n_grp

    def project(k):
        seqs = slice(k * per, (k + 1) * per)
        shift = mod_ref[3, seqs, :][:, None, :]
        scale = mod_ref[4, seqs, :][:, None, :]
        h = (x_ref[seqs] * (1.0 + scale) + shift).astype(BF16).reshape(per * ROWS, D_MODEL)
        a_ref[:, k * per * LANES:(k + 1) * per * LANES] = lax.dot_general(
            w_ref[...], h, (((1,), (1,)), ((), ())), preferred_element_type=F32)

    def scatter(k):
        for q in range(PAIRS):
            for b in range(k * per, (k + 1) * per):
                x0 = a_ref[(2 * q) * SSM_GROUPS:(2 * q + 1) * SSM_GROUPS, b * LANES:(b + 1) * LANES]
                x1 = a_ref[(2 * q + 1) * SSM_GROUPS:(2 * q + 2) * SSM_GROUPS, b * LANES:(b + 1) * LANES]
                c0, c1 = _lane_halves(x0, x1, lane)
                s_ref[q, pl.ds(b, SSM_GROUPS, stride=bsz), :] = c0
                s_ref[q, pl.ds(half + b, SSM_GROUPS, stride=bsz), :] = c1

    project(0)
    for k in range(n_grp):
        if k + 1 < n_grp:
            project(k + 1)
        scatter(k)
    for c2 in range(2):
        for q in range(PAIRS):
            o_ref[c2, :, :, q * LANES:(q + 1) * LANES] = (
                s_ref[q, c2 * half:(c2 + 1) * half, :].reshape(SSM_GROUPS, bsz, LANES))


def _s5in(x, mods, w_t, *, layer):
    bsz, seq, _ = x.shape
    width = SSM_GROUP * CHUNK
    return pl.pallas_call(
        _s5in_kernel,
        grid=(seq // ROWS,),
        in_specs=[
            pl.BlockSpec((bsz, ROWS, D_MODEL), lambda j: (0, j, 0)),
            _resident((None, 9, bsz, D_MODEL), lambda j: (layer, 0, 0, 0)),
            _resident((D_MODEL, D_MODEL), lambda j: (0, 0)),
        ],
        out_specs=pl.BlockSpec((2, SSM_GROUPS, bsz, width), lambda j: (j, 0, 0, 0)),
        out_shape=jax.ShapeDtypeStruct((seq // CHUNK, SSM_GROUPS, bsz, width), F32),
        scratch_shapes=[
            pltpu.VMEM((D_MODEL, bsz * ROWS), F32),
            pltpu.VMEM((PAIRS, 2 * SSM_GROUPS * bsz, LANES), F32),
        ],
        compiler_params=_params(("arbitrary",), 40),
        name="s5_in",
    )(x, mods, w_t)


def _s5scan_kernel(u_ref, arr_ref, air_ref, arc_ref, aic_ref, ldt_ref, btr_ref, bti_ref,
                   ctr_ref, cti_ref, d_ref, y_ref, t_ref, ws_ref, wc_ref):
    n_chunks, bsz, width = u_ref.shape
    st = SSM_STATE
    dt = jnp.exp(ldt_ref[...])

    ar, ai = arr_ref[...], air_ref[...]
    mag = jnp.exp(ar * dt)
    abr, abi = mag * jnp.cos(ai * dt), mag * jnp.sin(ai * dt)
    den = ar * ar + ai * ai
    pr, pim = abr - 1.0, abi
    cfr, cfi = (pr * ar + pim * ai) / den, (pim * ar - pr * ai) / den
    btr, bti = btr_ref[...], bti_ref[...]
    bbr, bbi = cfr * btr - cfi * bti, cfr * bti + cfi * btr

    arc, aic = arc_ref[...], aic_ref[...]
    lane = lax.broadcasted_iota(jnp.int32, (1, LANES), 1)
    lag = (lane % CHUNK).astype(F32)
    first = lane < CHUNK
    ctr, cti = ctr_ref[...], cti_ref[...]

    def c_times_power(shift):
        e = lag + shift
        m = jnp.exp(arc * dt * e)
        er, ei = m * jnp.cos(aic * dt * e), m * jnp.sin(aic * dt * e)
        xr, xi = [], []
        for q in range(PAIRS):
            cr = jnp.where(first, ctr[:, 2 * q:2 * q + 1], ctr[:, 2 * q + 1:2 * q + 2])
            ci = jnp.where(first, cti[:, 2 * q:2 * q + 1], cti[:, 2 * q + 1:2 * q + 2])
            xr.append(cr * er - ci * ei)
            xi.append(cr * ei + ci * er)
        return jnp.concatenate(xr, axis=1), jnp.concatenate(xi, axis=1)

    xr, xi = c_times_power(0.0)
    hi = lax.Precision.HIGHEST
    kflat = (jnp.dot(bbr, xr, precision=hi, preferred_element_type=F32)
             - jnp.dot(bbi, xi, precision=hi, preferred_element_type=F32))
    prow = lax.broadcasted_iota(jnp.int32, kflat.shape, 0)
    plane = lax.broadcasted_iota(jnp.int32, kflat.shape, 1)
    kflat = kflat + jnp.where(plane == prow * CHUNK, d_ref[...], 0.0)

    srow = lax.broadcasted_iota(jnp.int32, (CHUNK, LANES), 0)
    keep = (lax.broadcasted_iota(jnp.int32, (CHUNK, LANES), 1) % CHUNK) >= srow
    for p in range(SSM_GROUP):
        rows = jnp.broadcast_to(kflat[p:p + 1, :], (CHUNK, width))
        for q in range(PAIRS):
            blk = pltpu.roll(rows[:, q * LANES:(q + 1) * LANES], 0, 1, stride=1, stride_axis=0)
            t_ref[p * CHUNK:(p + 1) * CHUNK, q * LANES:(q + 1) * LANES] = (
                jnp.where(keep, blk, 0.0).astype(BF16))

    e = (CHUNK - 1 - lax.broadcasted_iota(jnp.int32, (CHUNK, 1), 0)).astype(F32)
    m = jnp.exp(ar * dt * e)
    er, ei = m * jnp.cos(ai * dt * e), m * jnp.sin(ai * dt * e)
    zpad = jnp.zeros((CHUNK, LANES - st), F32)
    for p in range(SSM_GROUP):
        br, bi = bbr[p:p + 1, :], bbi[p:p + 1, :]
        ws_ref[p * CHUNK:(p + 1) * CHUNK, :] = jnp.concatenate(
            [er * br - ei * bi, zpad, er * bi + ei * br, zpad], axis=1).astype(BF16)

    pr_, pi_ = c_times_power(1.0)
    zrows = jnp.zeros((LANES - st, width), F32)
    wc_ref[...] = jnp.concatenate([pr_, zrows, -pi_, zrows], axis=0).astype(BF16)

    u = u_ref[...].reshape(n_chunks * bsz, width).astype(BF16)
    s = jnp.dot(u, ws_ref[...], preferred_element_type=F32)
    m64 = jnp.exp(ar * dt * CHUNK)
    zlane = jnp.zeros((1, LANES - st), F32)
    a64r = jnp.concatenate([m64 * jnp.cos(ai * dt * CHUNK), zlane], axis=1)
    a64i = jnp.concatenate([m64 * jnp.sin(ai * dt * CHUNK), zlane], axis=1)
    hr = jnp.zeros((bsz, LANES), F32)
    hi_ = jnp.zeros((bsz, LANES), F32)
    prev = []
    for c in range(n_chunks):
        prev.append(jnp.concatenate([hr, hi_], axis=1))
        sr, si = s[c * bsz:(c + 1) * bsz, :LANES], s[c * bsz:(c + 1) * bsz, LANES:]
        hr, hi_ = a64r * hr - a64i * hi_ + sr, a64r * hi_ + a64i * hr + si
    hprev = jnp.concatenate(prev, axis=0).astype(BF16)

    y = (jnp.dot(u, t_ref[...], preferred_element_type=F32)
         + jnp.dot(hprev, wc_ref[...], preferred_element_type=F32))
    y_ref[...] = y.reshape(n_chunks, bsz, width)


def _s5scan(u4, a_re, a_im, log_dt, b_re, b_im, c_re, c_im, d):
    n_chunks, groups, bsz, width = u4.shape
    st, pg = SSM_STATE, SSM_GROUP
    per_g = lambda *shape: pl.BlockSpec((None,) + shape, lambda g: (g,) + (0,) * len(shape))
    return pl.pallas_call(
        _s5scan_kernel,
        grid=(groups,),
        in_specs=[
            pl.BlockSpec((n_chunks, None, bsz, width), lambda g: (0, g, 0, 0)),
            per_g(1, st), per_g(1, st), per_g(st, 1), per_g(st, 1), per_g(1, 1),
            per_g(pg, st), per_g(pg, st), per_g(st, pg), per_g(st, pg), per_g(pg, 1),
        ],
        out_specs=pl.BlockSpec((n_chunks, None, bsz, width), lambda g: (0, g, 0, 0)),
        out_shape=jax.ShapeDtypeStruct(u4.shape, F32),
        scratch_shapes=[
            pltpu.VMEM((width, width), BF16),
            pltpu.VMEM((width, 2 * LANES), BF16),
            pltpu.VMEM((2 * LANES, width), BF16),
        ],
        compiler_params=_params(("arbitrary",), 32),
        name="s5_scan",
    )(u4,
      a_re.reshape(groups, 1, st), a_im.reshape(groups, 1, st),
      a_re.reshape(groups, st, 1), a_im.reshape(groups, st, 1),
      log_dt.reshape(groups, 1, 1),
      jnp.swapaxes(b_re, 1, 2), jnp.swapaxes(b_im, 1, 2),
      jnp.swapaxes(c_re, 1, 2), jnp.swapaxes(c_im, 1, 2),
      d.reshape(groups, pg, 1))


def _s5out_kernel(y_ref, x_ref, mod_ref, wg_ref, wo_ref, g_ref, b_ref, o_ref, a_ref, s_ref):
    bsz = x_ref.shape[0]
    half = SSM_GROUPS * bsz
    for c2 in range(2):
        for q in range(PAIRS):
            s_ref[q, c2 * half:(c2 + 1) * half, :] = (
                y_ref[c2, :, :, q * LANES:(q + 1) * LANES].reshape(half, LANES))
    lane = lax.broadcasted_iota(jnp.int32, (SSM_GROUPS, LANES), 1)
    n_grp = 4
    per = bsz // n_grp

    def gather(k):
        for q in range(PAIRS):
            for b in range(k * per, (k + 1) * per):
                c0 = s_ref[q, pl.ds(b, SSM_GROUPS, stride=bsz), :]
                c1 = s_ref[q, pl.ds(half + b, SSM_GROUPS, stride=bsz), :]
                x0, x1 = _lane_halves(c0, c1, lane)
                a_ref[(2 * q) * SSM_GROUPS:(2 * q + 1) * SSM_GROUPS, b * LANES:(b + 1) * LANES] = x0
                a_ref[(2 * q + 1) * SSM_GROUPS:(2 * q + 2) * SSM_GROUPS, b * LANES:(b + 1) * LANES] = x1
        return jax.nn.gelu(a_ref[:, k * per * LANES:(k + 1) * per * LANES], approximate=True)

    def gate_mm(z):
        return jnp.dot(wg_ref[...], z.astype(BF16), preferred_element_type=F32)

    def out_mm(z, gt):
        zz = (z * jax.nn.sigmoid(gt)).astype(BF16)
        return lax.dot_general(zz, wo_ref[...], (((0,), (0,)), ((), ())), preferred_element_type=F32)

    def finish(k, y):
        rows = slice(k * per, (k + 1) * per)
        y = y.reshape(per, ROWS, D_MODEL)
        gate = 1.0 + mod_ref[5, rows, :][:, None, :]
        o_ref[rows] = _layer_norm(ALPHA * x_ref[rows] + gate * y, g_ref[...], b_ref[...])

    zs, gs, ys = {}, {}, {}
    for step in range(n_grp + 3):
        if step < n_grp:
            zs[step] = gather(step)
        if 0 <= step - 1 < n_grp:
            gs[step - 1] = gate_mm(zs[step - 1])
        if 0 <= step - 2 < n_grp:
            ys[step - 2] = out_mm(zs.pop(step - 2), gs.pop(step - 2))
        if 0 <= step - 3 < n_grp:
            finish(step - 3, ys.pop(step - 3))


def _s5out(y4, x, mods, wg_t, wo, ln_g, ln_b, *, layer):
    bsz, seq, _ = x.shape
    width = SSM_GROUP * CHUNK
    return pl.pallas_call(
        _s5out_kernel,
        grid=(seq // ROWS,),
        in_specs=[
            pl.BlockSpec((2, SSM_GROUPS, bsz, width), lambda j: (j, 0, 0, 0)),
            pl.BlockSpec((bsz, ROWS, D_MODEL), lambda j: (0, j, 0)),
            _resident((None, 9, bsz, D_MODEL), lambda j: (layer, 0, 0, 0)),
            _resident((D_MODEL, D_MODEL), lambda j: (0, 0)),
            _resident((D_MODEL, D_MODEL), lambda j: (0, 0)),
            _resident((None, None, 1, D_MODEL), lambda j: (layer, 1, 0, 0)),
            _resident((None, None, 1, D_MODEL), lambda j: (layer, 1, 0, 0)),
        ],
        out_specs=pl.BlockSpec((bsz, ROWS, D_MODEL), lambda j: (0, j, 0)),
        out_shape=jax.ShapeDtypeStruct(x.shape, F32),
        scratch_shapes=[
            pltpu.VMEM((D_MODEL, bsz * ROWS), F32),
            pltpu.VMEM((PAIRS, 2 * SSM_GROUPS * bsz, LANES), F32),
        ],
        compiler_params=_params(("arbitrary",), 48),
        name="s5_out",
    )(y4, x, mods, wg_t, wo, ln_g, ln_b)


def kernel(x, c, ada_w, ada_b, ln_g, ln_b, ffn_w1, ffn_w3, ffn_w2, attn_w_in, attn_lam, attn_subln_g, attn_w_out, ssm_w_in, ssm_a_re, ssm_a_im, ssm_log_dt, ssm_b_re, ssm_b_im, ssm_c_re, ssm_c_im, ssm_d, ssm_w_gate, ssm_w_out):
    mods = _ada(c, ada_w, ada_b)
    lng = ln_g.reshape(DEPTH, 3, 1, D_MODEL)
    lnb = ln_b.reshape(DEPTH, 3, 1, D_MODEL)
    w1, w3, w2 = ffn_w1, ffn_w3, ffn_w2
    gp = (SSM_GROUPS, SSM_GROUP)

    for layer in range(DEPTH):
        i = layer // 2
        proj = None
        x = _ffn(x, mods, w1, w3, w2, lng, lnb, layer=layer, half=0, sub=0)
        if layer % 2 == 0:
            lam_init = 0.8 - 0.6 * math.exp(-0.3 * layer)
            qkv = _modproj(x, mods, attn_w_in[i].astype(BF16), layer=layer, sub=1)
            o, w1, w3, w2 = _attention(qkv, attn_lam[i], attn_subln_g[i], ffn_w1, ffn_w3, ffn_w2,
                                       lam_init=lam_init)
            proj = (o, attn_w_out[i].astype(BF16), 1)
        else:
            rows_pg = lambda w: w.reshape(*gp, D_MODEL).transpose(1, 0, 2).reshape(D_MODEL, D_MODEL)
            w_in_t = rows_pg(ssm_w_in[i].astype(BF16).T)
            wg_t = rows_pg(rows_pg(ssm_w_gate[i].astype(BF16)).T)
            wo = rows_pg(ssm_w_out[i].astype(BF16))
            u4 = _s5in(x, mods, w_in_t, layer=layer)
            y4 = _s5scan(u4, ssm_a_re[i], ssm_a_im[i], ssm_log_dt[i], ssm_b_re[i], ssm_b_im[i],
                         ssm_c_re[i], ssm_c_im[i], ssm_d[i])
            x = _s5out(y4, x, mods, wg_t, wo, lng, lnb, layer=layer)
        x = _ffn(x, mods, w1, w3, w2, lng, lnb, layer=layer, half=1, sub=2, proj=proj)
    return x
```
